```python
import math
import jax, jax.numpy as jnp
from jax import lax
import numpy as np

D_MODEL = 2048
BATCH = 2
SEQ = 8192
DEPTH = 1

EPS = 1e-6
D_PLE = 256
RET_WIDTH = D_MODEL // 2
RET_HEAD_DIM = 128
RET_HEADS = RET_WIDTH // RET_HEAD_DIM
RET_CHUNK = 128
ROPE_BASE = 10000.0
SSD_WIDTH = D_MODEL - RET_WIDTH
SSD_HEAD_DIM = 64
SSD_HEADS = SSD_WIDTH // SSD_HEAD_DIM
SSD_GROUPS = 2
SSD_HEADS_PER_GROUP = SSD_HEADS // SSD_GROUPS
SSD_STATE = 128
SSD_CONV = 5
SSD_CHUNK = 128
SSD_CONV_DIM = SSD_WIDTH + 2 * SSD_GROUPS * SSD_STATE
D_MIX = RET_WIDTH + SSD_WIDTH
D_FF = (11 * D_MODEL) // 4
FFN_CONV = 3
OFF_Q = 0
OFF_K = OFF_Q + RET_WIDTH
OFF_V = OFF_K + RET_WIDTH
OFF_G = OFF_V + RET_WIDTH
OFF_Z = OFF_G + RET_WIDTH
OFF_XBC = OFF_Z + SSD_WIDTH
OFF_DT = OFF_XBC + SSD_CONV_DIM
N_IN = OFF_DT + 2 * SSD_HEADS

kernel_name = 'hybrid_retention_ssd_encoder_layer'


def rmsnorm(t, w):
    tf = t.astype(jnp.float32)
    y = tf * lax.rsqrt(jnp.mean(tf * tf, axis=-1, keepdims=True) + EPS)
    return (y * w.astype(jnp.float32)).astype(t.dtype)


def rotary(t, positions):
    half = t.shape[-1] // 2
    inv_freq = ROPE_BASE ** (-jnp.arange(half, dtype=jnp.float32) / half)
    ang = positions.astype(jnp.float32)[..., None] * inv_freq
    cos = jnp.cos(ang)[:, :, None, :].astype(t.dtype)
    sin = jnp.sin(ang)[:, :, None, :].astype(t.dtype)
    t1, t2 = t[..., :half], t[..., half:]
    return jnp.concatenate([t1 * cos - t2 * sin, t1 * sin + t2 * cos], axis=-1)


def depthwise_conv(t, w, b):
    pad = w.shape[0] // 2
    y = lax.conv_general_dilated(t, w[:, None, :].astype(t.dtype), window_strides=(1,),
                                 padding=[(pad, pad)], dimension_numbers=('NWC', 'WIO', 'NWC'),
                                 feature_group_count=t.shape[-1])
    return y + b.astype(t.dtype)


def bidir_retention(q, k, v):
    b, L, H, dk = q.shape
    dv = v.shape[-1]
    C = RET_CHUNK
    n = L // C
    dt = q.dtype
    hh = jnp.arange(H, dtype=jnp.float32)
    lf = jnp.log1p(-jnp.exp2(-5.0 - hh))
    lb = jnp.log1p(-jnp.exp2(-5.5 - hh))
    idx = jnp.arange(C, dtype=jnp.float32)
    dist = idx[:, None] - idx[None, :]
    adist = jnp.abs(dist)
    mask = jnp.where(dist >= 0, jnp.exp(lf[:, None, None] * adist),
                     jnp.exp(lb[:, None, None] * adist)).astype(dt)
    q = q.reshape(b, n, C, H, dk)
    k = k.reshape(b, n, C, H, dk)
    v = v.reshape(b, n, C, H, dv)
    s = jnp.einsum('bnihd,bnjhd->bnhij', q, k) * mask
    intra = jnp.einsum('bnhij,bnjhv->bnihv', s, v)
    k_f = jnp.exp(lf[None, :] * (C - 1.0 - idx)[:, None]).astype(dt)
    k_b = jnp.exp(lb[None, :] * idx[:, None]).astype(dt)
    q_f = jnp.exp(lf[None, :] * (idx + 1.0)[:, None]).astype(dt)
    q_b = jnp.exp(lb[None, :] * (C - idx)[:, None]).astype(dt)
    dec_f = jnp.exp(lf * C).astype(dt)[:, None, None]
    dec_b = jnp.exp(lb * C).astype(dt)[:, None, None]
    kv_f = jnp.einsum('bnjhd,jh,bnjhv->nbhdv', k, k_f, v)
    kv_b = jnp.einsum('bnjhd,jh,bnjhv->nbhdv', k, k_b, v)

    def step_f(r, kv):
        return r * dec_f + kv, r

    def step_b(r, kv):
        return r * dec_b + kv, r

    r0 = jnp.zeros((b, H, dk, dv), dt)
    _, r_f = lax.scan(step_f, r0, kv_f)
    _, r_b = lax.scan(step_b, r0, kv_b, reverse=True)
    inter = (jnp.einsum('bnihd,ih,nbhdv->bnihv', q, q_f, r_f)
             + jnp.einsum('bnihd,ih,nbhdv->bnihv', q, q_b, r_b))
    return (intra + inter).reshape(b, L, H, dv)


def retention_group(proj, positions, norm_w):
    b, L, _ = proj.shape
    shp = (b, L, RET_HEADS, RET_HEAD_DIM)
    q = rotary(proj[..., OFF_Q:OFF_K].reshape(shp), positions)
    k = rotary(proj[..., OFF_K:OFF_V].reshape(shp), positions) * (RET_HEAD_DIM ** -0.5)
    v = proj[..., OFF_V:OFF_G].reshape(shp)
    g = proj[..., OFF_G:OFF_Z]
    o = bidir_retention(q, k, v)
    o = rmsnorm(o, norm_w.reshape(RET_HEADS, RET_HEAD_DIM)).reshape(b, L, RET_WIDTH)
    return jax.nn.silu(g) * o


def ssd_scan(x, dt, a, bm, cm):
    b, L, G, E, P = x.shape
    N = bm.shape[-1]
    Q = SSD_CHUNK
    c = L // Q
    x = x.reshape(b, c, Q, G, E, P)
    dt = dt.reshape(b, c, Q, G, E)
    bm = bm.reshape(b, c, Q, G, N)
    cm = cm.reshape(b, c, Q, G, N)
    xdt = x * dt[..., None]
    a_cs = jnp.cumsum(jnp.moveaxis(dt.astype(jnp.float32) * a, 2, -1), axis=-1)
    seg = a_cs[..., :, None] - a_cs[..., None, :]
    tril = jnp.tril(jnp.ones((Q, Q), dtype=bool))
    lmat = jnp.exp(jnp.where(tril, seg, -jnp.inf)).astype(x.dtype)
    cb = jnp.einsum('bclgn,bcsgn->bcgls', cm, bm)
    y_diag = jnp.einsum('bcgls,bcgels,bcsgep->bclgep', cb, lmat, xdt)
    decay_states = jnp.exp(a_cs[..., -1:] - a_cs).astype(x.dtype)
    states = jnp.einsum('bcsgn,bcges,bcsgep->cbgepn', bm, decay_states, xdt)
    chunk_decay = jnp.moveaxis(jnp.exp(a_cs[..., -1]).astype(x.dtype), 1, 0)[..., None, None]

    def step(h, inp):
        st, dec = inp
        return h * dec + st, h

    _, h_in = lax.scan(step, jnp.zeros_like(states[0]), (states, chunk_decay))
    y_off = jnp.einsum('bclgn,cbgepn,bcgel->bclgep', cm, h_in, jnp.exp(a_cs).astype(x.dtype))
    return (y_diag + y_off).astype(x.dtype).reshape(b, L, G, E, P)


def ssd_group(proj, conv_w, conv_b, dt_bias, a_log, d_skip, norm_w):
    b, L, _ = proj.shape
    G, E, P, N = SSD_GROUPS, SSD_HEADS_PER_GROUP, SSD_HEAD_DIM, SSD_STATE
    z = proj[..., OFF_Z:OFF_XBC]
    xbc = jax.nn.silu(depthwise_conv(proj[..., OFF_XBC:OFF_DT], conv_w, conv_b))
    xs = xbc[..., :SSD_WIDTH].reshape(b, L, G, E, P)
    bm = xbc[..., SSD_WIDTH:SSD_WIDTH + G * N].reshape(b, L, G, N)
    cm = xbc[..., SSD_WIDTH + G * N:].reshape(b, L, G, N)
    dt_raw = proj[..., OFF_DT:N_IN]
    dt_f = jax.nn.softplus(dt_raw[..., :SSD_HEADS] + dt_bias[0]).reshape(b, L, G, E)
    dt_b = jax.nn.softplus(dt_raw[..., SSD_HEADS:] + dt_bias[1]).reshape(b, L, G, E)
    a_f = -jnp.exp(a_log[0].astype(jnp.float32)).reshape(G, E)
    a_b = -jnp.exp(a_log[1].astype(jnp.float32)).reshape(G, E)
    y_f = ssd_scan(xs, dt_f, a_f, bm, cm)
    y_b = jnp.flip(ssd_scan(jnp.flip(xs, 1), jnp.flip(dt_b, 1), a_b,
                            jnp.flip(bm, 1), jnp.flip(cm, 1)), 1)
    y_self = jnp.einsum('blgn,blgn->blg', cm, bm)[..., None, None] * dt_b[..., None] * xs
    y = y_f + y_b - y_self + d_skip.reshape(G, E)[..., None] * xs
    y = (y.reshape(b, L, SSD_WIDTH) * jax.nn.silu(z)).reshape(b, L, G, SSD_WIDTH // G)
    return rmsnorm(y, norm_w.reshape(G, SSD_WIDTH // G)).reshape(b, L, SSD_WIDTH)


def conv_glu_ffn(h, w_gate, w_up, conv_w, conv_b, w_down):
    gate = depthwise_conv(h @ w_gate, conv_w, conv_b)
    return (jax.nn.gelu(gate, approximate=True) * (h @ w_up)) @ w_down


def setup_inputs(seed: int = 0) -> dict:
    key = jax.random.key(seed)
    ks = iter(jax.random.split(key, 32))

    def nrm(shape, scale):
        return jax.random.normal(next(ks), shape, jnp.float32) * scale

    def gain(shape):
        return 1.0 + nrm(shape, 0.02)

    x = nrm((BATCH, SEQ, D_MODEL), 1.0)
    p = nrm((DEPTH, BATCH, SEQ, D_PLE), 1.0)
    positions = (jnp.arange(SEQ, dtype=jnp.int32)[None, :]
                 + jax.random.randint(next(ks), (BATCH, 1), 0, 1024, dtype=jnp.int32))
    u = jax.random.uniform(next(ks), (DEPTH, 2, SSD_HEADS), jnp.float32)
    dt0 = jnp.exp(u * (math.log(0.1) - math.log(0.001)) + math.log(0.001))
    ssd_dt_bias = dt0 + jnp.log(-jnp.expm1(-dt0))
    ssd_a_log = jnp.log(jax.random.uniform(next(ks), (DEPTH, 2, SSD_HEADS), jnp.float32, 1.0, 16.0))
    return {
        'x': x,
        'p': p,
        'positions': positions,
        'norm_mix_w': gain((DEPTH, D_MODEL)),
        'w_in': nrm((DEPTH, D_MODEL, N_IN), D_MODEL ** -0.5),
        'ret_norm_w': gain((DEPTH, RET_WIDTH)),
        'ssd_conv_w': nrm((DEPTH, SSD_CONV, SSD_CONV_DIM), SSD_CONV ** -0.5),
        'ssd_conv_b': nrm((DEPTH, SSD_CONV_DIM), 0.02),
        'ssd_dt_bias': ssd_dt_bias,
        'ssd_a_log': ssd_a_log,
        'ssd_d': 1.0 + nrm((DEPTH, SSD_HEADS), 0.1),
        'ssd_norm_w': gain((DEPTH, SSD_WIDTH)),
        'w_out': nrm((DEPTH, D_MIX, D_MODEL), D_MIX ** -0.5),
        'norm_ffn_w': gain((DEPTH, D_MODEL)),
        'ffn_w_gate': nrm((DEPTH, D_MODEL, D_FF), D_MODEL ** -0.5),
        'ffn_w_up': nrm((DEPTH, D_MODEL, D_FF), D_MODEL ** -0.5),
        'ffn_conv_w': nrm((DEPTH, FFN_CONV, D_FF), FFN_CONV ** -0.5),
        'ffn_conv_b': nrm((DEPTH, D_FF), 0.02),
        'ffn_w_down': nrm((DEPTH, D_FF, D_MODEL), D_FF ** -0.5),
        'ple_norm_w': gain((DEPTH, D_MODEL)),
        'ple_w_gate': nrm((DEPTH, D_MODEL, D_MODEL), D_MODEL ** -0.5),
        'ple_b_gate': nrm((DEPTH, D_MODEL), 0.02),
        'ple_w_proj': nrm((DEPTH, D_PLE, D_MODEL), D_PLE ** -0.5),
        'final_norm_w': gain((D_MODEL,)),
    }


def reference(x, p, positions, norm_mix_w, w_in, ret_norm_w, ssd_conv_w, ssd_conv_b,
              ssd_dt_bias, ssd_a_log, ssd_d, ssd_norm_w, w_out, norm_ffn_w, ffn_w_gate,
              ffn_w_up, ffn_conv_w, ffn_conv_b, ffn_w_down, ple_norm_w, ple_w_gate,
              ple_b_gate, ple_w_proj, final_norm_w):
    h = x
    for i in range(DEPTH):
        hn = rmsnorm(h, norm_mix_w[i])
        proj = hn @ w_in[i]
        ret = retention_group(proj, positions, ret_norm_w[i])
        ssd = ssd_group(proj, ssd_conv_w[i], ssd_conv_b[i], ssd_dt_bias[i], ssd_a_log[i],
                        ssd_d[i], ssd_norm_w[i])
        h = h + jnp.concatenate([ret, ssd], axis=-1) @ w_out[i]
        hn = rmsnorm(h, norm_ffn_w[i])
        h = h + conv_glu_ffn(hn, ffn_w_gate[i], ffn_w_up[i], ffn_conv_w[i], ffn_conv_b[i], ffn_w_down[i])
        gate = jax.nn.sigmoid(rmsnorm(h, ple_norm_w[i]) @ ple_w_gate[i] + ple_b_gate[i])
        h = h + gate * (p[i] @ ple_w_proj[i])
    return rmsnorm(h, final_norm_w)
```

```python
import functools
import math

import numpy as np
import jax
import jax.numpy as jnp
from jax import lax
from jax.experimental import pallas as pl
from jax.experimental.pallas import tpu as pltpu

F32 = jnp.float32
BF16 = jnp.bfloat16

D_MODEL = 2048
EPS = 1e-6
D_PLE = 256
RET_WIDTH = D_MODEL // 2
RET_HEAD_DIM = 128
RET_HEADS = RET_WIDTH // RET_HEAD_DIM
ROPE_BASE = 10000.0
SSD_WIDTH = D_MODEL - RET_WIDTH
SSD_HEAD_DIM = 64
SSD_HEADS = SSD_WIDTH // SSD_HEAD_DIM
SSD_GROUPS = 2
SSD_HEADS_PER_GROUP = SSD_HEADS // SSD_GROUPS
SSD_STATE = 128
SSD_CONV = 5
SSD_BC = 2 * SSD_GROUPS * SSD_STATE
SSD_CONV_DIM = SSD_WIDTH + SSD_BC
D_FF = (11 * D_MODEL) // 4
FFN_CONV = 3
N_MAIN = 4 * RET_WIDTH + SSD_WIDTH + SSD_CONV_DIM
N_DT = 2 * SSD_HEADS

CHUNK = 128
LANES = 128
HALO = 16
GROUP_W = SSD_WIDTH // SSD_GROUPS

COL_Q, COL_K, COL_V, COL_G, COL_Z, COL_XS = 0, 1, 2, 3, 4, 5
COL_BC = (5 * RET_WIDTH + SSD_WIDTH) // SSD_BC

VMEM_LIMIT = 56 * 1024 * 1024


def _cparams(n_axes):
    return pltpu.CompilerParams(dimension_semantics=("arbitrary",) * n_axes,
                                vmem_limit_bytes=VMEM_LIMIT)


def _rms(xf, w_row):
    ms = jnp.mean(xf * xf, axis=-1, keepdims=True)
    return xf * lax.rsqrt(ms + EPS) * w_row


def _silu(x):
    return x * jax.nn.sigmoid(x)


def _softplus(x):
    return jnp.maximum(x, 0.0) + jnp.log1p(jnp.exp(-jnp.abs(x)))


def _gelu_tanh(x):
    c = math.sqrt(2.0 / math.pi)
    return 0.5 * x * (1.0 + jnp.tanh(c * (x + 0.044715 * (x * x * x))))


def _dot(a, b):
    return jnp.dot(a, b, preferred_element_type=F32)


def _dot_nt(a, b):
    return lax.dot_general(a, b, (((1,), (1,)), ((), ())), preferred_element_type=F32)


def _split3(a):
    hi = a.astype(BF16)
    r1 = a - hi.astype(F32)
    mid = r1.astype(BF16)
    lo = (r1 - mid.astype(F32)).astype(BF16)
    return hi, mid, lo


def _dot_exact_lhs(m01, a):
    hi, mid, lo = _split3(a)
    return _dot(m01, hi) + _dot(m01, mid) + _dot(m01, lo)


def _dot_exact_rhs(a, m01):
    hi, mid, lo = _split3(a)
    return _dot(hi, m01) + _dot(mid, m01) + _dot(lo, m01)


def _col_bcast(row):
    return jnp.broadcast_to(row, (LANES, LANES)).T


def _tri_masks():
    r = lax.broadcasted_iota(jnp.int32, (CHUNK, CHUNK), 0)
    c = lax.broadcasted_iota(jnp.int32, (CHUNK, CHUNK), 1)
    lower = r >= c
    tri = jnp.where(lower, 1.0, 0.0).astype(BF16)
    tri_t = jnp.where(r <= c, 1.0, 0.0).astype(BF16)
    return lower, tri, tri_t


def _inproj_kernel(x_ref, nw_ref, w_ref, wdt_ref, proj_ref, dt_ref, hn_ref):
    @pl.when(pl.program_id(1) == 0)
    def _():
        hn = _rms(x_ref[...], nw_ref[...]).astype(BF16)
        hn_ref[...] = hn
        dt_ref[...] = _dot(hn, wdt_ref[...])

    proj_ref[...] = _dot(hn_ref[...], w_ref[...]).astype(BF16)


def _in_projection(x2, norm_w, w_main, w_dt, tm=1024, tn=512):
    t = x2.shape[0]
    return pl.pallas_call(
        _inproj_kernel,
        grid=(t // tm, N_MAIN // tn),
        in_specs=[
            pl.BlockSpec((tm, D_MODEL), lambda i, j: (i, 0)),
            pl.BlockSpec((1, D_MODEL), lambda i, j: (0, 0)),
            pl.BlockSpec((D_MODEL, tn), lambda i, j: (0, j)),
            pl.BlockSpec((D_MODEL, LANES), lambda i, j: (0, 0)),
        ],
        out_specs=[
            pl.BlockSpec((tm, tn), lambda i, j: (i, j)),
            pl.BlockSpec((tm, LANES), lambda i, j: (i, 0)),
        ],
        out_shape=[
            jax.ShapeDtypeStruct((t, N_MAIN), BF16),
            jax.ShapeDtypeStruct((t, LANES), F32),
        ],
        scratch_shapes=[pltpu.VMEM((tm, D_MODEL), BF16)],
        compiler_params=_cparams(2),
        name="in_projection",
    )(x2, norm_w, w_main, w_dt)


def _conv5_silu(ext_ref, prev, cur, nxt, w_ref, b_ref, has_prev, has_next):
    ext_ref[0:HALO, :] = prev.astype(F32) * has_prev
    ext_ref[HALO:HALO + CHUNK, :] = cur.astype(F32)
    ext_ref[HALO + CHUNK:, :] = nxt.astype(F32) * has_next
    pad = SSD_CONV // 2
    acc = b_ref[...] + w_ref[pad:pad + 1, :] * ext_ref[HALO:HALO + CHUNK, :]
    for j in range(SSD_CONV):
        if j != pad:
            o = HALO + j - pad
            acc = acc + w_ref[j:j + 1, :] * ext_ref[o:o + CHUNK, :]
    return _silu(acc)


def _expand_rows(w, tot_row, expand01):
    stacked = jnp.concatenate([w, jnp.broadcast_to(tot_row, (8, LANES))], axis=0)
    e = _dot_exact_rhs(stacked, expand01)
    return e[:CHUNK], e[CHUNK:CHUNK + 1]


def _state_increment(bm_f32, xw):
    parts = []
    for g in range(SSD_GROUPS):
        bm_t = bm_f32[:, g * SSD_STATE:(g + 1) * SSD_STATE].T.astype(BF16)
        parts.append(_dot(bm_t, xw[:, g * GROUP_W:(g + 1) * GROUP_W]))
    return jnp.concatenate(parts, axis=1)


def _prep_kernel(ret_dec_b,
                 pos_ref, q_ref, k_ref, v_ref,
                 xs_ref, xsp_ref, xsn_ref, bc_ref, bcp_ref, bcn_ref, dt_ref,
                 cwx_ref, cbx_ref, cwb_ref, cbb_ref, dtbias_ref, arow_ref,
                 invfreq_ref, sign_ref, kb_ref, expand_ref,
                 qr_ref, kr_ref, xbc_ref, rb_ref, sb_ref,
                 rb_state, sb_state, ext_xs, ext_bc):
    i = pl.program_id(1)
    nc = pl.num_programs(1)

    @pl.when(i == 0)
    def _():
        rb_state[...] = jnp.zeros_like(rb_state)
        sb_state[...] = jnp.zeros_like(sb_state)

    has_next = (i > 0).astype(F32)
    has_prev = (i < nc - 1).astype(F32)

    pos_col = _col_bcast(pos_ref[...].astype(F32))
    ang = pos_col * invfreq_ref[...]
    cosv = jnp.cos(ang)
    sinv = jnp.sin(ang) * sign_ref[...]

    def rot(t):
        return t * cosv + pltpu.roll(t, RET_HEAD_DIM // 2, axis=1) * sinv

    kscale = RET_HEAD_DIM ** -0.5
    for h in range(RET_HEADS):
        sl = slice(h * RET_HEAD_DIM, (h + 1) * RET_HEAD_DIM)
        qr_ref[:, sl] = rot(q_ref[:, sl].astype(F32)).astype(BF16)
        kr = rot(k_ref[:, sl].astype(F32)) * kscale
        kr_ref[:, sl] = kr.astype(BF16)
        rb_ref[h] = rb_state[h].astype(BF16)
        kt = (kr.T * kb_ref[h:h + 1, :]).astype(BF16)
        rb_state[h] = rb_state[h] * ret_dec_b[h] + _dot(kt, v_ref[:, sl])

    xs = _conv5_silu(ext_xs, xsp_ref[...], xs_ref[...], xsn_ref[...], cwx_ref, cbx_ref,
                     has_prev, has_next)
    bc = _conv5_silu(ext_bc, bcp_ref[...], bc_ref[...], bcn_ref[...], cwb_ref, cbb_ref,
                     has_prev, has_next)
    xbc_ref[:, :SSD_WIDTH] = xs.astype(BF16)
    xbc_ref[:, SSD_WIDTH:] = bc.astype(BF16)

    _, _, tri_t = _tri_masks()
    dt = _softplus(dt_ref[...] + dtbias_ref[...])
    rcs = _dot_exact_lhs(tri_t, dt * arow_ref[...])
    tot = rcs[0:1, :]
    w = jnp.exp(tot - rcs) * dt
    wexp, cdec = _expand_rows(w, jnp.exp(tot), expand_ref[...])
    xw = (xs * wexp).astype(BF16)
    ds = _state_increment(bc[:, :SSD_GROUPS * SSD_STATE], xw)
    sb_ref[...] = sb_state[...].astype(BF16)
    sb_state[...] = sb_state[...] * cdec + ds


def _prep_call(proj3, dt3, pos4, consts, ret_dec_b):
    b, l, _ = proj3.shape
    nc = l // CHUNK
    rows16 = l // HALO
    per16 = CHUNK // HALO

    def cix(i):
        return nc - 1 - i

    def col(cb):
        return lambda bi, i: (bi, cix(i), cb)

    def prev_halo(cb):
        return lambda bi, i: (bi, jnp.maximum(cix(i) * per16 - 1, 0), cb)

    def next_halo(cb):
        return lambda bi, i: (bi, jnp.minimum((cix(i) + 1) * per16, rows16 - 1), cb)

    def const(shape):
        return pl.BlockSpec(shape, lambda bi, i: (0,) * len(shape))

    in_specs = [
        pl.BlockSpec((None, None, 1, CHUNK), lambda bi, i: (bi, cix(i), 0, 0)),
        pl.BlockSpec((None, CHUNK, RET_WIDTH), col(COL_Q)),
        pl.BlockSpec((None, CHUNK, RET_WIDTH), col(COL_K)),
        pl.BlockSpec((None, CHUNK, RET_WIDTH), col(COL_V)),
        pl.BlockSpec((None, CHUNK, SSD_WIDTH), col(COL_XS)),
        pl.BlockSpec((None, HALO, SSD_WIDTH), prev_halo(COL_XS)),
        pl.BlockSpec((None, HALO, SSD_WIDTH), next_halo(COL_XS)),
        pl.BlockSpec((None, CHUNK, SSD_BC), col(COL_BC)),
        pl.BlockSpec((None, HALO, SSD_BC), prev_halo(COL_BC)),
        pl.BlockSpec((None, HALO, SSD_BC), next_halo(COL_BC)),
        pl.BlockSpec((None, CHUNK, LANES), col(0)),
        const((SSD_CONV, SSD_WIDTH)), const((1, SSD_WIDTH)),
        const((SSD_CONV, SSD_BC)), const((1, SSD_BC)),
        const((1, LANES)), const((1, LANES)),
        const((1, LANES)), const((1, LANES)),
        const((RET_HEADS, LANES)),
        const((LANES, SSD_WIDTH)),
    ]
    out_specs = [
        pl.BlockSpec((None, CHUNK, RET_WIDTH), col(0)),
        pl.BlockSpec((None, CHUNK, RET_WIDTH), col(0)),
        pl.BlockSpec((None, CHUNK, SSD_CONV_DIM), col(0)),
        pl.BlockSpec((None, None, RET_HEADS, RET_HEAD_DIM, RET_HEAD_DIM),
                     lambda bi, i: (bi, cix(i), 0, 0, 0)),
        pl.BlockSpec((None, None, SSD_STATE, SSD_WIDTH), lambda bi, i: (bi, cix(i), 0, 0)),
    ]
    out_shape = [
        jax.ShapeDtypeStruct((b, l, RET_WIDTH), BF16),
        jax.ShapeDtypeStruct((b, l, RET_WIDTH), BF16),
        jax.ShapeDtypeStruct((b, l, SSD_CONV_DIM), BF16),
        jax.ShapeDtypeStruct((b, nc, RET_HEADS, RET_HEAD_DIM, RET_HEAD_DIM), BF16),
        jax.ShapeDtypeStruct((b, nc, SSD_STATE, SSD_WIDTH), BF16),
    ]
    scratch = [
        pltpu.VMEM((RET_HEADS, RET_HEAD_DIM, RET_HEAD_DIM), F32),
        pltpu.VMEM((SSD_STATE, SSD_WIDTH), F32),
        pltpu.VMEM((CHUNK + 2 * HALO, SSD_WIDTH), F32),
        pltpu.VMEM((CHUNK + 2 * HALO, SSD_BC), F32),
    ]
    return pl.pallas_call(
        functools.partial(_prep_kernel, ret_dec_b),
        grid=(b, nc),
        in_specs=in_specs,
        out_specs=out_specs,
        out_shape=out_shape,
        scratch_shapes=scratch,
        compiler_params=_cparams(2),
        name="reverse_sweep",
    )(pos4, proj3, proj3, proj3, proj3, proj3, proj3, proj3, proj3, proj3, dt3,
      consts["conv_w_xs"], consts["conv_b_xs"], consts["conv_w_bc"], consts["conv_b_bc"],
      consts["dt_bias"], consts["a_row"], consts["inv_freq"], consts["rot_sign"],
      consts["ret_kb"], consts["expand_b"])


def _mix_kernel(ret_dec_f,
                qr_ref, kr_ref, v_ref, g_ref, z_ref, xbc_ref, dt_ref, rb_ref, sb_ref,
                mask_ref, qf_ref, qb_ref, kf_ref, retnw_ref,
                dtbias_ref, arow_ref, expand_ref, dexp_ref, ssdnw_ref,
                out_ref,
                rf_state, sf_state):
    i = pl.program_id(1)

    @pl.when(i == 0)
    def _():
        rf_state[...] = jnp.zeros_like(rf_state)
        sf_state[...] = jnp.zeros_like(sf_state)

    for h in range(RET_HEADS):
        sl = slice(h * RET_HEAD_DIM, (h + 1) * RET_HEAD_DIM)
        qh = qr_ref[:, sl]
        kh = kr_ref[:, sl]
        vh = v_ref[:, sl]
        qf32 = qh.astype(F32)
        s = (_dot_nt(qh, kh) * mask_ref[h]).astype(BF16)
        lhs = jnp.concatenate([s, (qf32 * qf_ref[h]).astype(BF16),
                               (qf32 * qb_ref[h]).astype(BF16)], axis=1)
        rhs = jnp.concatenate([vh, rf_state[h].astype(BF16), rb_ref[h]], axis=0)
        o = _dot(lhs, rhs)
        kt = (kh.astype(F32).T * kf_ref[h:h + 1, :]).astype(BF16)
        rf_state[h] = rf_state[h] * ret_dec_f[h] + _dot(kt, vh)
        o = _rms(o, retnw_ref[:, sl])
        out_ref[:, sl] = (_silu(g_ref[:, sl].astype(F32)) * o).astype(BF16)

    lower, tri, tri_t = _tri_masks()
    dt = _softplus(dt_ref[...] + dtbias_ref[...])
    a = dt * arow_ref[...]
    lane = lax.broadcasted_iota(jnp.int32, (CHUNK, LANES), 1)
    prefix = _dot_exact_lhs(tri, a)
    acs = jnp.where(lane < SSD_HEADS, prefix, _dot_exact_lhs(tri_t, a))
    acs_t = acs.T
    dt_t = dt.T
    xs16 = xbc_ref[:, :SSD_WIDTH]
    bm = xbc_ref[:, SSD_WIDTH:SSD_WIDTH + SSD_GROUPS * SSD_STATE]
    cm = xbc_ref[:, SSD_WIDTH + SSD_GROUPS * SSD_STATE:]
    lane_lo = lane < SSD_HEAD_DIM

    ys = []
    for g in range(SSD_GROUPS):
        gs = slice(g * SSD_STATE, (g + 1) * SSD_STATE)
        cm_g = cm[:, gs]
        cm_f32 = cm_g.astype(F32)
        cb = _dot_nt(cm_g, bm[:, gs])
        for pr in range(SSD_HEADS_PER_GROUP // 2):
            ps = slice(g * GROUP_W + pr * LANES, g * GROUP_W + (pr + 1) * LANES)
            rhs = jnp.concatenate([xs16[:, ps], sf_state[:, ps].astype(BF16), sb_ref[:, ps]],
                                  axis=0)
            pair = []
            for sub in range(2):
                e = g * SSD_HEADS_PER_GROUP + 2 * pr + sub
                row_f = acs_t[e:e + 1, :]
                row_b = acs_t[SSD_HEADS + e:SSD_HEADS + e + 1, :]
                col_f = _col_bcast(row_f)
                col_b = _col_bcast(row_b)
                arg = jnp.where(lower, col_f - row_f, col_b - row_b)
                wrow = jnp.where(lower, dt_t[e:e + 1, :], dt_t[SSD_HEADS + e:SSD_HEADS + e + 1, :])
                m = (cb * jnp.exp(arg) * wrow).astype(BF16)
                lhs = jnp.concatenate([m, (cm_f32 * jnp.exp(col_f)).astype(BF16),
                                       (cm_f32 * jnp.exp(col_b)).astype(BF16)], axis=1)
                pair.append(_dot(lhs, rhs))
            ys.append(jnp.where(lane_lo, pair[0], pair[1]))
    y = jnp.concatenate(ys, axis=1)

    xs = xs16.astype(F32)
    y = (y + dexp_ref[...] * xs) * _silu(z_ref[...].astype(F32))
    for g in range(SSD_GROUPS):
        gs = slice(g * GROUP_W, (g + 1) * GROUP_W)
        out_ref[:, RET_WIDTH + g * GROUP_W:RET_WIDTH + (g + 1) * GROUP_W] = _rms(
            y[:, gs], ssdnw_ref[:, gs]).astype(BF16)

    tot = prefix[CHUNK - 1:CHUNK, :]
    w = jnp.exp(tot - prefix) * dt
    wexp, cdec = _expand_rows(w, jnp.exp(tot), expand_ref[...])
    xw = (xs * wexp).astype(BF16)
    ds = _state_increment(bm.astype(F32), xw)
    sf_state[...] = sf_state[...] * cdec + ds


def _mix_call(proj3, qr, kr, xbc, dt3, rb, sb, consts, ret_dec_f):
    b, l, _ = proj3.shape
    nc = l // CHUNK

    def col(cb):
        return lambda bi, i: (bi, i, cb)

    def const(shape):
        return pl.BlockSpec(shape, lambda bi, i: (0,) * len(shape))

    in_specs = [
        pl.BlockSpec((None, CHUNK, RET_WIDTH), col(0)),
        pl.BlockSpec((None, CHUNK, RET_WIDTH), col(0)),
        pl.BlockSpec((None, CHUNK, RET_WIDTH), col(COL_V)),
        pl.BlockSpec((None, CHUNK, RET_WIDTH), col(COL_G)),
        pl.BlockSpec((None, CHUNK, SSD_WIDTH), col(COL_Z)),
        pl.BlockSpec((None, CHUNK, SSD_CONV_DIM), col(0)),
        pl.BlockSpec((None, CHUNK, LANES), col(0)),
        pl.BlockSpec((None, None, RET_HEADS, RET_HEAD_DIM, RET_HEAD_DIM),
                     lambda bi, i: (bi, i, 0, 0, 0)),
        pl.BlockSpec((None, None, SSD_STATE, SSD_WIDTH), lambda bi, i: (bi, i, 0, 0)),
        const((RET_HEADS, CHUNK, CHUNK)), const((RET_HEADS, CHUNK, LANES)),
        const((RET_HEADS, CHUNK, LANES)), const((RET_HEADS, LANES)), const((1, RET_WIDTH)),
        const((1, LANES)), const((1, LANES)), const((LANES, SSD_WIDTH)),
        const((1, SSD_WIDTH)), const((1, SSD_WIDTH)),
    ]
    return pl.pallas_call(
        functools.partial(_mix_kernel, ret_dec_f),
        grid=(b, nc),
        in_specs=in_specs,
        out_specs=pl.BlockSpec((None, CHUNK, D_MODEL), col(0)),
        out_shape=jax.ShapeDtypeStruct((b, l, D_MODEL), BF16),
        scratch_shapes=[
            pltpu.VMEM((RET_HEADS, RET_HEAD_DIM, RET_HEAD_DIM), F32),
            pltpu.VMEM((SSD_STATE, SSD_WIDTH), F32),
        ],
        compiler_params=_cparams(2),
        name="forward_sweep",
    )(qr, kr, proj3, proj3, proj3, xbc, dt3, rb, sb,
      consts["ret_mask"], consts["ret_qf"], consts["ret_qb"], consts["ret_kf"],
      consts["ret_norm_w"], consts["dt_bias"], consts["a_row"], consts["expand_f"],
      consts["d_exp"], consts["ssd_norm_w"])


def _outproj_kernel(mix_ref, w_ref, x_ref, o_ref):
    o_ref[...] = x_ref[...] + _dot(mix_ref[...], w_ref[...])


def _out_projection(mix2, w_out, x2, tm=1024, tn=1024):
    t = x2.shape[0]
    return pl.pallas_call(
        _outproj_kernel,
        grid=(t // tm, D_MODEL // tn),
        in_specs=[
            pl.BlockSpec((tm, D_MODEL), lambda i, j: (i, 0)),
            pl.BlockSpec((D_MODEL, tn), lambda i, j: (0, j)),
            pl.BlockSpec((tm, tn), lambda i, j: (i, j)),
        ],
        out_specs=pl.BlockSpec((tm, tn), lambda i, j: (i, j)),
        out_shape=jax.ShapeDtypeStruct((t, D_MODEL), F32),
        compiler_params=_cparams(2),
        name="out_projection",
    )(mix2, w_out, x2)


def _ffn_kernel(tiles_per_seq,
                h_ref, hp_ref, hn_ref, nw_ref, wg_ref, wu_ref, cw_ref, cb_ref, wd_ref,
                o_ref, hnorm, gate_s):
    i = pl.program_id(0)
    j = pl.program_id(1)
    tm = h_ref.shape[0]

    @pl.when(j == 0)
    def _():
        pos_in_seq = i % tiles_per_seq
        has_prev = (pos_in_seq > 0).astype(F32)
        has_next = (pos_in_seq < tiles_per_seq - 1).astype(F32)
        nw = nw_ref[...]
        hnorm[0:HALO, :] = (_rms(hp_ref[...], nw) * has_prev).astype(BF16)
        hnorm[HALO:HALO + tm, :] = _rms(h_ref[...], nw).astype(BF16)
        hnorm[HALO + tm:, :] = (_rms(hn_ref[...], nw) * has_next).astype(BF16)
        o_ref[...] = h_ref[...]

    gate_s[...] = _dot(hnorm[...], wg_ref[...])
    pad = FFN_CONV // 2
    gate = cb_ref[...]
    for t in range(FFN_CONV):
        o = HALO + t - pad
        gate = gate + cw_ref[t:t + 1, :] * gate_s[o:o + tm, :]
    up = _dot(hnorm[HALO:HALO + tm, :], wu_ref[...])
    act = (_gelu_tanh(gate) * up).astype(BF16)
    o_ref[...] += _dot(act, wd_ref[...])


def _ffn_call(h2, norm_w, wg, wu, conv_w, conv_b, wd, seq_len, tm=512, tf=512):
    t = h2.shape[0]
    per16 = tm // HALO
    rows16 = t // HALO
    return pl.pallas_call(
        functools.partial(_ffn_kernel, seq_len // tm),
        grid=(t // tm, D_FF // tf),
        in_specs=[
            pl.BlockSpec((tm, D_MODEL), lambda i, j: (i, 0)),
            pl.BlockSpec((HALO, D_MODEL), lambda i, j: (jnp.maximum(i * per16 - 1, 0), 0)),
            pl.BlockSpec((HALO, D_MODEL), lambda i, j: (jnp.minimum((i + 1) * per16, rows16 - 1), 0)),
            pl.BlockSpec((1, D_MODEL), lambda i, j: (0, 0)),
            pl.BlockSpec((D_MODEL, tf), lambda i, j: (0, j)),
            pl.BlockSpec((D_MODEL, tf), lambda i, j: (0, j)),
            pl.BlockSpec((FFN_CONV, tf), lambda i, j: (0, j)),
            pl.BlockSpec((1, tf), lambda i, j: (0, j)),
            pl.BlockSpec((tf, D_MODEL), lambda i, j: (j, 0)),
        ],
        out_specs=pl.BlockSpec((tm, D_MODEL), lambda i, j: (i, 0)),
        out_shape=jax.ShapeDtypeStruct((t, D_MODEL), F32),
        scratch_shapes=[
            pltpu.VMEM((tm + 2 * HALO, D_MODEL), BF16),
            pltpu.VMEM((tm + 2 * HALO, tf), F32),
        ],
        compiler_params=_cparams(2),
        name="conv_glu_ffn",
    )(h2, h2, h2, norm_w, wg, wu, conv_w, conv_b, wd)


def _ple_kernel(apply_final, h_ref, p_ref, nw_ref, wg_ref, bg_ref, wp_ref, fw_ref, o_ref):
    h = h_ref[...]
    hn = _rms(h, nw_ref[...]).astype(BF16)
    gate = jax.nn.sigmoid(_dot(hn, wg_ref[...]) + bg_ref[...])
    h = h + gate * _dot(p_ref[...].astype(BF16), wp_ref[...])
    if apply_final:
        h = _rms(h, fw_ref[...])
    o_ref[...] = h


def _ple_call(h2, p2, norm_w, wg, bg, wp, final_w, apply_final, tm=512):
    t = h2.shape[0]
    return pl.pallas_call(
        functools.partial(_ple_kernel, apply_final),
        grid=(t // tm,),
        in_specs=[
            pl.BlockSpec((tm, D_MODEL), lambda i: (i, 0)),
            pl.BlockSpec((tm, D_PLE), lambda i: (i, 0)),
            pl.BlockSpec((1, D_MODEL), lambda i: (0, 0)),
            pl.BlockSpec((D_MODEL, D_MODEL), lambda i: (0, 0)),
            pl.BlockSpec((1, D_MODEL), lambda i: (0, 0)),
            pl.BlockSpec((D_PLE, D_MODEL), lambda i: (0, 0)),
            pl.BlockSpec((1, D_MODEL), lambda i: (0, 0)),
        ],
        out_specs=pl.BlockSpec((tm, D_MODEL), lambda i: (i, 0)),
        out_shape=jax.ShapeDtypeStruct((t, D_MODEL), F32),
        compiler_params=_cparams(1),
        name="ple_gate",
    )(h2, p2, norm_w, wg, bg, wp, final_w)


def _retention_tables():
    hh = np.arange(RET_HEADS, dtype=np.float64)
    lf = np.log1p(-np.exp2(-5.0 - hh))
    lb = np.log1p(-np.exp2(-5.5 - hh))
    idx = np.arange(CHUNK, dtype=np.float64)
    dist = idx[:, None] - idx[None, :]
    mask = np.where(dist >= 0, np.exp(lf[:, None, None] * np.abs(dist)),
                    np.exp(lb[:, None, None] * np.abs(dist)))
    ones = np.ones((1, 1, LANES))
    qf = np.exp(lf[:, None] * (idx + 1.0)[None, :])[:, :, None] * ones
    qb = np.exp(lb[:, None] * (CHUNK - idx)[None, :])[:, :, None] * ones
    kf = np.exp(lf[:, None] * (CHUNK - 1.0 - idx)[None, :])
    kb = np.exp(lb[:, None] * idx[None, :])
    dec_f = tuple(float(v) for v in np.exp(lf * CHUNK))
    dec_b = tuple(float(v) for v in np.exp(lb * CHUNK))
    f = lambda a: jnp.asarray(a, F32)
    return dict(ret_mask=f(mask), ret_qf=f(qf), ret_qb=f(qb), ret_kf=f(kf), ret_kb=f(kb)), dec_f, dec_b


def _expand_matrix(first_row):
    e = np.zeros((LANES, SSD_WIDTH), np.float32)
    for h in range(SSD_HEADS):
        e[first_row + h, h * SSD_HEAD_DIM:(h + 1) * SSD_HEAD_DIM] = 1.0
    return jnp.asarray(e, BF16)


def _pad_lanes(v):
    return jnp.pad(v.reshape(1, -1), ((0, 0), (0, LANES - v.size)))


def kernel(x, p, positions, norm_mix_w, w_in, ret_norm_w, ssd_conv_w, ssd_conv_b, ssd_dt_bias,
           ssd_a_log, ssd_d, ssd_norm_w, w_out, norm_ffn_w, ffn_w_gate, ffn_w_up, ffn_conv_w,
           ffn_conv_b, ffn_w_down, ple_norm_w, ple_w_gate, ple_b_gate, ple_w_proj, final_norm_w):
    b, l, _ = x.shape
    depth = w_in.shape[0]
    t = b * l
    nc = l // CHUNK
    row = lambda v: v.reshape(1, -1).astype(F32)

    tables, dec_f, dec_b = _retention_tables()
    half = RET_HEAD_DIM // 2
    inv_freq = ROPE_BASE ** (-jnp.arange(half, dtype=F32) / half)
    rot = dict(
        inv_freq=jnp.concatenate([inv_freq, inv_freq]).reshape(1, LANES),
        rot_sign=jnp.concatenate([-jnp.ones((half,), F32), jnp.ones((half,), F32)]).reshape(1, LANES),
        expand_f=_expand_matrix(0),
        expand_b=_expand_matrix(SSD_HEADS),
    )
    pos4 = positions.reshape(b, nc, 1, CHUNK)

    h = x.reshape(t, D_MODEL)
    for i in range(depth):
        consts = dict(tables)
        consts.update(rot)
        consts.update(
            conv_w_xs=ssd_conv_w[i][:, :SSD_WIDTH], conv_b_xs=row(ssd_conv_b[i][:SSD_WIDTH]),
            conv_w_bc=ssd_conv_w[i][:, SSD_WIDTH:], conv_b_bc=row(ssd_conv_b[i][SSD_WIDTH:]),
            dt_bias=_pad_lanes(ssd_dt_bias[i]),
            a_row=_pad_lanes(-jnp.exp(ssd_a_log[i].astype(F32))),
            ret_norm_w=row(ret_norm_w[i]),
            d_exp=row(jnp.repeat(ssd_d[i], SSD_HEAD_DIM)),
            ssd_norm_w=row(ssd_norm_w[i]),
        )
        w_main = w_in[i][:, :N_MAIN].astype(BF16)
        w_dt = jnp.pad(w_in[i][:, N_MAIN:], ((0, 0), (0, LANES - N_DT))).astype(BF16)

        proj, dt = _in_projection(h, row(norm_mix_w[i]), w_main, w_dt)
        proj3 = proj.reshape(b, l, N_MAIN)
        dt3 = dt.reshape(b, l, LANES)
        qr, kr, xbc, rb, sb = _prep_call(proj3, dt3, pos4, consts, dec_b)
        mix = _mix_call(proj3, qr, kr, xbc, dt3, rb, sb, consts, dec_f)
        h = _out_projection(mix.reshape(t, D_MODEL), w_out[i].astype(BF16), h)
        h = _ffn_call(h, row(norm_ffn_w[i]), ffn_w_gate[i].astype(BF16), ffn_w_up[i].astype(BF16),
                      ffn_conv_w[i], row(ffn_conv_b[i]), ffn_w_down[i].astype(BF16), l)
        h = _ple_call(h, p[i].reshape(t, D_PLE), row(ple_norm_w[i]), ple_w_gate[i].astype(BF16),
                      row(ple_b_gate[i]), ple_w_proj[i].astype(BF16), row(final_norm_w),
                      apply_final=(i == depth - 1))
    return h.reshape(b, l, D_MODEL)
```

```python
import functools
import math

import numpy as np
import jax
import jax.numpy as jnp
from jax import lax
from jax.experimental import pallas as pl
from jax.experimental.pallas import tpu as pltpu

F32 = jnp.float32
BF16 = jnp.bfloat16

D_MODEL = 2048
EPS = 1e-6
D_PLE = 256
RET_WIDTH = D_MODEL // 2
RET_HEAD_DIM = 128
RET_HEADS = RET_WIDTH // RET_HEAD_DIM
ROPE_BASE = 10000.0
SSD_WIDTH = D_MODEL - RET_WIDTH
SSD_HEAD_DIM = 64
SSD_HEADS = SSD_WIDTH // SSD_HEAD_DIM
SSD_GROUPS = 2
SSD_HEADS_PER_GROUP = SSD_HEADS // SSD_GROUPS
SSD_STATE = 128
SSD_CONV = 5
SSD_BC = 2 * SSD_GROUPS * SSD_STATE
SSD_CONV_DIM = SSD_WIDTH + SSD_BC
D_FF = (11 * D_MODEL) // 4
FFN_CONV = 3
N_MAIN = 4 * RET_WIDTH + SSD_WIDTH + SSD_CONV_DIM
N_DT = 2 * SSD_HEADS

CHUNK = 128
LANES = 128
HALO = 16
GROUP_W = SSD_WIDTH // SSD_GROUPS

COL_Q, COL_K, COL_V, COL_G, COL_Z, COL_XS = 0, 1, 2, 3, 4, 5
COL_BC = (5 * RET_WIDTH + SSD_WIDTH) // SSD_BC

VMEM_LIMIT = 56 * 1024 * 1024


def _cparams(n_axes):
    return pltpu.CompilerParams(dimension_semantics=("arbitrary",) * n_axes,
                                vmem_limit_bytes=VMEM_LIMIT)


def _rms(xf, w_row):
    ms = jnp.mean(xf * xf, axis=-1, keepdims=True)
    return xf * lax.rsqrt(ms + EPS) * w_row


def _silu(x):
    return x * jax.nn.sigmoid(x)


def _softplus(x):
    return jnp.maximum(x, 0.0) + jnp.log1p(jnp.exp(-jnp.abs(x)))


def _gelu_tanh(x):
    c = math.sqrt(2.0 / math.pi)
    return 0.5 * x * (1.0 + jnp.tanh(c * (x + 0.044715 * (x * x * x))))


def _dot(a, b):
    return jnp.dot(a, b, preferred_element_type=F32)


def _dot_nt(a, b):
    return lax.dot_general(a, b, (((1,), (1,)), ((), ())), preferred_element_type=F32)


def _split3(a):
    hi = a.astype(BF16)
    r1 = a - hi.astype(F32)
    mid = r1.astype(BF16)
    lo = (r1 - mid.astype(F32)).astype(BF16)
    return hi, mid, lo


def _dot_exact_lhs(m01, a):
    hi, mid, lo = _split3(a)
    return _dot(m01, hi) + _dot(m01, mid) + _dot(m01, lo)


def _dot_exact_rhs(a, m01):
    hi, mid, lo = _split3(a)
    return _dot(hi, m01) + _dot(mid, m01) + _dot(lo, m01)


def _col_bcast(row):
    return jnp.broadcast_to(row, (LANES, LANES)).T


def _tri_masks():
    r = lax.broadcasted_iota(jnp.int32, (CHUNK, CHUNK), 0)
    c = lax.broadcasted_iota(jnp.int32, (CHUNK, CHUNK), 1)
    lower = r >= c
    tri = jnp.where(lower, 1.0, 0.0).astype(BF16)
    tri_t = jnp.where(r <= c, 1.0, 0.0).astype(BF16)
    return lower, tri, tri_t


def _inproj_kernel(x_ref, nw_ref, w_ref, wdt_ref, proj_ref, dt_ref, hn_ref):
    @pl.when(pl.program_id(1) == 0)
    def _():
        hn = _rms(x_ref[...], nw_ref[...]).astype(BF16)
        hn_ref[...] = hn
        dt_ref[...] = _dot(hn, wdt_ref[...])

    proj_ref[...] = _dot(hn_ref[...], w_ref[...]).astype(BF16)


def _in_projection(x2, norm_w, w_main, w_dt, tm=1024, tn=512):
    t = x2.shape[0]
    return pl.pallas_call(
        _inproj_kernel,
        grid=(t // tm, N_MAIN // tn),
        in_specs=[
            pl.BlockSpec((tm, D_MODEL), lambda i, j: (i, 0)),
            pl.BlockSpec((1, D_MODEL), lambda i, j: (0, 0)),
            pl.BlockSpec((D_MODEL, tn), lambda i, j: (0, j)),
            pl.BlockSpec((D_MODEL, LANES), lambda i, j: (0, 0)),
        ],
        out_specs=[
            pl.BlockSpec((tm, tn), lambda i, j: (i, j)),
            pl.BlockSpec((tm, LANES), lambda i, j: (i, 0)),
        ],
        out_shape=[
            jax.ShapeDtypeStruct((t, N_MAIN), BF16),
            jax.ShapeDtypeStruct((t, LANES), F32),
        ],
        scratch_shapes=[pltpu.VMEM((tm, D_MODEL), BF16)],
        compiler_params=_cparams(2),
        name="in_projection",
    )(x2, norm_w, w_main, w_dt)


def _conv5_silu(ext_ref, prev, cur, nxt, w_ref, b_ref, has_prev, has_next):
    ext_ref[0:HALO, :] = prev.astype(F32) * has_prev
    ext_ref[HALO:HALO + CHUNK, :] = cur.astype(F32)
    ext_ref[HALO + CHUNK:, :] = nxt.astype(F32) * has_next
    pad = SSD_CONV // 2
    acc = b_ref[...] + w_ref[pad:pad + 1, :] * ext_ref[HALO:HALO + CHUNK, :]
    for j in range(SSD_CONV):
        if j != pad:
            o = HALO + j - pad
            acc = acc + w_ref[j:j + 1, :] * ext_ref[o:o + CHUNK, :]
    return _silu(acc)


def _expand_rows(w, tot_row, expand01):
    stacked = jnp.concatenate([w, jnp.broadcast_to(tot_row, (8, LANES))], axis=0)
    e = _dot_exact_rhs(stacked, expand01)
    return e[:CHUNK], e[CHUNK:CHUNK + 1]


def _state_increment(bm_f32, xw):
    parts = []
    for g in range(SSD_GROUPS):
        bm_t = bm_f32[:, g * SSD_STATE:(g + 1) * SSD_STATE].T.astype(BF16)
        parts.append(_dot(bm_t, xw[:, g * GROUP_W:(g + 1) * GROUP_W]))
    return jnp.concatenate(parts, axis=1)


def _prep_kernel(ret_dec_b,
                 pos_ref, q_ref, k_ref, v_ref,
                 xs_ref, xsp_ref, xsn_ref, bc_ref, bcp_ref, bcn_ref, dt_ref,
                 cwx_ref, cbx_ref, cwb_ref, cbb_ref, dtbias_ref, arow_ref,
                 invfreq_ref, sign_ref, kb_ref, expand_ref,
                 qr_ref, kr_ref, xbc_ref, rb_ref, sb_ref,
                 rb_state, sb_state, ext_xs, ext_bc):
    i = pl.program_id(1)
    nc = pl.num_programs(1)

    @pl.when(i == 0)
    def _():
        rb_state[...] = jnp.zeros_like(rb_state)
        sb_state[...] = jnp.zeros_like(sb_state)

    has_next = (i > 0).astype(F32)
    has_prev = (i < nc - 1).astype(F32)

    pos_col = _col_bcast(pos_ref[...].astype(F32))
    ang = pos_col * invfreq_ref[...]
    cosv = jnp.cos(ang)
    sinv = jnp.sin(ang) * sign_ref[...]

    def rot(t):
        return t * cosv + pltpu.roll(t, RET_HEAD_DIM // 2, axis=1) * sinv

    kscale = RET_HEAD_DIM ** -0.5
    for h in range(RET_HEADS):
        sl = slice(h * RET_HEAD_DIM, (h + 1) * RET_HEAD_DIM)
        qr_ref[:, sl] = rot(q_ref[:, sl].astype(F32)).astype(BF16)
        kr = rot(k_ref[:, sl].astype(F32)) * kscale
        kr_ref[:, sl] = kr.astype(BF16)
        rb_ref[h] = rb_state[h].astype(BF16)
        kt = (kr.T * kb_ref[h:h + 1, :]).astype(BF16)
        rb_state[h] = rb_state[h] * ret_dec_b[h] + _dot(kt, v_ref[:, sl])

    xs = _conv5_silu(ext_xs, xsp_ref[...], xs_ref[...], xsn_ref[...], cwx_ref, cbx_ref,
                     has_prev, has_next)
    bc = _conv5_silu(ext_bc, bcp_ref[...], bc_ref[...], bcn_ref[...], cwb_ref, cbb_ref,
                     has_prev, has_next)
    xbc_ref[:, :SSD_WIDTH] = xs.astype(BF16)
    xbc_ref[:, SSD_WIDTH:] = bc.astype(BF16)

    _, _, tri_t = _tri_masks()
    dt = _softplus(dt_ref[...] + dtbias_ref[...])
    rcs = _dot_exact_lhs(tri_t, dt * arow_ref[...])
    tot = rcs[0:1, :]
    w = jnp.exp(tot - rcs) * dt
    wexp, cdec = _expand_rows(w, jnp.exp(tot), expand_ref[...])
    xw = (xs * wexp).astype(BF16)
    ds = _state_increment(bc[:, :SSD_GROUPS * SSD_STATE], xw)
    sb_ref[...] = sb_state[...].astype(BF16)
    sb_state[...] = sb_state[...] * cdec + ds


def _prep_call(proj3, dt3, pos4, consts, ret_dec_b):
    b, l, _ = proj3.shape
    nc = l // CHUNK
    rows16 = l // HALO
    per16 = CHUNK // HALO

    def cix(i):
        return nc - 1 - i

    def col(cb):
        return lambda bi, i: (bi, cix(i), cb)

    def prev_halo(cb):
        return lambda bi, i: (bi, jnp.maximum(cix(i) * per16 - 1, 0), cb)

    def next_halo(cb):
        return lambda bi, i: (bi, jnp.minimum((cix(i) + 1) * per16, rows16 - 1), cb)

    def const(shape):
        return pl.BlockSpec(shape, lambda bi, i: (0,) * len(shape))

    in_specs = [
        pl.BlockSpec((None, None, 1, CHUNK), lambda bi, i: (bi, cix(i), 0, 0)),
        pl.BlockSpec((None, CHUNK, RET_WIDTH), col(COL_Q)),
        pl.BlockSpec((None, CHUNK, RET_WIDTH), col(COL_K)),
        pl.BlockSpec((None, CHUNK, RET_WIDTH), col(COL_V)),
        pl.BlockSpec((None, CHUNK, SSD_WIDTH), col(COL_XS)),
        pl.BlockSpec((None, HALO, SSD_WIDTH), prev_halo(COL_XS)),
        pl.BlockSpec((None, HALO, SSD_WIDTH), next_halo(COL_XS)),
        pl.BlockSpec((None, CHUNK, SSD_BC), col(COL_BC)),
        pl.BlockSpec((None, HALO, SSD_BC), prev_halo(COL_BC)),
        pl.BlockSpec((None, HALO, SSD_BC), next_halo(COL_BC)),
        pl.BlockSpec((None, CHUNK, LANES), col(0)),
        const((SSD_CONV, SSD_WIDTH)), const((1, SSD_WIDTH)),
        const((SSD_CONV, SSD_BC)), const((1, SSD_BC)),
        const((1, LANES)), const((1, LANES)),
        const((1, LANES)), const((1, LANES)),
        const((RET_HEADS, LANES)),
        const((LANES, SSD_WIDTH)),
    ]
    out_specs = [
        pl.BlockSpec((None, CHUNK, RET_WIDTH), col(0)),
        pl.BlockSpec((None, CHUNK, RET_WIDTH), col(0)),
        pl.BlockSpec((None, CHUNK, SSD_CONV_DIM), col(0)),
        pl.BlockSpec((None, None, RET_HEADS, RET_HEAD_DIM, RET_HEAD_DIM),
                     lambda bi, i: (bi, cix(i), 0, 0, 0)),
        pl.BlockSpec((None, None, SSD_STATE, SSD_WIDTH), lambda bi, i: (bi, cix(i), 0, 0)),
    ]
    out_shape = [
        jax.ShapeDtypeStruct((b, l, RET_WIDTH), BF16),
        jax.ShapeDtypeStruct((b, l, RET_WIDTH), BF16),
        jax.ShapeDtypeStruct((b, l, SSD_CONV_DIM), BF16),
        jax.ShapeDtypeStruct((b, nc, RET_HEADS, RET_HEAD_DIM, RET_HEAD_DIM), BF16),
        jax.ShapeDtypeStruct((b, nc, SSD_STATE, SSD_WIDTH), BF16),
    ]
    scratch = [
        pltpu.VMEM((RET_HEADS, RET_HEAD_DIM, RET_HEAD_DIM), F32),
        pltpu.VMEM((SSD_STATE, SSD_WIDTH), F32),
        pltpu.VMEM((CHUNK + 2 * HALO, SSD_WIDTH), F32),
        pltpu.VMEM((CHUNK + 2 * HALO, SSD_BC), F32),
    ]
    return pl.pallas_call(
        functools.partial(_prep_kernel, ret_dec_b),
        grid=(b, nc),
        in_specs=in_specs,
        out_specs=out_specs,
        out_shape=out_shape,
        scratch_shapes=scratch,
        compiler_params=_cparams(2),
        name="reverse_sweep",
    )(pos4, proj3, proj3, proj3, proj3, proj3, proj3, proj3, proj3, proj3, dt3,
      consts["conv_w_xs"], consts["conv_b_xs"], consts["conv_w_bc"], consts["conv_b_bc"],
      consts["dt_bias"], consts["a_row"], consts["inv_freq"], consts["rot_sign"],
      consts["ret_kb"], consts["expand_b"])


def _mix_kernel(ret_dec_f,
                qr_ref, kr_ref, v_ref, g_ref, z_ref, xbc_ref, dt_ref, rb_ref, sb_ref,
                mask_ref, qf_ref, qb_ref, kf_ref, retnw_ref,
                dtbias_ref, arow_ref, expand_ref, dexp_ref, ssdnw_ref,
                out_ref,
                rf_state, sf_state):
    i = pl.program_id(1)

    @pl.when(i == 0)
    def _():
        rf_state[...] = jnp.zeros_like(rf_state)
        sf_state[...] = jnp.zeros_like(sf_state)

    for h in range(RET_HEADS):
        sl = slice(h * RET_HEAD_DIM, (h + 1) * RET_HEAD_DIM)
        qh = qr_ref[:, sl]
        kh = kr_ref[:, sl]
        vh = v_ref[:, sl]
        qf32 = qh.astype(F32)
        s = (_dot_nt(qh, kh) * mask_ref[h]).astype(BF16)
        lhs = jnp.concatenate([s, (qf32 * qf_ref[h]).astype(BF16),
                               (qf32 * qb_ref[h]).astype(BF16)], axis=1)
        rhs = jnp.concatenate([vh, rf_state[h].astype(BF16), rb_ref[h]], axis=0)
        o = _dot(lhs, rhs)
        kt = (kh.astype(F32).T * kf_ref[h:h + 1, :]).astype(BF16)
        rf_state[h] = rf_state[h] * ret_dec_f[h] + _dot(kt, vh)
        o = _rms(o, retnw_ref[:, sl])
        out_ref[:, sl] = (_silu(g_ref[:, sl].astype(F32)) * o).astype(BF16)

    lower, tri, tri_t = _tri_masks()
    dt = _softplus(dt_ref[...] + dtbias_ref[...])
    a = dt * arow_ref[...]
    lane = lax.broadcasted_iota(jnp.int32, (CHUNK, LANES), 1)
    prefix = _dot_exact_lhs(tri, a)
    acs = jnp.where(lane < SSD_HEADS, prefix, _dot_exact_lhs(tri_t, a))
    acs_t = acs.T
    dt_t = dt.T
    xs16 = xbc_ref[:, :SSD_WIDTH]
    bm = xbc_ref[:, SSD_WIDTH:SSD_WIDTH + SSD_GROUPS * SSD_STATE]
    cm = xbc_ref[:, SSD_WIDTH + SSD_GROUPS * SSD_STATE:]
    lane_lo = lane < SSD_HEAD_DIM

    ys = []
    for g in range(SSD_GROUPS):
        gs = slice(g * SSD_STATE, (g + 1) * SSD_STATE)
        cm_g = cm[:, gs]
        cm_f32 = cm_g.astype(F32)
        cb = _dot_nt(cm_g, bm[:, gs])
        for pr in range(SSD_HEADS_PER_GROUP // 2):
            ps = slice(g * GROUP_W + pr * LANES, g * GROUP_W + (pr + 1) * LANES)
            rhs = jnp.concatenate([xs16[:, ps], sf_state[:, ps].astype(BF16), sb_ref[:, ps]],
                                  axis=0)
            pair = []
            for sub in range(2):
                e = g * SSD_HEADS_PER_GROUP + 2 * pr + sub
                row_f = acs_t[e:e + 1, :]
                row_b = acs_t[SSD_HEADS + e:SSD_HEADS + e + 1, :]
                col_f = _col_bcast(row_f)
                col_b = _col_bcast(row_b)
                arg = jnp.where(lower, col_f - row_f, col_b - row_b)
                wrow = jnp.where(lower, dt_t[e:e + 1, :], dt_t[SSD_HEADS + e:SSD_HEADS + e + 1, :])
                m = (cb * jnp.exp(arg) * wrow).astype(BF16)
                lhs = jnp.concatenate([m, (cm_f32 * jnp.exp(col_f)).astype(BF16),
                                       (cm_f32 * jnp.exp(col_b)).astype(BF16)], axis=1)
                pair.append(_dot(lhs, rhs))
            ys.append(jnp.where(lane_lo, pair[0], pair[1]))
    y = jnp.concatenate(ys, axis=1)

    xs = xs16.astype(F32)
    y = (y + dexp_ref[...] * xs) * _silu(z_ref[...].astype(F32))
    for g in range(SSD_GROUPS):
        gs = slice(g * GROUP_W, (g + 1) * GROUP_W)
        out_ref[:, RET_WIDTH + g * GROUP_W:RET_WIDTH + (g + 1) * GROUP_W] = _rms(
            y[:, gs], ssdnw_ref[:, gs]).astype(BF16)

    tot = prefix[CHUNK - 1:CHUNK, :]
    w = jnp.exp(tot - prefix) * dt
    wexp, cdec = _expand_rows(w, jnp.exp(tot), expand_ref[...])
    xw = (xs * wexp).astype(BF16)
    ds = _state_increment(bm.astype(F32), xw)
    sf_state[...] = sf_state[...] * cdec + ds


def _mix_call(proj3, qr, kr, xbc, dt3, rb, sb, consts, ret_dec_f):
    b, l, _ = proj3.shape
    nc = l // CHUNK

    def col(cb):
        return lambda bi, i: (bi, i, cb)

    def const(shape):
        return pl.BlockSpec(shape, lambda bi, i: (0,) * len(shape))

    in_specs = [
        pl.BlockSpec((None, CHUNK, RET_WIDTH), col(0)),
        pl.BlockSpec((None, CHUNK, RET_WIDTH), col(0)),
        pl.BlockSpec((None, CHUNK, RET_WIDTH), col(COL_V)),
        pl.BlockSpec((None, CHUNK, RET_WIDTH), col(COL_G)),
        pl.BlockSpec((None, CHUNK, SSD_WIDTH), col(COL_Z)),
        pl.BlockSpec((None, CHUNK, SSD_CONV_DIM), col(0)),
        pl.BlockSpec((None, CHUNK, LANES), col(0)),
        pl.BlockSpec((None, None, RET_HEADS, RET_HEAD_DIM, RET_HEAD_DIM),
                     lambda bi, i: (bi, i, 0, 0, 0)),
        pl.BlockSpec((None, None, SSD_STATE, SSD_WIDTH), lambda bi, i: (bi, i, 0, 0)),
        const((RET_HEADS, CHUNK, CHUNK)), const((RET_HEADS, CHUNK, LANES)),
        const((RET_HEADS, CHUNK, LANES)), const((RET_HEADS, LANES)), const((1, RET_WIDTH)),
        const((1, LANES)), const((1, LANES)), const((LANES, SSD_WIDTH)),
        const((1, SSD_WIDTH)), const((1, SSD_WIDTH)),
    ]
    return pl.pallas_call(
        functools.partial(_mix_kernel, ret_dec_f),
        grid=(b, nc),
        in_specs=in_specs,
        out_specs=pl.BlockSpec((None, CHUNK, D_MODEL), col(0)),
        out_shape=jax.ShapeDtypeStruct((b, l, D_MODEL), BF16),
        scratch_shapes=[
            pltpu.VMEM((RET_HEADS, RET_HEAD_DIM, RET_HEAD_DIM), F32),
            pltpu.VMEM((SSD_STATE, SSD_WIDTH), F32),
        ],
        compiler_params=_cparams(2),
        name="forward_sweep",
    )(qr, kr, proj3, proj3, proj3, xbc, dt3, rb, sb,
      consts["ret_mask"], consts["ret_qf"], consts["ret_qb"], consts["ret_kf"],
      consts["ret_norm_w"], consts["dt_bias"], consts["a_row"], consts["expand_f"],
      consts["d_exp"], consts["ssd_norm_w"])


def _outproj_kernel(mix_ref, w_ref, x_ref, nw_ref, h_ref, hn_ref):
    h = x_ref[...] + _dot(mix_ref[...], w_ref[...])
    h_ref[...] = h
    hn_ref[...] = _rms(h, nw_ref[...]).astype(BF16)


def _out_projection(mix2, w_out, x2, ffn_norm_w, tm=512):
    t = x2.shape[0]
    return pl.pallas_call(
        _outproj_kernel,
        grid=(t // tm,),
        in_specs=[
            pl.BlockSpec((tm, D_MODEL), lambda i: (i, 0)),
            pl.BlockSpec((D_MODEL, D_MODEL), lambda i: (0, 0)),
            pl.BlockSpec((tm, D_MODEL), lambda i: (i, 0)),
            pl.BlockSpec((1, D_MODEL), lambda i: (0, 0)),
        ],
        out_specs=[
            pl.BlockSpec((tm, D_MODEL), lambda i: (i, 0)),
            pl.BlockSpec((tm, D_MODEL), lambda i: (i, 0)),
        ],
        out_shape=[
            jax.ShapeDtypeStruct((t, D_MODEL), F32),
            jax.ShapeDtypeStruct((t, D_MODEL), BF16),
        ],
        compiler_params=_cparams(1),
        name="out_projection",
    )(mix2, w_out, x2, ffn_norm_w)


def _ffn_kernel(tiles_per_seq,
                hn_ref, hp_ref, hx_ref, wg_ref, wu_ref, cw_ref, cb_ref, wd_ref,
                o_ref, hbuf, gate_s):
    i = pl.program_id(0)
    j = pl.program_id(1)
    tm = hn_ref.shape[0]

    @pl.when(j == 0)
    def _():
        pos_in_seq = i % tiles_per_seq
        zero = jnp.zeros((HALO, D_MODEL), BF16)
        hbuf[0:HALO, :] = jnp.where(pos_in_seq > 0, hp_ref[...], zero)
        hbuf[HALO:HALO + tm, :] = hn_ref[...]
        hbuf[HALO + tm:, :] = jnp.where(pos_in_seq < tiles_per_seq - 1, hx_ref[...], zero)
        o_ref[...] = jnp.zeros_like(o_ref)

    gate_s[...] = _dot(hbuf[...], wg_ref[...])
    pad = FFN_CONV // 2
    gate = cb_ref[...]
    for t in range(FFN_CONV):
        o = HALO + t - pad
        gate = gate + cw_ref[t:t + 1, :] * gate_s[o:o + tm, :]
    up = _dot(hn_ref[...], wu_ref[...])
    act = (_gelu_tanh(gate) * up).astype(BF16)
    o_ref[...] += _dot(act, wd_ref[...])


def _ffn_call(hn2, wg, wu, conv_w, conv_b, wd, seq_len, tm=1024, tf=512):
    t = hn2.shape[0]
    per16 = tm // HALO
    rows16 = t // HALO
    return pl.pallas_call(
        functools.partial(_ffn_kernel, seq_len // tm),
        grid=(t // tm, D_FF // tf),
        in_specs=[
            pl.BlockSpec((tm, D_MODEL), lambda i, j: (i, 0)),
            pl.BlockSpec((HALO, D_MODEL), lambda i, j: (jnp.maximum(i * per16 - 1, 0), 0)),
            pl.BlockSpec((HALO, D_MODEL), lambda i, j: (jnp.minimum((i + 1) * per16, rows16 - 1), 0)),
            pl.BlockSpec((D_MODEL, tf), lambda i, j: (0, j)),
            pl.BlockSpec((D_MODEL, tf), lambda i, j: (0, j)),
            pl.BlockSpec((FFN_CONV, tf), lambda i, j: (0, j)),
            pl.BlockSpec((1, tf), lambda i, j: (0, j)),
            pl.BlockSpec((tf, D_MODEL), lambda i, j: (j, 0)),
        ],
        out_specs=pl.BlockSpec((tm, D_MODEL), lambda i, j: (i, 0)),
        out_shape=jax.ShapeDtypeStruct((t, D_MODEL), F32),
        scratch_shapes=[
            pltpu.VMEM((tm + 2 * HALO, D_MODEL), BF16),
            pltpu.VMEM((tm + 2 * HALO, tf), F32),
        ],
        compiler_params=_cparams(2),
        name="conv_glu_ffn",
    )(hn2, hn2, hn2, wg, wu, conv_w, conv_b, wd)


def _ple_kernel(apply_final, h_ref, d_ref, p_ref, nw_ref, wg_ref, bg_ref, wp_ref, fw_ref, o_ref):
    h = h_ref[...] + d_ref[...]
    hn = _rms(h, nw_ref[...]).astype(BF16)
    gate = jax.nn.sigmoid(_dot(hn, wg_ref[...]) + bg_ref[...])
    h = h + gate * _dot(p_ref[...].astype(BF16), wp_ref[...])
    if apply_final:
        h = _rms(h, fw_ref[...])
    o_ref[...] = h


def _ple_call(h2, delta2, p2, norm_w, wg, bg, wp, final_w, apply_final, tm=512):
    t = h2.shape[0]
    return pl.pallas_call(
        functools.partial(_ple_kernel, apply_final),
        grid=(t // tm,),
        in_specs=[
            pl.BlockSpec((tm, D_MODEL), lambda i: (i, 0)),
            pl.BlockSpec((tm, D_MODEL), lambda i: (i, 0)),
            pl.BlockSpec((tm, D_PLE), lambda i: (i, 0)),
            pl.BlockSpec((1, D_MODEL), lambda i: (0, 0)),
            pl.BlockSpec((D_MODEL, D_MODEL), lambda i: (0, 0)),
            pl.BlockSpec((1, D_MODEL), lambda i: (0, 0)),
            pl.BlockSpec((D_PLE, D_MODEL), lambda i: (0, 0)),
            pl.BlockSpec((1, D_MODEL), lambda i: (0, 0)),
        ],
        out_specs=pl.BlockSpec((tm, D_MODEL), lambda i: (i, 0)),
        out_shape=jax.ShapeDtypeStruct((t, D_MODEL), F32),
        compiler_params=_cparams(1),
        name="ple_gate",
    )(h2, delta2, p2, norm_w, wg, bg, wp, final_w)


def _retention_tables():
    hh = np.arange(RET_HEADS, dtype=np.float64)
    lf = np.log1p(-np.exp2(-5.0 - hh))
    lb = np.log1p(-np.exp2(-5.5 - hh))
    idx = np.arange(CHUNK, dtype=np.float64)
    dist = idx[:, None] - idx[None, :]
    mask = np.where(dist >= 0, np.exp(lf[:, None, None] * np.abs(dist)),
                    np.exp(lb[:, None, None] * np.abs(dist)))
    ones = np.ones((1, 1, LANES))
    qf = np.exp(lf[:, None] * (idx + 1.0)[None, :])[:, :, None] * ones
    qb = np.exp(lb[:, None] * (CHUNK - idx)[None, :])[:, :, None] * ones
    kf = np.exp(lf[:, None] * (CHUNK - 1.0 - idx)[None, :])
    kb = np.exp(lb[:, None] * idx[None, :])
    dec_f = tuple(float(v) for v in np.exp(lf * CHUNK))
    dec_b = tuple(float(v) for v in np.exp(lb * CHUNK))
    f = lambda a: jnp.asarray(a, F32)
    return dict(ret_mask=f(mask), ret_qf=f(qf), ret_qb=f(qb), ret_kf=f(kf), ret_kb=f(kb)), dec_f, dec_b


def _expand_matrix(first_row):
    e = np.zeros((LANES, SSD_WIDTH), np.float32)
    for h in range(SSD_HEADS):
        e[first_row + h, h * SSD_HEAD_DIM:(h + 1) * SSD_HEAD_DIM] = 1.0
    return jnp.asarray(e, BF16)


def _pad_lanes(v):
    return jnp.pad(v.reshape(1, -1), ((0, 0), (0, LANES - v.size)))


def kernel(x, p, positions, norm_mix_w, w_in, ret_norm_w, ssd_conv_w, ssd_conv_b, ssd_dt_bias,
           ssd_a_log, ssd_d, ssd_norm_w, w_out, norm_ffn_w, ffn_w_gate, ffn_w_up, ffn_conv_w,
           ffn_conv_b, ffn_w_down, ple_norm_w, ple_w_gate, ple_b_gate, ple_w_proj, final_norm_w):
    b, l, _ = x.shape
    depth = w_in.shape[0]
    t = b * l
    nc = l // CHUNK
    row = lambda v: v.reshape(1, -1).astype(F32)

    tables, dec_f, dec_b = _retention_tables()
    half = RET_HEAD_DIM // 2
    inv_freq = ROPE_BASE ** (-jnp.arange(half, dtype=F32) / half)
    rot = dict(
        inv_freq=jnp.concatenate([inv_freq, inv_freq]).reshape(1, LANES),
        rot_sign=jnp.concatenate([-jnp.ones((half,), F32), jnp.ones((half,), F32)]).reshape(1, LANES),
        expand_f=_expand_matrix(0),
        expand_b=_expand_matrix(SSD_HEADS),
    )
    pos4 = positions.reshape(b, nc, 1, CHUNK)

    h = x.reshape(t, D_MODEL)
    for i in range(depth):
        consts = dict(tables)
        consts.update(rot)
        consts.update(
            conv_w_xs=ssd_conv_w[i][:, :SSD_WIDTH], conv_b_xs=row(ssd_conv_b[i][:SSD_WIDTH]),
            conv_w_bc=ssd_conv_w[i][:, SSD_WIDTH:], conv_b_bc=row(ssd_conv_b[i][SSD_WIDTH:]),
            dt_bias=_pad_lanes(ssd_dt_bias[i]),
            a_row=_pad_lanes(-jnp.exp(ssd_a_log[i].astype(F32))),
            ret_norm_w=row(ret_norm_w[i]),
            d_exp=row(jnp.repeat(ssd_d[i], SSD_HEAD_DIM)),
            ssd_norm_w=row(ssd_norm_w[i]),
        )
        w_main = w_in[i][:, :N_MAIN].astype(BF16)
        w_dt = jnp.pad(w_in[i][:, N_MAIN:], ((0, 0), (0, LANES - N_DT))).astype(BF16)

        proj, dt = _in_projection(h, row(norm_mix_w[i]), w_main, w_dt)
        proj3 = proj.reshape(b, l, N_MAIN)
        dt3 = dt.reshape(b, l, LANES)
        qr, kr, xbc, rb, sb = _prep_call(proj3, dt3, pos4, consts, dec_b)
        mix = _mix_call(proj3, qr, kr, xbc, dt3, rb, sb, consts, dec_f)
        h, hn = _out_projection(mix.reshape(t, D_MODEL), w_out[i].astype(BF16), h,
                                row(norm_ffn_w[i]))
        delta = _ffn_call(hn, ffn_w_gate[i].astype(BF16), ffn_w_up[i].astype(BF16),
                          ffn_conv_w[i], row(ffn_conv_b[i]), ffn_w_down[i].astype(BF16), l)
        h = _ple_call(h, delta, p[i].reshape(t, D_PLE), row(ple_norm_w[i]),
                      ple_w_gate[i].astype(BF16),
                      row(ple_b_gate[i]), ple_w_proj[i].astype(BF16), row(final_norm_w),
                      apply_final=(i == depth - 1))
    return h.reshape(b, l, D_MODEL)
```

```python
import functools
import math

import numpy as np
import jax
import jax.numpy as jnp
from jax import lax
from jax.experimental import pallas as pl
from jax.experimental.pallas import tpu as pltpu

F32 = jnp.float32
BF16 = jnp.bfloat16

D_MODEL = 2048
EPS = 1e-6
D_PLE = 256
RET_WIDTH = D_MODEL // 2
RET_HEAD_DIM = 128
RET_HEADS = RET_WIDTH // RET_HEAD_DIM
ROPE_BASE = 10000.0
SSD_WIDTH = D_MODEL - RET_WIDTH
SSD_HEAD_DIM = 64
SSD_HEADS = SSD_WIDTH // SSD_HEAD_DIM
SSD_GROUPS = 2
SSD_HEADS_PER_GROUP = SSD_HEADS // SSD_GROUPS
SSD_STATE = 128
SSD_CONV = 5
SSD_BC = 2 * SSD_GROUPS * SSD_STATE
SSD_CONV_DIM = SSD_WIDTH + SSD_BC
D_FF = (11 * D_MODEL) // 4
FFN_CONV = 3
N_MAIN = 4 * RET_WIDTH + SSD_WIDTH + SSD_CONV_DIM
N_DT = 2 * SSD_HEADS

CHUNK = 128
LANES = 128
HALO = 16
GROUP_W = SSD_WIDTH // SSD_GROUPS

COL_Q, COL_K, COL_V, COL_G, COL_Z, COL_XS = 0, 1, 2, 3, 4, 5
COL_BC = (5 * RET_WIDTH + SSD_WIDTH) // SSD_BC

VMEM_LIMIT = 56 * 1024 * 1024
LOG2E = math.log2(math.e)


def _cparams(n_axes):
    return pltpu.CompilerParams(dimension_semantics=("arbitrary",) * n_axes,
                                vmem_limit_bytes=VMEM_LIMIT)


def _rms(xf, w_row):
    ms = jnp.mean(xf * xf, axis=-1, keepdims=True)
    return xf * lax.rsqrt(ms + EPS) * w_row


def _silu(x):
    return x * jax.nn.sigmoid(x)


def _softplus(x):
    return jnp.maximum(x, 0.0) + jnp.log1p(jnp.exp(-jnp.abs(x)))


def _gelu_tanh(x):
    c = math.sqrt(2.0 / math.pi)
    return 0.5 * x * (1.0 + jnp.tanh(c * (x + 0.044715 * (x * x * x))))


def _dot(a, b):
    return jnp.dot(a, b, preferred_element_type=F32)


def _dot_nt(a, b):
    return lax.dot_general(a, b, (((1,), (1,)), ((), ())), preferred_element_type=F32)


def _split3(a):
    hi = a.astype(BF16)
    r1 = a - hi.astype(F32)
    mid = r1.astype(BF16)
    lo = (r1 - mid.astype(F32)).astype(BF16)
    return hi, mid, lo


def _dot_exact_lhs(m01, parts):
    hi, mid, lo = parts
    return _dot(m01, hi) + _dot(m01, mid) + _dot(m01, lo)


def _col_bcast(row):
    return jnp.broadcast_to(row, (LANES, LANES)).T


def _tri_masks():
    r = lax.broadcasted_iota(jnp.int32, (CHUNK, CHUNK), 0)
    c = lax.broadcasted_iota(jnp.int32, (CHUNK, CHUNK), 1)
    lower = r >= c
    tri = jnp.where(lower, 1.0, 0.0).astype(BF16)
    tri_t = jnp.where(r <= c, 1.0, 0.0).astype(BF16)
    return lower, tri, tri_t


ROW_SPLIT = 4


def _staggered_row_specs(tm, n_tiles, n_steps):
    tq = tm // ROW_SPLIT

    def spec(q):
        def index(i, j):
            nxt = jnp.minimum(i + (j >= n_steps - ROW_SPLIT + q).astype(jnp.int32), n_tiles - 1)
            return (nxt * ROW_SPLIT + q, 0)
        return pl.BlockSpec((tq, D_MODEL), index)

    return [spec(q) for q in range(ROW_SPLIT)]


def _inproj_kernel(*refs):
    x_parts = refs[:ROW_SPLIT]
    nw_ref, w_ref, wdt_ref, proj_ref, dt_ref, hn_ref = refs[ROW_SPLIT:]
    tq = x_parts[0].shape[0]

    @pl.when(pl.program_id(1) == 0)
    def _():
        for q, x_ref in enumerate(x_parts):
            hn_ref[q * tq:(q + 1) * tq, :] = _rms(x_ref[...], nw_ref[...]).astype(BF16)
        dt_ref[...] = _dot(hn_ref[...], wdt_ref[...])

    proj_ref[...] = _dot(hn_ref[...], w_ref[...]).astype(BF16)


def _in_projection(x2, norm_w, w_all, w_dt, tm=1024, tn=512):
    t = x2.shape[0]
    return pl.pallas_call(
        _inproj_kernel,
        grid=(t // tm, N_MAIN // tn),
        in_specs=_staggered_row_specs(tm, t // tm, N_MAIN // tn) + [
            pl.BlockSpec((1, D_MODEL), lambda i, j: (0, 0)),
            pl.BlockSpec((D_MODEL, tn), lambda i, j: (0, j)),
            pl.BlockSpec((D_MODEL, LANES), lambda i, j: (0, 0)),
        ],
        out_specs=[
            pl.BlockSpec((tm, tn), lambda i, j: (i, j)),
            pl.BlockSpec((tm, LANES), lambda i, j: (i, 0)),
        ],
        out_shape=[
            jax.ShapeDtypeStruct((t, N_MAIN), BF16),
            jax.ShapeDtypeStruct((t, LANES), F32),
        ],
        scratch_shapes=[pltpu.VMEM((tm, D_MODEL), BF16)],
        compiler_params=_cparams(2),
        name="in_projection",
    )(*([x2] * ROW_SPLIT), norm_w, w_all, w_dt)


def _conv5_silu(prev, cur, nxt, w_ref, b_ref, has_prev, has_next):
    zero = jnp.zeros_like(prev)
    depth = 2 * LANES
    ext = jnp.concatenate(
        [jnp.where(has_prev, prev, zero), cur, jnp.where(has_next, nxt, zero),
         jnp.zeros((depth - CHUNK - 2 * HALO, cur.shape[1]), cur.dtype)], axis=0)
    r = lax.broadcasted_iota(jnp.int32, (CHUNK, depth), 0)
    c = lax.broadcasted_iota(jnp.int32, (CHUNK, depth), 1)
    pad = SSD_CONV // 2
    acc = b_ref[...] + w_ref[pad:pad + 1, :] * cur.astype(F32)
    for j in range(SSD_CONV):
        if j != pad:
            shift = jnp.where(c == r + (HALO + j - pad), 1.0, 0.0).astype(BF16)
            acc = acc + w_ref[j:j + 1, :] * _dot(shift, ext)
    return _silu(acc)


def _expand_rows(w, tot_row, expand01):
    stacked = jnp.concatenate([w, jnp.broadcast_to(tot_row, (HALO, LANES))], axis=0)
    e = _dot(stacked.astype(BF16), expand01)
    return e[:CHUNK], e[CHUNK:CHUNK + 1]


def _state_increment(bm_f32, xw):
    parts = []
    for g in range(SSD_GROUPS):
        bm_t = bm_f32[:, g * SSD_STATE:(g + 1) * SSD_STATE].T.astype(BF16)
        parts.append(_dot(bm_t, xw[:, g * GROUP_W:(g + 1) * GROUP_W]))
    return jnp.concatenate(parts, axis=1)


def _prep_kernel(ret_dec_b,
                 pos_ref, q_ref, k_ref, v_ref,
                 xs_ref, xsp_ref, xsn_ref, bc_ref, bcp_ref, bcn_ref, dt_ref,
                 cwx_ref, cbx_ref, cwb_ref, cbb_ref, dtbias_ref, arow2_ref,
                 freq_ref, phase_ref, kb_ref, expand_ref,
                 qr_ref, kr_ref, krt_ref, xbc_ref, rb_ref, sb_ref,
                 rb_state, sb_state):
    i = pl.program_id(1)
    nc = pl.num_programs(1)

    @pl.when(i == 0)
    def _():
        rb_state[...] = jnp.zeros_like(rb_state)
        sb_state[...] = jnp.zeros_like(sb_state)

    has_next = i > 0
    has_prev = i < nc - 1

    half = RET_HEAD_DIM // 2
    ang_t = freq_ref[...] * pos_ref[...].astype(F32) - phase_ref[...]
    cs = jnp.cos(ang_t).T
    sc = pltpu.roll(cs, half, axis=1)
    lane_lo = lax.broadcasted_iota(jnp.int32, (CHUNK, LANES), 1) < half
    cosv = jnp.where(lane_lo, cs, sc)
    sinv = jnp.where(lane_lo, -sc, cs)

    def rot(t, c, s):
        return t * c + pltpu.roll(t, half, axis=1) * s

    kscale = RET_HEAD_DIM ** -0.5
    cosk = cosv * kscale
    sink = sinv * kscale
    for h in range(RET_HEADS):
        sl = slice(h * RET_HEAD_DIM, (h + 1) * RET_HEAD_DIM)
        qr_ref[:, sl] = rot(q_ref[:, sl].astype(F32), cosv, sinv).astype(BF16)
        kr = rot(k_ref[:, sl].astype(F32), cosk, sink)
        kr_ref[:, sl] = kr.astype(BF16)
        kt = kr.T
        krt_ref[h] = kt.astype(BF16)
        rb_ref[h] = rb_state[h].astype(BF16)
        rb_state[h] = rb_state[h] * ret_dec_b[h] + _dot((kt * kb_ref[h:h + 1, :]).astype(BF16),
                                                        v_ref[:, sl])

    xs = _conv5_silu(xsp_ref[...], xs_ref[...], xsn_ref[...], cwx_ref, cbx_ref,
                     has_prev, has_next)
    bc = _conv5_silu(bcp_ref[...], bc_ref[...], bcn_ref[...], cwb_ref, cbb_ref,
                     has_prev, has_next)
    xbc_ref[:, :SSD_WIDTH] = xs.astype(BF16)
    xbc_ref[:, SSD_WIDTH:] = bc.astype(BF16)

    _, _, tri_t = _tri_masks()
    dt = _softplus(dt_ref[...] + dtbias_ref[...])
    rcs = _dot_exact_lhs(tri_t, _split3(dt * arow2_ref[...]))
    tot = rcs[0:1, :]
    w = jnp.exp2(tot - rcs) * dt
    wexp, cdec = _expand_rows(w, jnp.exp2(tot), expand_ref[...])
    xw = (xs * wexp).astype(BF16)
    ds = _state_increment(bc[:, :SSD_GROUPS * SSD_STATE], xw)
    sb_ref[...] = sb_state[...].astype(BF16)
    sb_state[...] = sb_state[...] * cdec + ds


def _prep_call(proj3, dt3, pos4, consts, ret_dec_b):
    b, l, _ = proj3.shape
    nc = l // CHUNK
    rows16 = l // HALO
    per16 = CHUNK // HALO

    def cix(i):
        return nc - 1 - i

    def col(cb):
        return lambda bi, i: (bi, cix(i), cb)

    def prev_halo(cb):
        return lambda bi, i: (bi, jnp.maximum(cix(i) * per16 - 1, 0), cb)

    def next_halo(cb):
        return lambda bi, i: (bi, jnp.minimum((cix(i) + 1) * per16, rows16 - 1), cb)

    def const(shape):
        return pl.BlockSpec(shape, lambda bi, i: (0,) * len(shape))

    in_specs = [
        pl.BlockSpec((None, None, 1, CHUNK), lambda bi, i: (bi, cix(i), 0, 0)),
        pl.BlockSpec((None, CHUNK, RET_WIDTH), col(COL_Q)),
        pl.BlockSpec((None, CHUNK, RET_WIDTH), col(COL_K)),
        pl.BlockSpec((None, CHUNK, RET_WIDTH), col(COL_V)),
        pl.BlockSpec((None, CHUNK, SSD_WIDTH), col(COL_XS)),
        pl.BlockSpec((None, HALO, SSD_WIDTH), prev_halo(COL_XS)),
        pl.BlockSpec((None, HALO, SSD_WIDTH), next_halo(COL_XS)),
        pl.BlockSpec((None, CHUNK, SSD_BC), col(COL_BC)),
        pl.BlockSpec((None, HALO, SSD_BC), prev_halo(COL_BC)),
        pl.BlockSpec((None, HALO, SSD_BC), next_halo(COL_BC)),
        pl.BlockSpec((None, CHUNK, LANES), col(0)),
        const((SSD_CONV, SSD_WIDTH)), const((1, SSD_WIDTH)),
        const((SSD_CONV, SSD_BC)), const((1, SSD_BC)),
        const((1, LANES)), const((1, LANES)),
        const((LANES, LANES)), const((LANES, LANES)),
        const((RET_HEADS, LANES)),
        const((LANES, SSD_WIDTH)),
    ]
    out_specs = [
        pl.BlockSpec((None, CHUNK, RET_WIDTH), col(0)),
        pl.BlockSpec((None, CHUNK, RET_WIDTH), col(0)),
        pl.BlockSpec((None, None, RET_HEADS, RET_HEAD_DIM, CHUNK),
                     lambda bi, i: (bi, cix(i), 0, 0, 0)),
        pl.BlockSpec((None, CHUNK, SSD_CONV_DIM), col(0)),
        pl.BlockSpec((None, None, RET_HEADS, RET_HEAD_DIM, RET_HEAD_DIM),
                     lambda bi, i: (bi, cix(i), 0, 0, 0)),
        pl.BlockSpec((None, None, SSD_STATE, SSD_WIDTH), lambda bi, i: (bi, cix(i), 0, 0)),
    ]
    out_shape = [
        jax.ShapeDtypeStruct((b, l, RET_WIDTH), BF16),
        jax.ShapeDtypeStruct((b, l, RET_WIDTH), BF16),
        jax.ShapeDtypeStruct((b, nc, RET_HEADS, RET_HEAD_DIM, CHUNK), BF16),
        jax.ShapeDtypeStruct((b, l, SSD_CONV_DIM), BF16),
        jax.ShapeDtypeStruct((b, nc, RET_HEADS, RET_HEAD_DIM, RET_HEAD_DIM), BF16),
        jax.ShapeDtypeStruct((b, nc, SSD_STATE, SSD_WIDTH), BF16),
    ]
    scratch = [
        pltpu.VMEM((RET_HEADS, RET_HEAD_DIM, RET_HEAD_DIM), F32),
        pltpu.VMEM((SSD_STATE, SSD_WIDTH), F32),
    ]
    return pl.pallas_call(
        functools.partial(_prep_kernel, ret_dec_b),
        grid=(b, nc),
        in_specs=in_specs,
        out_specs=out_specs,
        out_shape=out_shape,
        scratch_shapes=scratch,
        compiler_params=_cparams(2),
        name="reverse_sweep",
    )(pos4, proj3, proj3, proj3, proj3, proj3, proj3, proj3, proj3, proj3, dt3,
      consts["conv_w_xs"], consts["conv_b_xs"], consts["conv_w_bc"], consts["conv_b_bc"],
      consts["dt_bias"], consts["a_row2"], consts["rot_freq"], consts["rot_phase"],
      consts["ret_kb"], consts["expand_b"])


def _mix_kernel(ret_dec_f,
                qr_ref, kr_ref, krt_ref, v_ref, g_ref, z_ref, xbc_ref, dt_ref, rb_ref, sb_ref,
                mask_ref, qf_ref, qb_ref, kf_ref, retnw_ref,
                dtbias_ref, arow2_ref, expand_ref, expand2_ref, dexp_ref, ssdnw_ref,
                out_ref,
                rf_state, sf_state):
    i = pl.program_id(1)

    @pl.when(i == 0)
    def _():
        rf_state[...] = jnp.zeros_like(rf_state)
        sf_state[...] = jnp.zeros_like(sf_state)

    for h in range(RET_HEADS):
        sl = slice(h * RET_HEAD_DIM, (h + 1) * RET_HEAD_DIM)
        qh = qr_ref[:, sl]
        kh = kr_ref[:, sl]
        vh = v_ref[:, sl]
        s = (_dot_nt(qh, kh) * mask_ref[h]).astype(BF16)
        lhs = jnp.concatenate([s, qh * qf_ref[h], qh * qb_ref[h]], axis=1)
        rhs = jnp.concatenate([vh, rf_state[h].astype(BF16), rb_ref[h]], axis=0)
        o = _dot(lhs, rhs)
        rf_state[h] = rf_state[h] * ret_dec_f[h] + _dot(krt_ref[h], vh * kf_ref[h])
        o = _rms(o, retnw_ref[:, sl])
        out_ref[:, sl] = (_silu(g_ref[:, sl].astype(F32)) * o).astype(BF16)

    lower, tri, tri_t = _tri_masks()
    dt = _softplus(dt_ref[...] + dtbias_ref[...])
    parts = _split3(dt * arow2_ref[...])
    lane = lax.broadcasted_iota(jnp.int32, (CHUNK, LANES), 1)
    prefix = _dot_exact_lhs(tri, parts)
    acs = jnp.where(lane < SSD_HEADS, prefix, _dot_exact_lhs(tri_t, parts))
    acs_t = acs.T
    src_t = (acs - jnp.log2(dt)).T
    edge = _dot(jnp.exp2(acs).astype(BF16), expand2_ref[...])
    xs16 = xbc_ref[:, :SSD_WIDTH]
    bm = xbc_ref[:, SSD_WIDTH:SSD_WIDTH + SSD_GROUPS * SSD_STATE]
    cm = xbc_ref[:, SSD_WIDTH + SSD_GROUPS * SSD_STATE:]
    lane_lo = lane < SSD_HEAD_DIM

    ys = []
    for g in range(SSD_GROUPS):
        gs = slice(g * SSD_STATE, (g + 1) * SSD_STATE)
        gw = slice(g * GROUP_W, (g + 1) * GROUP_W)
        gwb = slice(SSD_WIDTH + g * GROUP_W, SSD_WIDTH + (g + 1) * GROUP_W)
        cm_g = cm[:, gs]
        cb = _dot_nt(cm_g, bm[:, gs])
        y_off = (edge[:, gw] * _dot(cm_g, sf_state[:, gw].astype(BF16))
                 + edge[:, gwb] * _dot(cm_g, sb_ref[:, gw]))
        for pr in range(SSD_HEADS_PER_GROUP // 2):
            ps = slice(g * GROUP_W + pr * LANES, g * GROUP_W + (pr + 1) * LANES)
            pair = []
            for sub in range(2):
                e = g * SSD_HEADS_PER_GROUP + 2 * pr + sub
                eb = SSD_HEADS + e
                dst = jnp.where(lower, _col_bcast(acs_t[e:e + 1, :]), _col_bcast(acs_t[eb:eb + 1, :]))
                src = jnp.where(lower, src_t[e:e + 1, :], src_t[eb:eb + 1, :])
                m = (cb * jnp.exp2(dst - src)).astype(BF16)
                pair.append(_dot(m, xs16[:, ps]))
            ys.append(jnp.where(lane_lo, pair[0], pair[1]) + y_off[:, pr * LANES:(pr + 1) * LANES])
    y = jnp.concatenate(ys, axis=1)

    xs = xs16.astype(F32)
    y = (y + dexp_ref[...] * xs) * _silu(z_ref[...].astype(F32))
    for g in range(SSD_GROUPS):
        gs = slice(g * GROUP_W, (g + 1) * GROUP_W)
        out_ref[:, RET_WIDTH + g * GROUP_W:RET_WIDTH + (g + 1) * GROUP_W] = _rms(
            y[:, gs], ssdnw_ref[:, gs]).astype(BF16)

    tot = prefix[CHUNK - 1:CHUNK, :]
    w = jnp.exp2(tot - prefix) * dt
    wexp, cdec = _expand_rows(w, jnp.exp2(tot), expand_ref[...])
    xw = (xs * wexp).astype(BF16)
    ds = _state_increment(bm.astype(F32), xw)
    sf_state[...] = sf_state[...] * cdec + ds


def _mix_call(proj3, qr, kr, krt, xbc, dt3, rb, sb, consts, ret_dec_f):
    b, l, _ = proj3.shape
    nc = l // CHUNK

    def col(cb):
        return lambda bi, i: (bi, i, cb)

    def const(shape):
        return pl.BlockSpec(shape, lambda bi, i: (0,) * len(shape))

    in_specs = [
        pl.BlockSpec((None, CHUNK, RET_WIDTH), col(0)),
        pl.BlockSpec((None, CHUNK, RET_WIDTH), col(0)),
        pl.BlockSpec((None, None, RET_HEADS, RET_HEAD_DIM, CHUNK), lambda bi, i: (bi, i, 0, 0, 0)),
        pl.BlockSpec((None, CHUNK, RET_WIDTH), col(COL_V)),
        pl.BlockSpec((None, CHUNK, RET_WIDTH), col(COL_G)),
        pl.BlockSpec((None, CHUNK, SSD_WIDTH), col(COL_Z)),
        pl.BlockSpec((None, CHUNK, SSD_CONV_DIM), col(0)),
        pl.BlockSpec((None, CHUNK, LANES), col(0)),
        pl.BlockSpec((None, None, RET_HEADS, RET_HEAD_DIM, RET_HEAD_DIM),
                     lambda bi, i: (bi, i, 0, 0, 0)),
        pl.BlockSpec((None, None, SSD_STATE, SSD_WIDTH), lambda bi, i: (bi, i, 0, 0)),
        const((RET_HEADS, CHUNK, CHUNK)), const((RET_HEADS, CHUNK, LANES)),
        const((RET_HEADS, CHUNK, LANES)), const((RET_HEADS, CHUNK, LANES)), const((1, RET_WIDTH)),
        const((1, LANES)), const((1, LANES)), const((LANES, SSD_WIDTH)),
        const((LANES, 2 * SSD_WIDTH)), const((1, SSD_WIDTH)), const((1, SSD_WIDTH)),
    ]
    return pl.pallas_call(
        functools.partial(_mix_kernel, ret_dec_f),
        grid=(b, nc),
        in_specs=in_specs,
        out_specs=pl.BlockSpec((None, CHUNK, D_MODEL), col(0)),
        out_shape=jax.ShapeDtypeStruct((b, l, D_MODEL), BF16),
        scratch_shapes=[
            pltpu.VMEM((RET_HEADS, RET_HEAD_DIM, RET_HEAD_DIM), F32),
            pltpu.VMEM((SSD_STATE, SSD_WIDTH), F32),
        ],
        compiler_params=_cparams(2),
        name="forward_sweep",
    )(qr, kr, krt, proj3, proj3, proj3, xbc, dt3, rb, sb,
      consts["ret_mask"], consts["ret_qf"], consts["ret_qb"], consts["ret_kf"],
      consts["ret_norm_w"], consts["dt_bias"], consts["a_row2"], consts["expand_f"],
      consts["expand_fb"], consts["d_exp"], consts["ssd_norm_w"])


def _outproj_kernel(mix_ref, w_ref, x_ref, nw_ref, h_ref, hn_ref):
    h = x_ref[...] + _dot(mix_ref[...], w_ref[...])
    h_ref[...] = h
    hn_ref[...] = _rms(h, nw_ref[...]).astype(BF16)


def _out_projection(mix2, w_out, x2, ffn_norm_w, tm=512):
    t = x2.shape[0]
    return pl.pallas_call(
        _outproj_kernel,
        grid=(t // tm,),
        in_specs=[
            pl.BlockSpec((tm, D_MODEL), lambda i: (i, 0)),
            pl.BlockSpec((D_MODEL, D_MODEL), lambda i: (0, 0)),
            pl.BlockSpec((tm, D_MODEL), lambda i: (i, 0)),
            pl.BlockSpec((1, D_MODEL), lambda i: (0, 0)),
        ],
        out_specs=[
            pl.BlockSpec((tm, D_MODEL), lambda i: (i, 0)),
            pl.BlockSpec((tm, D_MODEL), lambda i: (i, 0)),
        ],
        out_shape=[
            jax.ShapeDtypeStruct((t, D_MODEL), F32),
            jax.ShapeDtypeStruct((t, D_MODEL), BF16),
        ],
        compiler_params=_cparams(1),
        name="out_projection",
    )(mix2, w_out, x2, ffn_norm_w)


def _ffn_kernel(tiles_per_seq, *refs):
    hn_parts = refs[:ROW_SPLIT]
    hp_ref, hx_ref, wg_ref, wu_ref, cw_ref, cb_ref, wd_ref, o_ref, hbuf, gate_s = refs[ROW_SPLIT:]
    i = pl.program_id(0)
    j = pl.program_id(1)
    tq = hn_parts[0].shape[0]
    tm = tq * ROW_SPLIT

    @pl.when(j == 0)
    def _():
        pos_in_seq = i % tiles_per_seq
        zero = jnp.zeros((HALO, D_MODEL), BF16)
        hbuf[0:HALO, :] = jnp.where(pos_in_seq > 0, hp_ref[...], zero)
        for q, hn_ref in enumerate(hn_parts):
            hbuf[HALO + q * tq:HALO + (q + 1) * tq, :] = hn_ref[...]
        hbuf[HALO + tm:, :] = jnp.where(pos_in_seq < tiles_per_seq - 1, hx_ref[...], zero)
        o_ref[...] = jnp.zeros_like(o_ref)

    gate_s[...] = _dot(hbuf[...], wg_ref[...])
    pad = FFN_CONV // 2
    gate = cb_ref[...]
    for t in range(FFN_CONV):
        o = HALO + t - pad
        gate = gate + cw_ref[t:t + 1, :] * gate_s[o:o + tm, :]
    up = _dot(hbuf[HALO:HALO + tm, :], wu_ref[...])
    act = (_gelu_tanh(gate) * up).astype(BF16)
    o_ref[...] += _dot(act, wd_ref[...])


def _ffn_call(hn2, wg, wu, conv_w, conv_b, wd, seq_len, tm=1024, tf=512):
    t = hn2.shape[0]
    per16 = tm // HALO
    rows16 = t // HALO
    return pl.pallas_call(
        functools.partial(_ffn_kernel, seq_len // tm),
        grid=(t // tm, D_FF // tf),
        in_specs=_staggered_row_specs(tm, t // tm, D_FF // tf) + [
            pl.BlockSpec((HALO, D_MODEL), lambda i, j: (jnp.maximum(i * per16 - 1, 0), 0)),
            pl.BlockSpec((HALO, D_MODEL), lambda i, j: (jnp.minimum((i + 1) * per16, rows16 - 1), 0)),
            pl.BlockSpec((D_MODEL, tf), lambda i, j: (0, j)),
            pl.BlockSpec((D_MODEL, tf), lambda i, j: (0, j)),
            pl.BlockSpec((FFN_CONV, tf), lambda i, j: (0, j)),
            pl.BlockSpec((1, tf), lambda i, j: (0, j)),
            pl.BlockSpec((tf, D_MODEL), lambda i, j: (j, 0)),
        ],
        out_specs=pl.BlockSpec((tm, D_MODEL), lambda i, j: (i, 0)),
        out_shape=jax.ShapeDtypeStruct((t, D_MODEL), F32),
        scratch_shapes=[
            pltpu.VMEM((tm + 2 * HALO, D_MODEL), BF16),
            pltpu.VMEM((tm + 2 * HALO, tf), F32),
        ],
        compiler_params=_cparams(2),
        name="conv_glu_ffn",
    )(*([hn2] * (ROW_SPLIT + 2)), wg, wu, conv_w, conv_b, wd)


def _ple_kernel(apply_final, h_ref, d_ref, p_ref, nw_ref, wg_ref, bg_ref, wp_ref, fw_ref, o_ref):
    h = h_ref[...] + d_ref[...]
    hn = _rms(h, nw_ref[...]).astype(BF16)
    gate = jax.nn.sigmoid(_dot(hn, wg_ref[...]) + bg_ref[...])
    h = h + gate * _dot(p_ref[...].astype(BF16), wp_ref[...])
    if apply_final:
        h = _rms(h, fw_ref[...])
    o_ref[...] = h


def _ple_call(h2, delta2, p2, norm_w, wg, bg, wp, final_w, apply_final, tm=512):
    t = h2.shape[0]
    return pl.pallas_call(
        functools.partial(_ple_kernel, apply_final),
        grid=(t // tm,),
        in_specs=[
            pl.BlockSpec((tm, D_MODEL), lambda i: (i, 0)),
            pl.BlockSpec((tm, D_MODEL), lambda i: (i, 0)),
            pl.BlockSpec((tm, D_PLE), lambda i: (i, 0)),
            pl.BlockSpec((1, D_MODEL), lambda i: (0, 0)),
            pl.BlockSpec((D_MODEL, D_MODEL), lambda i: (0, 0)),
            pl.BlockSpec((1, D_MODEL), lambda i: (0, 0)),
            pl.BlockSpec((D_PLE, D_MODEL), lambda i: (0, 0)),
            pl.BlockSpec((1, D_MODEL), lambda i: (0, 0)),
        ],
        out_specs=pl.BlockSpec((tm, D_MODEL), lambda i: (i, 0)),
        out_shape=jax.ShapeDtypeStruct((t, D_MODEL), F32),
        compiler_params=_cparams(1),
        name="ple_gate",
    )(h2, delta2, p2, norm_w, wg, bg, wp, final_w)


def _retention_tables():
    hh = np.arange(RET_HEADS, dtype=np.float64)
    lf = np.log1p(-np.exp2(-5.0 - hh))
    lb = np.log1p(-np.exp2(-5.5 - hh))
    idx = np.arange(CHUNK, dtype=np.float64)
    dist = idx[:, None] - idx[None, :]
    mask = np.where(dist >= 0, np.exp(lf[:, None, None] * np.abs(dist)),
                    np.exp(lb[:, None, None] * np.abs(dist)))
    ones = np.ones((1, 1, LANES))
    qf = np.exp(lf[:, None] * (idx + 1.0)[None, :])[:, :, None] * ones
    qb = np.exp(lb[:, None] * (CHUNK - idx)[None, :])[:, :, None] * ones
    kf = np.exp(lf[:, None] * (CHUNK - 1.0 - idx)[None, :])[:, :, None] * ones
    kb = np.exp(lb[:, None] * idx[None, :])
    dec_f = tuple(float(v) for v in np.exp(lf * CHUNK))
    dec_b = tuple(float(v) for v in np.exp(lb * CHUNK))
    f = lambda a: jnp.asarray(a, F32)
    h = lambda a: jnp.asarray(a, BF16)
    return dict(ret_mask=f(mask), ret_qf=h(qf), ret_qb=h(qb), ret_kf=h(kf), ret_kb=f(kb)), dec_f, dec_b


def _expand_matrix(first_row):
    e = np.zeros((LANES, SSD_WIDTH), np.float32)
    for h in range(SSD_HEADS):
        e[first_row + h, h * SSD_HEAD_DIM:(h + 1) * SSD_HEAD_DIM] = 1.0
    return e


def _rotary_tables():
    half = RET_HEAD_DIM // 2
    inv_freq = ROPE_BASE ** (-jnp.arange(half, dtype=F32) / half)
    freq = jnp.broadcast_to(jnp.concatenate([inv_freq, inv_freq])[:, None], (LANES, CHUNK))
    phase = jnp.broadcast_to(
        jnp.concatenate([jnp.zeros((half,), F32), jnp.full((half,), math.pi / 2, F32)])[:, None],
        (LANES, CHUNK))
    return freq, phase


def _pad_lanes(v):
    return jnp.pad(v.reshape(1, -1), ((0, 0), (0, LANES - v.size)))


def kernel(x, p, positions, norm_mix_w, w_in, ret_norm_w, ssd_conv_w, ssd_conv_b, ssd_dt_bias,
           ssd_a_log, ssd_d, ssd_norm_w, w_out, norm_ffn_w, ffn_w_gate, ffn_w_up, ffn_conv_w,
           ffn_conv_b, ffn_w_down, ple_norm_w, ple_w_gate, ple_b_gate, ple_w_proj, final_norm_w):
    b, l, _ = x.shape
    depth = w_in.shape[0]
    t = b * l
    nc = l // CHUNK
    row = lambda v: v.reshape(1, -1).astype(F32)

    tables, dec_f, dec_b = _retention_tables()
    rot_freq, rot_phase = _rotary_tables()
    exp_f, exp_b = _expand_matrix(0), _expand_matrix(SSD_HEADS)
    rot = dict(
        rot_freq=rot_freq, rot_phase=rot_phase,
        expand_f=jnp.asarray(exp_f, BF16),
        expand_b=jnp.asarray(exp_b, BF16),
        expand_fb=jnp.asarray(np.concatenate([exp_f, exp_b], axis=1), BF16),
    )
    pos4 = positions.reshape(b, nc, 1, CHUNK)

    h = x.reshape(t, D_MODEL)
    for i in range(depth):
        consts = dict(tables)
        consts.update(rot)
        consts.update(
            conv_w_xs=ssd_conv_w[i][:, :SSD_WIDTH], conv_b_xs=row(ssd_conv_b[i][:SSD_WIDTH]),
            conv_w_bc=ssd_conv_w[i][:, SSD_WIDTH:], conv_b_bc=row(ssd_conv_b[i][SSD_WIDTH:]),
            dt_bias=_pad_lanes(ssd_dt_bias[i]),
            a_row2=_pad_lanes(-jnp.exp(ssd_a_log[i].astype(F32)) * LOG2E),
            ret_norm_w=row(ret_norm_w[i]),
            d_exp=row(jnp.repeat(ssd_d[i], SSD_HEAD_DIM)),
            ssd_norm_w=row(ssd_norm_w[i]),
        )
        w_all = w_in[i].astype(BF16)
        w_dt = jnp.pad(w_all[:, N_MAIN:], ((0, 0), (0, LANES - N_DT)))

        proj, dt = _in_projection(h, row(norm_mix_w[i]), w_all, w_dt)
        proj3 = proj.reshape(b, l, N_MAIN)
        dt3 = dt.reshape(b, l, LANES)
        qr, kr, krt, xbc, rb, sb = _prep_call(proj3, dt3, pos4, consts, dec_b)
        mix = _mix_call(proj3, qr, kr, krt, xbc, dt3, rb, sb, consts, dec_f)
        h, hn = _out_projection(mix.reshape(t, D_MODEL), w_out[i].astype(BF16), h,
                                row(norm_ffn_w[i]))
        delta = _ffn_call(hn, ffn_w_gate[i].astype(BF16), ffn_w_up[i].astype(BF16),
                          ffn_conv_w[i], row(ffn_conv_b[i]), ffn_w_down[i].astype(BF16), l)
        h = _ple_call(h, delta, p[i].reshape(t, D_PLE), row(ple_norm_w[i]),
                      ple_w_gate[i].astype(BF16),
                      row(ple_b_gate[i]), ple_w_proj[i].astype(BF16), row(final_norm_w),
                      apply_final=(i == depth - 1))
    return h.reshape(b, l, D_MODEL)
```

```python
import functools
import math

import numpy as np
import jax
import jax.numpy as jnp
from jax import lax
from jax.experimental import pallas as pl
from jax.experimental.pallas import tpu as pltpu

F32 = jnp.float32
BF16 = jnp.bfloat16

D_MODEL = 2048
EPS = 1e-6
D_PLE = 256
RET_WIDTH = D_MODEL // 2
RET_HEAD_DIM = 128
RET_HEADS = RET_WIDTH // RET_HEAD_DIM
ROPE_BASE = 10000.0
SSD_WIDTH = D_MODEL - RET_WIDTH
SSD_HEAD_DIM = 64
SSD_HEADS = SSD_WIDTH // SSD_HEAD_DIM
SSD_GROUPS = 2
SSD_HEADS_PER_GROUP = SSD_HEADS // SSD_GROUPS
SSD_STATE = 128
SSD_CONV = 5
SSD_BC = 2 * SSD_GROUPS * SSD_STATE
SSD_CONV_DIM = SSD_WIDTH + SSD_BC
D_FF = (11 * D_MODEL) // 4
FFN_CONV = 3
N_MAIN = 4 * RET_WIDTH + SSD_WIDTH + SSD_CONV_DIM
N_DT = 2 * SSD_HEADS

CHUNK = 128
LANES = 128
HALO = 16
GROUP_W = SSD_WIDTH // SSD_GROUPS

COL_Q, COL_K, COL_V, COL_G, COL_Z, COL_XS = 0, 1, 2, 3, 4, 5
COL_BC = (5 * RET_WIDTH + SSD_WIDTH) // SSD_BC

VMEM_LIMIT = 56 * 1024 * 1024
LOG2E = math.log2(math.e)


def _cparams(n_axes):
    return pltpu.CompilerParams(dimension_semantics=("arbitrary",) * n_axes,
                                vmem_limit_bytes=VMEM_LIMIT)


def _rms(xf, w_row):
    ms = jnp.mean(xf * xf, axis=-1, keepdims=True)
    return xf * lax.rsqrt(ms + EPS) * w_row


def _silu(x):
    return x * jax.nn.sigmoid(x)


def _softplus(x):
    return jnp.maximum(x, 0.0) + jnp.log1p(jnp.exp(-jnp.abs(x)))


def _gelu_tanh(x):
    c = math.sqrt(2.0 / math.pi)
    return 0.5 * x * (1.0 + jnp.tanh(c * (x + 0.044715 * (x * x * x))))


def _dot(a, b):
    return jnp.dot(a, b, preferred_element_type=F32)


def _dot_nt(a, b):
    return lax.dot_general(a, b, (((1,), (1,)), ((), ())), preferred_element_type=F32)


def _split3(a):
    hi = a.astype(BF16)
    r1 = a - hi.astype(F32)
    mid = r1.astype(BF16)
    lo = (r1 - mid.astype(F32)).astype(BF16)
    return hi, mid, lo


def _dot_exact_lhs(m01, parts):
    hi, mid, lo = parts
    return _dot(m01, hi) + _dot(m01, mid) + _dot(m01, lo)


def _col_bcast(row):
    return jnp.broadcast_to(row, (LANES, LANES)).T


def _tri_masks():
    r = lax.broadcasted_iota(jnp.int32, (CHUNK, CHUNK), 0)
    c = lax.broadcasted_iota(jnp.int32, (CHUNK, CHUNK), 1)
    lower = r >= c
    tri = jnp.where(lower, 1.0, 0.0).astype(BF16)
    tri_t = jnp.where(r <= c, 1.0, 0.0).astype(BF16)
    return lower, tri, tri_t


ROW_SPLIT = 4


def _staggered_row_specs(tm, n_tiles, n_steps):
    tq = tm // ROW_SPLIT

    def spec(q):
        def index(i, j):
            nxt = jnp.minimum(i + (j >= n_steps - ROW_SPLIT + q).astype(jnp.int32), n_tiles - 1)
            return (nxt * ROW_SPLIT + q, 0)
        return pl.BlockSpec((tq, D_MODEL), index)

    return [spec(q) for q in range(ROW_SPLIT)]


def _inproj_kernel(*refs):
    x_parts = refs[:ROW_SPLIT]
    nw_ref, w_ref, wdt_ref, proj_ref, dt_ref, hn_ref = refs[ROW_SPLIT:]
    tq = x_parts[0].shape[0]

    @pl.when(pl.program_id(1) == 0)
    def _():
        for q, x_ref in enumerate(x_parts):
            hn_ref[q * tq:(q + 1) * tq, :] = _rms(x_ref[...], nw_ref[...]).astype(BF16)
        dt_ref[...] = _dot(hn_ref[...], wdt_ref[...])

    proj_ref[...] = _dot(hn_ref[...], w_ref[...]).astype(BF16)


def _in_projection(x2, norm_w, w_all, w_dt, tm=1024, tn=512):
    t = x2.shape[0]
    return pl.pallas_call(
        _inproj_kernel,
        grid=(t // tm, N_MAIN // tn),
        in_specs=_staggered_row_specs(tm, t // tm, N_MAIN // tn) + [
            pl.BlockSpec((1, D_MODEL), lambda i, j: (0, 0)),
            pl.BlockSpec((D_MODEL, tn), lambda i, j: (0, j)),
            pl.BlockSpec((D_MODEL, LANES), lambda i, j: (0, 0)),
        ],
        out_specs=[
            pl.BlockSpec((tm, tn), lambda i, j: (i, j)),
            pl.BlockSpec((tm, LANES), lambda i, j: (i, 0)),
        ],
        out_shape=[
            jax.ShapeDtypeStruct((t, N_MAIN), BF16),
            jax.ShapeDtypeStruct((t, LANES), F32),
        ],
        scratch_shapes=[pltpu.VMEM((tm, D_MODEL), BF16)],
        compiler_params=_cparams(2),
        name="in_projection",
    )(*([x2] * ROW_SPLIT), norm_w, w_all, w_dt)


def _conv5_silu(prev, cur, nxt, w_ref, b_ref, has_prev, has_next):
    zero = jnp.zeros_like(prev)
    depth = 2 * LANES
    ext = jnp.concatenate(
        [jnp.where(has_prev, prev, zero), cur, jnp.where(has_next, nxt, zero),
         jnp.zeros((depth - CHUNK - 2 * HALO, cur.shape[1]), cur.dtype)], axis=0)
    r = lax.broadcasted_iota(jnp.int32, (CHUNK, depth), 0)
    c = lax.broadcasted_iota(jnp.int32, (CHUNK, depth), 1)
    pad = SSD_CONV // 2
    acc = b_ref[...] + w_ref[pad:pad + 1, :] * cur.astype(F32)
    for j in range(SSD_CONV):
        if j != pad:
            shift = jnp.where(c == r + (HALO + j - pad), 1.0, 0.0).astype(BF16)
            acc = acc + w_ref[j:j + 1, :] * _dot(shift, ext)
    return _silu(acc)


def _expand_rows(w, tot_row, expand01):
    stacked = jnp.concatenate([w, jnp.broadcast_to(tot_row, (HALO, LANES))], axis=0)
    e = _dot(stacked.astype(BF16), expand01)
    return e[:CHUNK], e[CHUNK:CHUNK + 1]


def _state_increment(bm_f32, xw):
    parts = []
    for g in range(SSD_GROUPS):
        bm_t = bm_f32[:, g * SSD_STATE:(g + 1) * SSD_STATE].T.astype(BF16)
        parts.append(_dot(bm_t, xw[:, g * GROUP_W:(g + 1) * GROUP_W]))
    return jnp.concatenate(parts, axis=1)


def _cast_slab_specs(weights, n_steps, nc):
    in_specs, out_specs, out_shapes, active = [], [], [], []
    for w in weights:
        rows = next(r for r in range(HALO, w.shape[0] + 1, HALO)
                    if w.shape[0] % r == 0 and w.shape[0] // r <= n_steps)
        n_active = w.shape[0] // rows

        def index(bi, i, _last=n_active - 1):
            return (jnp.minimum(bi * nc + i, _last), 0)

        in_specs.append(pl.BlockSpec((rows, w.shape[1]), index))
        out_specs.append(pl.BlockSpec((rows, w.shape[1]), index))
        out_shapes.append(jax.ShapeDtypeStruct(w.shape, BF16))
        active.append(n_active)
    return in_specs, out_specs, out_shapes, tuple(active)


def _sweep_with_casts(body, active, n_in, n_out, *refs):
    n_cast = len(active)
    ins, refs = refs[:n_in], refs[n_in:]
    srcs, refs = refs[:n_cast], refs[n_cast:]
    outs, refs = refs[:n_out], refs[n_out:]
    dsts, scratch = refs[:n_cast], refs[n_cast:]
    step = pl.program_id(0) * pl.num_programs(1) + pl.program_id(1)
    for n_active, src, dst in zip(active, srcs, dsts):
        @pl.when(step < n_active)
        def _():
            dst[...] = src[...].astype(BF16)
    body(*ins, *outs, *scratch)


def _prep_kernel(ret_dec_b,
                 pos_ref, q_ref, k_ref, v_ref,
                 xs_ref, xsp_ref, xsn_ref, bc_ref, bcp_ref, bcn_ref, dt_ref,
                 cwx_ref, cbx_ref, cwb_ref, cbb_ref, dtbias_ref, arow2_ref,
                 freq_ref, phase_ref, kb_ref, expand_ref,
                 qr_ref, kr_ref, krt_ref, xbc_ref, rb_ref, sb_ref,
                 rb_state, sb_state):
    i = pl.program_id(1)
    nc = pl.num_programs(1)

    @pl.when(i == 0)
    def _():
        rb_state[...] = jnp.zeros_like(rb_state)
        sb_state[...] = jnp.zeros_like(sb_state)

    has_next = i > 0
    has_prev = i < nc - 1

    half = RET_HEAD_DIM // 2
    ang_t = freq_ref[...] * pos_ref[...].astype(F32) - phase_ref[...]
    cs = jnp.cos(ang_t).T
    sc = pltpu.roll(cs, half, axis=1)
    lane_lo = lax.broadcasted_iota(jnp.int32, (CHUNK, LANES), 1) < half
    cosv = jnp.where(lane_lo, cs, sc)
    sinv = jnp.where(lane_lo, -sc, cs)

    def rot(t, c, s):
        return t * c + pltpu.roll(t, half, axis=1) * s

    kscale = RET_HEAD_DIM ** -0.5
    cosk = cosv * kscale
    sink = sinv * kscale
    for h in range(RET_HEADS):
        sl = slice(h * RET_HEAD_DIM, (h + 1) * RET_HEAD_DIM)
        qr_ref[:, sl] = rot(q_ref[:, sl].astype(F32), cosv, sinv).astype(BF16)
        kr = rot(k_ref[:, sl].astype(F32), cosk, sink)
        kr_ref[:, sl] = kr.astype(BF16)
        kt = kr.T
        krt_ref[h] = kt.astype(BF16)
        rb_ref[h] = rb_state[h].astype(BF16)
        rb_state[h] = rb_state[h] * ret_dec_b[h] + _dot((kt * kb_ref[h:h + 1, :]).astype(BF16),
                                                        v_ref[:, sl])

    xs = _conv5_silu(xsp_ref[...], xs_ref[...], xsn_ref[...], cwx_ref, cbx_ref,
                     has_prev, has_next)
    bc = _conv5_silu(bcp_ref[...], bc_ref[...], bcn_ref[...], cwb_ref, cbb_ref,
                     has_prev, has_next)
    xbc_ref[:, :SSD_WIDTH] = xs.astype(BF16)
    xbc_ref[:, SSD_WIDTH:] = bc.astype(BF16)

    _, _, tri_t = _tri_masks()
    dt = _softplus(dt_ref[...] + dtbias_ref[...])
    rcs = _dot_exact_lhs(tri_t, _split3(dt * arow2_ref[...]))
    tot = rcs[0:1, :]
    w = jnp.exp2(tot - rcs) * dt
    wexp, cdec = _expand_rows(w, jnp.exp2(tot), expand_ref[...])
    xw = (xs * wexp).astype(BF16)
    ds = _state_increment(bc[:, :SSD_GROUPS * SSD_STATE], xw)
    sb_ref[...] = sb_state[...].astype(BF16)
    sb_state[...] = sb_state[...] * cdec + ds


def _prep_call(proj3, dt3, pos4, consts, ret_dec_b, cast_weights):
    b, l, _ = proj3.shape
    nc = l // CHUNK
    rows16 = l // HALO
    per16 = CHUNK // HALO

    def cix(i):
        return nc - 1 - i

    def col(cb):
        return lambda bi, i: (bi, cix(i), cb)

    def prev_halo(cb):
        return lambda bi, i: (bi, jnp.maximum(cix(i) * per16 - 1, 0), cb)

    def next_halo(cb):
        return lambda bi, i: (bi, jnp.minimum((cix(i) + 1) * per16, rows16 - 1), cb)

    def const(shape):
        return pl.BlockSpec(shape, lambda bi, i: (0,) * len(shape))

    in_specs = [
        pl.BlockSpec((None, None, 1, CHUNK), lambda bi, i: (bi, cix(i), 0, 0)),
        pl.BlockSpec((None, CHUNK, RET_WIDTH), col(COL_Q)),
        pl.BlockSpec((None, CHUNK, RET_WIDTH), col(COL_K)),
        pl.BlockSpec((None, CHUNK, RET_WIDTH), col(COL_V)),
        pl.BlockSpec((None, CHUNK, SSD_WIDTH), col(COL_XS)),
        pl.BlockSpec((None, HALO, SSD_WIDTH), prev_halo(COL_XS)),
        pl.BlockSpec((None, HALO, SSD_WIDTH), next_halo(COL_XS)),
        pl.BlockSpec((None, CHUNK, SSD_BC), col(COL_BC)),
        pl.BlockSpec((None, HALO, SSD_BC), prev_halo(COL_BC)),
        pl.BlockSpec((None, HALO, SSD_BC), next_halo(COL_BC)),
        pl.BlockSpec((None, CHUNK, LANES), col(0)),
        const((SSD_CONV, SSD_WIDTH)), const((1, SSD_WIDTH)),
        const((SSD_CONV, SSD_BC)), const((1, SSD_BC)),
        const((1, LANES)), const((1, LANES)),
        const((LANES, LANES)), const((LANES, LANES)),
        const((RET_HEADS, LANES)),
        const((LANES, SSD_WIDTH)),
    ]
    out_specs = [
        pl.BlockSpec((None, CHUNK, RET_WIDTH), col(0)),
        pl.BlockSpec((None, CHUNK, RET_WIDTH), col(0)),
        pl.BlockSpec((None, None, RET_HEADS, RET_HEAD_DIM, CHUNK),
                     lambda bi, i: (bi, cix(i), 0, 0, 0)),
        pl.BlockSpec((None, CHUNK, SSD_CONV_DIM), col(0)),
        pl.BlockSpec((None, None, RET_HEADS, RET_HEAD_DIM, RET_HEAD_DIM),
                     lambda bi, i: (bi, cix(i), 0, 0, 0)),
        pl.BlockSpec((None, None, SSD_STATE, SSD_WIDTH), lambda bi, i: (bi, cix(i), 0, 0)),
    ]
    out_shape = [
        jax.ShapeDtypeStruct((b, l, RET_WIDTH), BF16),
        jax.ShapeDtypeStruct((b, l, RET_WIDTH), BF16),
        jax.ShapeDtypeStruct((b, nc, RET_HEADS, RET_HEAD_DIM, CHUNK), BF16),
        jax.ShapeDtypeStruct((b, l, SSD_CONV_DIM), BF16),
        jax.ShapeDtypeStruct((b, nc, RET_HEADS, RET_HEAD_DIM, RET_HEAD_DIM), BF16),
        jax.ShapeDtypeStruct((b, nc, SSD_STATE, SSD_WIDTH), BF16),
    ]
    scratch = [
        pltpu.VMEM((RET_HEADS, RET_HEAD_DIM, RET_HEAD_DIM), F32),
        pltpu.VMEM((SSD_STATE, SSD_WIDTH), F32),
    ]
    cast_in, cast_out, cast_shapes, active = _cast_slab_specs(cast_weights, b * nc, nc)
    outs = pl.pallas_call(
        functools.partial(_sweep_with_casts, functools.partial(_prep_kernel, ret_dec_b),
                          active, len(in_specs), len(out_specs)),
        grid=(b, nc),
        in_specs=in_specs + cast_in,
        out_specs=out_specs + cast_out,
        out_shape=out_shape + cast_shapes,
        scratch_shapes=scratch,
        compiler_params=_cparams(2),
        name="reverse_sweep",
    )(pos4, proj3, proj3, proj3, proj3, proj3, proj3, proj3, proj3, proj3, dt3,
      consts["conv_w_xs"], consts["conv_b_xs"], consts["conv_w_bc"], consts["conv_b_bc"],
      consts["dt_bias"], consts["a_row2"], consts["rot_freq"], consts["rot_phase"],
      consts["ret_kb"], consts["expand_b"], *cast_weights)
    return outs[:len(out_specs)], outs[len(out_specs):]


def _mix_kernel(ret_dec_f,
                qr_ref, kr_ref, krt_ref, v_ref, g_ref, z_ref, xbc_ref, dt_ref, rb_ref, sb_ref,
                mask_ref, qf_ref, qb_ref, kf_ref, retnw_ref,
                dtbias_ref, arow2_ref, expand_ref, expand2_ref, dexp_ref, ssdnw_ref,
                out_ref,
                rf_state, sf_state):
    i = pl.program_id(1)

    @pl.when(i == 0)
    def _():
        rf_state[...] = jnp.zeros_like(rf_state)
        sf_state[...] = jnp.zeros_like(sf_state)

    for h in range(RET_HEADS):
        sl = slice(h * RET_HEAD_DIM, (h + 1) * RET_HEAD_DIM)
        qh = qr_ref[:, sl]
        kh = kr_ref[:, sl]
        vh = v_ref[:, sl]
        s = (_dot_nt(qh, kh) * mask_ref[h]).astype(BF16)
        lhs = jnp.concatenate([s, qh * qf_ref[h], qh * qb_ref[h]], axis=1)
        rhs = jnp.concatenate([vh, rf_state[h].astype(BF16), rb_ref[h]], axis=0)
        o = _dot(lhs, rhs)
        rf_state[h] = rf_state[h] * ret_dec_f[h] + _dot(krt_ref[h], vh * kf_ref[h])
        o = _rms(o, retnw_ref[:, sl])
        out_ref[:, sl] = (_silu(g_ref[:, sl].astype(F32)) * o).astype(BF16)

    lower, tri, tri_t = _tri_masks()
    dt = _softplus(dt_ref[...] + dtbias_ref[...])
    parts = _split3(dt * arow2_ref[...])
    lane = lax.broadcasted_iota(jnp.int32, (CHUNK, LANES), 1)
    prefix = _dot_exact_lhs(tri, parts)
    acs = jnp.where(lane < SSD_HEADS, prefix, _dot_exact_lhs(tri_t, parts))
    acs_t = acs.T
    src_t = (acs - jnp.log2(dt)).T
    edge = _dot(jnp.exp2(acs).astype(BF16), expand2_ref[...])
    xs16 = xbc_ref[:, :SSD_WIDTH]
    bm = xbc_ref[:, SSD_WIDTH:SSD_WIDTH + SSD_GROUPS * SSD_STATE]
    cm = xbc_ref[:, SSD_WIDTH + SSD_GROUPS * SSD_STATE:]
    lane_lo = lane < SSD_HEAD_DIM

    ys = []
    for g in range(SSD_GROUPS):
        gs = slice(g * SSD_STATE, (g + 1) * SSD_STATE)
        gw = slice(g * GROUP_W, (g + 1) * GROUP_W)
        gwb = slice(SSD_WIDTH + g * GROUP_W, SSD_WIDTH + (g + 1) * GROUP_W)
        cm_g = cm[:, gs]
        cb = _dot_nt(cm_g, bm[:, gs])
        y_off = (edge[:, gw] * _dot(cm_g, sf_state[:, gw].astype(BF16))
                 + edge[:, gwb] * _dot(cm_g, sb_ref[:, gw]))
        for pr in range(SSD_HEADS_PER_GROUP // 2):
            ps = slice(g * GROUP_W + pr * LANES, g * GROUP_W + (pr + 1) * LANES)
            pair = []
            for sub in range(2):
                e = g * SSD_HEADS_PER_GROUP + 2 * pr + sub
                eb = SSD_HEADS + e
                dst = jnp.where(lower, _col_bcast(acs_t[e:e + 1, :]), _col_bcast(acs_t[eb:eb + 1, :]))
                src = jnp.where(lower, src_t[e:e + 1, :], src_t[eb:eb + 1, :])
                m = (cb * jnp.exp2(dst - src)).astype(BF16)
                pair.append(_dot(m, xs16[:, ps]))
            ys.append(jnp.where(lane_lo, pair[0], pair[1]) + y_off[:, pr * LANES:(pr + 1) * LANES])
    y = jnp.concatenate(ys, axis=1)

    xs = xs16.astype(F32)
    y = (y + dexp_ref[...] * xs) * _silu(z_ref[...].astype(F32))
    for g in range(SSD_GROUPS):
        gs = slice(g * GROUP_W, (g + 1) * GROUP_W)
        out_ref[:, RET_WIDTH + g * GROUP_W:RET_WIDTH + (g + 1) * GROUP_W] = _rms(
            y[:, gs], ssdnw_ref[:, gs]).astype(BF16)

    tot = prefix[CHUNK - 1:CHUNK, :]
    w = jnp.exp2(tot - prefix) * dt
    wexp, cdec = _expand_rows(w, jnp.exp2(tot), expand_ref[...])
    xw = (xs * wexp).astype(BF16)
    ds = _state_increment(bm.astype(F32), xw)
    sf_state[...] = sf_state[...] * cdec + ds


def _mix_call(proj3, qr, kr, krt, xbc, dt3, rb, sb, consts, ret_dec_f, cast_weights):
    b, l, _ = proj3.shape
    nc = l // CHUNK

    def col(cb):
        return lambda bi, i: (bi, i, cb)

    def const(shape):
        return pl.BlockSpec(shape, lambda bi, i: (0,) * len(shape))

    in_specs = [
        pl.BlockSpec((None, CHUNK, RET_WIDTH), col(0)),
        pl.BlockSpec((None, CHUNK, RET_WIDTH), col(0)),
        pl.BlockSpec((None, None, RET_HEADS, RET_HEAD_DIM, CHUNK), lambda bi, i: (bi, i, 0, 0, 0)),
        pl.BlockSpec((None, CHUNK, RET_WIDTH), col(COL_V)),
        pl.BlockSpec((None, CHUNK, RET_WIDTH), col(COL_G)),
        pl.BlockSpec((None, CHUNK, SSD_WIDTH), col(COL_Z)),
        pl.BlockSpec((None, CHUNK, SSD_CONV_DIM), col(0)),
        pl.BlockSpec((None, CHUNK, LANES), col(0)),
        pl.BlockSpec((None, None, RET_HEADS, RET_HEAD_DIM, RET_HEAD_DIM),
                     lambda bi, i: (bi, i, 0, 0, 0)),
        pl.BlockSpec((None, None, SSD_STATE, SSD_WIDTH), lambda bi, i: (bi, i, 0, 0)),
        const((RET_HEADS, CHUNK, CHUNK)), const((RET_HEADS, CHUNK, LANES)),
        const((RET_HEADS, CHUNK, LANES)), const((RET_HEADS, CHUNK, LANES)), const((1, RET_WIDTH)),
        const((1, LANES)), const((1, LANES)), const((LANES, SSD_WIDTH)),
        const((LANES, 2 * SSD_WIDTH)), const((1, SSD_WIDTH)), const((1, SSD_WIDTH)),
    ]
    cast_in, cast_out, cast_shapes, active = _cast_slab_specs(cast_weights, b * nc, nc)
    outs = pl.pallas_call(
        functools.partial(_sweep_with_casts, functools.partial(_mix_kernel, ret_dec_f),
                          active, len(in_specs), 1),
        grid=(b, nc),
        in_specs=in_specs + cast_in,
        out_specs=[pl.BlockSpec((None, CHUNK, D_MODEL), col(0))] + cast_out,
        out_shape=[jax.ShapeDtypeStruct((b, l, D_MODEL), BF16)] + cast_shapes,
        scratch_shapes=[
            pltpu.VMEM((RET_HEADS, RET_HEAD_DIM, RET_HEAD_DIM), F32),
            pltpu.VMEM((SSD_STATE, SSD_WIDTH), F32),
        ],
        compiler_params=_cparams(2),
        name="forward_sweep",
    )(qr, kr, krt, proj3, proj3, proj3, xbc, dt3, rb, sb,
      consts["ret_mask"], consts["ret_qf"], consts["ret_qb"], consts["ret_kf"],
      consts["ret_norm_w"], consts["dt_bias"], consts["a_row2"], consts["expand_f"],
      consts["expand_fb"], consts["d_exp"], consts["ssd_norm_w"], *cast_weights)
    return outs[0], outs[1:]


def _outproj_kernel(mix_ref, w_ref, x_ref, nw_ref, h_ref, hn_ref):
    h = x_ref[...] + _dot(mix_ref[...], w_ref[...])
    h_ref[...] = h
    hn_ref[...] = _rms(h, nw_ref[...]).astype(BF16)


def _out_projection(mix2, w_out, x2, ffn_norm_w, tm=512):
    t = x2.shape[0]
    return pl.pallas_call(
        _outproj_kernel,
        grid=(t // tm,),
        in_specs=[
            pl.BlockSpec((tm, D_MODEL), lambda i: (i, 0)),
            pl.BlockSpec((D_MODEL, D_MODEL), lambda i: (0, 0)),
            pl.BlockSpec((tm, D_MODEL), lambda i: (i, 0)),
            pl.BlockSpec((1, D_MODEL), lambda i: (0, 0)),
        ],
        out_specs=[
            pl.BlockSpec((tm, D_MODEL), lambda i: (i, 0)),
            pl.BlockSpec((tm, D_MODEL), lambda i: (i, 0)),
        ],
        out_shape=[
            jax.ShapeDtypeStruct((t, D_MODEL), F32),
            jax.ShapeDtypeStruct((t, D_MODEL), BF16),
        ],
        compiler_params=_cparams(1),
        name="out_projection",
    )(mix2, w_out, x2, ffn_norm_w)


def _ffn_kernel(tiles_per_seq, hn_ref, hp_ref, hx_ref, wg_ref, wu_ref, cw_ref, cb_ref, wd_ref,
                o_ref, hbuf, gate_s):
    i = pl.program_id(0)
    j = pl.program_id(1)
    tm = hn_ref.shape[0]

    @pl.when(j == 0)
    def _():
        pos_in_seq = i % tiles_per_seq
        zero = jnp.zeros((HALO, D_MODEL), BF16)
        hbuf[0:HALO, :] = jnp.where(pos_in_seq > 0, hp_ref[...], zero)
        hbuf[HALO:HALO + tm, :] = hn_ref[...]
        hbuf[HALO + tm:, :] = jnp.where(pos_in_seq < tiles_per_seq - 1, hx_ref[...], zero)
        o_ref[...] = jnp.zeros_like(o_ref)

    gate_s[...] = _dot(hbuf[...], wg_ref[...])
    pad = FFN_CONV // 2
    gate = cb_ref[...]
    for t in range(FFN_CONV):
        o = HALO + t - pad
        gate = gate + cw_ref[t:t + 1, :] * gate_s[o:o + tm, :]
    up = _dot(hn_ref[...], wu_ref[...])
    act = (_gelu_tanh(gate) * up).astype(BF16)
    o_ref[...] += _dot(act, wd_ref[...])


def _ffn_call(hn2, wg, wu, conv_w, conv_b, wd, seq_len, tm=1024, tf=512):
    t = hn2.shape[0]
    per16 = tm // HALO
    rows16 = t // HALO
    return pl.pallas_call(
        functools.partial(_ffn_kernel, seq_len // tm),
        grid=(t // tm, D_FF // tf),
        in_specs=[
            pl.BlockSpec((tm, D_MODEL), lambda i, j: (i, 0)),
            pl.BlockSpec((HALO, D_MODEL), lambda i, j: (jnp.maximum(i * per16 - 1, 0), 0)),
            pl.BlockSpec((HALO, D_MODEL), lambda i, j: (jnp.minimum((i + 1) * per16, rows16 - 1), 0)),
            pl.BlockSpec((D_MODEL, tf), lambda i, j: (0, j)),
            pl.BlockSpec((D_MODEL, tf), lambda i, j: (0, j)),
            pl.BlockSpec((FFN_CONV, tf), lambda i, j: (0, j)),
            pl.BlockSpec((1, tf), lambda i, j: (0, j)),
            pl.BlockSpec((tf, D_MODEL), lambda i, j: (j, 0)),
        ],
        out_specs=pl.BlockSpec((tm, D_MODEL), lambda i, j: (i, 0)),
        out_shape=jax.ShapeDtypeStruct((t, D_MODEL), F32),
        scratch_shapes=[
            pltpu.VMEM((tm + 2 * HALO, D_MODEL), BF16),
            pltpu.VMEM((tm + 2 * HALO, tf), F32),
        ],
        compiler_params=_cparams(2),
        name="conv_glu_ffn",
    )(hn2, hn2, hn2, wg, wu, conv_w, conv_b, wd)


def _ple_kernel(apply_final, h_ref, d_ref, p_ref, nw_ref, wg_ref, bg_ref, wp_ref, fw_ref, o_ref):
    h = h_ref[...] + d_ref[...]
    hn = _rms(h, nw_ref[...]).astype(BF16)
    gate = jax.nn.sigmoid(_dot(hn, wg_ref[...]) + bg_ref[...])
    h = h + gate * _dot(p_ref[...].astype(BF16), wp_ref[...])
    if apply_final:
        h = _rms(h, fw_ref[...])
    o_ref[...] = h


def _ple_call(h2, delta2, p2, norm_w, wg, bg, wp, final_w, apply_final, tm=512):
    t = h2.shape[0]
    return pl.pallas_call(
        functools.partial(_ple_kernel, apply_final),
        grid=(t // tm,),
        in_specs=[
            pl.BlockSpec((tm, D_MODEL), lambda i: (i, 0)),
            pl.BlockSpec((tm, D_MODEL), lambda i: (i, 0)),
            pl.BlockSpec((tm, D_PLE), lambda i: (i, 0)),
            pl.BlockSpec((1, D_MODEL), lambda i: (0, 0)),
            pl.BlockSpec((D_MODEL, D_MODEL), lambda i: (0, 0)),
            pl.BlockSpec((1, D_MODEL), lambda i: (0, 0)),
            pl.BlockSpec((D_PLE, D_MODEL), lambda i: (0, 0)),
            pl.BlockSpec((1, D_MODEL), lambda i: (0, 0)),
        ],
        out_specs=pl.BlockSpec((tm, D_MODEL), lambda i: (i, 0)),
        out_shape=jax.ShapeDtypeStruct((t, D_MODEL), F32),
        compiler_params=_cparams(1),
        name="ple_gate",
    )(h2, delta2, p2, norm_w, wg, bg, wp, final_w)


def _retention_tables():
    hh = np.arange(RET_HEADS, dtype=np.float64)
    lf = np.log1p(-np.exp2(-5.0 - hh))
    lb = np.log1p(-np.exp2(-5.5 - hh))
    idx = np.arange(CHUNK, dtype=np.float64)
    dist = idx[:, None] - idx[None, :]
    mask = np.where(dist >= 0, np.exp(lf[:, None, None] * np.abs(dist)),
                    np.exp(lb[:, None, None] * np.abs(dist)))
    ones = np.ones((1, 1, LANES))
    qf = np.exp(lf[:, None] * (idx + 1.0)[None, :])[:, :, None] * ones
    qb = np.exp(lb[:, None] * (CHUNK - idx)[None, :])[:, :, None] * ones
    kf = np.exp(lf[:, None] * (CHUNK - 1.0 - idx)[None, :])[:, :, None] * ones
    kb = np.exp(lb[:, None] * idx[None, :])
    dec_f = tuple(float(v) for v in np.exp(lf * CHUNK))
    dec_b = tuple(float(v) for v in np.exp(lb * CHUNK))
    f = lambda a: jnp.asarray(a, F32)
    h = lambda a: jnp.asarray(a, BF16)
    return dict(ret_mask=f(mask), ret_qf=h(qf), ret_qb=h(qb), ret_kf=h(kf), ret_kb=f(kb)), dec_f, dec_b


def _expand_matrix(first_row):
    e = np.zeros((LANES, SSD_WIDTH), np.float32)
    for h in range(SSD_HEADS):
        e[first_row + h, h * SSD_HEAD_DIM:(h + 1) * SSD_HEAD_DIM] = 1.0
    return e


def _rotary_tables():
    half = RET_HEAD_DIM // 2
    inv_freq = ROPE_BASE ** (-jnp.arange(half, dtype=F32) / half)
    freq = jnp.broadcast_to(jnp.concatenate([inv_freq, inv_freq])[:, None], (LANES, CHUNK))
    phase = jnp.broadcast_to(
        jnp.concatenate([jnp.zeros((half,), F32), jnp.full((half,), math.pi / 2, F32)])[:, None],
        (LANES, CHUNK))
    return freq, phase


def _pad_lanes(v):
    return jnp.pad(v.reshape(1, -1), ((0, 0), (0, LANES - v.size)))


def kernel(x, p, positions, norm_mix_w, w_in, ret_norm_w, ssd_conv_w, ssd_conv_b, ssd_dt_bias,
           ssd_a_log, ssd_d, ssd_norm_w, w_out, norm_ffn_w, ffn_w_gate, ffn_w_up, ffn_conv_w,
           ffn_conv_b, ffn_w_down, ple_norm_w, ple_w_gate, ple_b_gate, ple_w_proj, final_norm_w):
    b, l, _ = x.shape
    depth = w_in.shape[0]
    t = b * l
    nc = l // CHUNK
    row = lambda v: v.reshape(1, -1).astype(F32)

    tables, dec_f, dec_b = _retention_tables()
    rot_freq, rot_phase = _rotary_tables()
    exp_f, exp_b = _expand_matrix(0), _expand_matrix(SSD_HEADS)
    rot = dict(
        rot_freq=rot_freq, rot_phase=rot_phase,
        expand_f=jnp.asarray(exp_f, BF16),
        expand_b=jnp.asarray(exp_b, BF16),
        expand_fb=jnp.asarray(np.concatenate([exp_f, exp_b], axis=1), BF16),
    )
    pos4 = positions.reshape(b, nc, 1, CHUNK)

    h = x.reshape(t, D_MODEL)
    for i in range(depth):
        consts = dict(tables)
        consts.update(rot)
        consts.update(
            conv_w_xs=ssd_conv_w[i][:, :SSD_WIDTH], conv_b_xs=row(ssd_conv_b[i][:SSD_WIDTH]),
            conv_w_bc=ssd_conv_w[i][:, SSD_WIDTH:], conv_b_bc=row(ssd_conv_b[i][SSD_WIDTH:]),
            dt_bias=_pad_lanes(ssd_dt_bias[i]),
            a_row2=_pad_lanes(-jnp.exp(ssd_a_log[i].astype(F32)) * LOG2E),
            ret_norm_w=row(ret_norm_w[i]),
            d_exp=row(jnp.repeat(ssd_d[i], SSD_HEAD_DIM)),
            ssd_norm_w=row(ssd_norm_w[i]),
        )
        w_all = w_in[i].astype(BF16)
        w_dt = jnp.pad(w_all[:, N_MAIN:], ((0, 0), (0, LANES - N_DT)))

        proj, dt = _in_projection(h, row(norm_mix_w[i]), w_all, w_dt)
        proj3 = proj.reshape(b, l, N_MAIN)
        dt3 = dt.reshape(b, l, LANES)
        (qr, kr, krt, xbc, rb, sb), (wg16, wu16) = _prep_call(
            proj3, dt3, pos4, consts, dec_b, [ffn_w_gate[i], ffn_w_up[i]])
        mix, (wd16, wo16, wpg16) = _mix_call(
            proj3, qr, kr, krt, xbc, dt3, rb, sb, consts, dec_f,
            [ffn_w_down[i], w_out[i], ple_w_gate[i]])
        h, hn = _out_projection(mix.reshape(t, D_MODEL), wo16, h, row(norm_ffn_w[i]))
        delta = _ffn_call(hn, wg16, wu16, ffn_conv_w[i], row(ffn_conv_b[i]), wd16, l)
        h = _ple_call(h, delta, p[i].reshape(t, D_PLE), row(ple_norm_w[i]), wpg16,
                      row(ple_b_gate[i]), ple_w_proj[i].astype(BF16), row(final_norm_w),
                      apply_final=(i == depth - 1))
    return h.reshape(b, l, D_MODEL)
```

```python
import functools
import math

import numpy as np
import jax
import jax.numpy as jnp
from jax import lax
from jax.experimental import pallas as pl
from jax.experimental.pallas import tpu as pltpu

F32 = jnp.float32
BF16 = jnp.bfloat16

D_MODEL = 2048
EPS = 1e-6
D_PLE = 256
RET_WIDTH = D_MODEL // 2
RET_HEAD_DIM = 128
RET_HEADS = RET_WIDTH // RET_HEAD_DIM
ROPE_BASE = 10000.0
SSD_WIDTH = D_MODEL - RET_WIDTH
SSD_HEAD_DIM = 64
SSD_HEADS = SSD_WIDTH // SSD_HEAD_DIM
SSD_GROUPS = 2
SSD_HEADS_PER_GROUP = SSD_HEADS // SSD_GROUPS
SSD_STATE = 128
SSD_CONV = 5
SSD_BC = 2 * SSD_GROUPS * SSD_STATE
SSD_CONV_DIM = SSD_WIDTH + SSD_BC
D_FF = (11 * D_MODEL) // 4
FFN_CONV = 3
N_MAIN = 4 * RET_WIDTH + SSD_WIDTH + SSD_CONV_DIM
N_DT = 2 * SSD_HEADS

CHUNK = 128
LANES = 128
HALO = 16
GROUP_W = SSD_WIDTH // SSD_GROUPS
FFN_TILE = 512

COL_Q, COL_K, COL_V, COL_G, COL_Z, COL_XS = 0, 1, 2, 3, 4, 5
COL_BC = (5 * RET_WIDTH + SSD_WIDTH) // SSD_BC

VMEM_LIMIT = 56 * 1024 * 1024
LOG2E = math.log2(math.e)


def _cparams(n_axes):
    return pltpu.CompilerParams(dimension_semantics=("arbitrary",) * n_axes,
                                vmem_limit_bytes=VMEM_LIMIT)


def _rms(xf, w_row):
    ms = jnp.mean(xf * xf, axis=-1, keepdims=True)
    return xf * lax.rsqrt(ms + EPS) * w_row


def _silu(x):
    return x * jax.nn.sigmoid(x)


def _softplus(x):
    return jnp.maximum(x, 0.0) + jnp.log1p(jnp.exp(-jnp.abs(x)))


def _gelu_tanh(x):
    c = math.sqrt(2.0 / math.pi)
    return 0.5 * x * (1.0 + jnp.tanh(c * (x + 0.044715 * (x * x * x))))


def _dot(a, b):
    return jnp.dot(a, b, preferred_element_type=F32)


def _dot_nt(a, b):
    return lax.dot_general(a, b, (((1,), (1,)), ((), ())), preferred_element_type=F32)


def _split3(a):
    hi = a.astype(BF16)
    r1 = a - hi.astype(F32)
    mid = r1.astype(BF16)
    lo = (r1 - mid.astype(F32)).astype(BF16)
    return hi, mid, lo


def _dot_exact_lhs(m01, parts):
    hi, mid, lo = parts
    return _dot(m01, hi) + _dot(m01, mid) + _dot(m01, lo)


def _col_bcast(row):
    return jnp.broadcast_to(row, (LANES, LANES)).T


def _tri_masks():
    r = lax.broadcasted_iota(jnp.int32, (CHUNK, CHUNK), 0)
    c = lax.broadcasted_iota(jnp.int32, (CHUNK, CHUNK), 1)
    lower = r >= c
    tri = jnp.where(lower, 1.0, 0.0).astype(BF16)
    tri_t = jnp.where(r <= c, 1.0, 0.0).astype(BF16)
    return lower, tri, tri_t


def _staggered_row_specs(tm, n_tiles, n_steps, n_split):
    assert n_split < n_steps and tm % n_split == 0
    tq = tm // n_split

    def spec(q):
        def index(i, j):
            nxt = jnp.minimum(i + (j >= n_steps - n_split + q).astype(jnp.int32), n_tiles - 1)
            return (nxt * n_split + q, 0)
        return pl.BlockSpec((tq, D_MODEL), index)

    return [spec(q) for q in range(n_split)]


def _inproj_kernel(n_split, *refs):
    x_parts = refs[:n_split]
    nw_ref, w_ref, wdt_ref, proj_ref, dt_ref, hn_ref = refs[n_split:]
    tq = x_parts[0].shape[0]

    @pl.when(pl.program_id(1) == 0)
    def _():
        for q, x_ref in enumerate(x_parts):
            hn_ref[q * tq:(q + 1) * tq, :] = _rms(x_ref[...], nw_ref[...]).astype(BF16)
        dt_ref[...] = _dot(hn_ref[...], wdt_ref[...])

    proj_ref[...] = _dot(hn_ref[...], w_ref[...]).astype(BF16)


def _in_projection(x2, norm_w, w_all, w_dt, tm=1024, tn=N_MAIN // 4, n_split=2):
    t = x2.shape[0]
    return pl.pallas_call(
        functools.partial(_inproj_kernel, n_split),
        grid=(t // tm, N_MAIN // tn),
        in_specs=_staggered_row_specs(tm, t // tm, N_MAIN // tn, n_split) + [
            pl.BlockSpec((1, D_MODEL), lambda i, j: (0, 0)),
            pl.BlockSpec((D_MODEL, tn), lambda i, j: (0, j)),
            pl.BlockSpec((D_MODEL, LANES), lambda i, j: (0, 0)),
        ],
        out_specs=[
            pl.BlockSpec((tm, tn), lambda i, j: (i, j)),
            pl.BlockSpec((tm, LANES), lambda i, j: (i, 0)),
        ],
        out_shape=[
            jax.ShapeDtypeStruct((t, N_MAIN), BF16),
            jax.ShapeDtypeStruct((t, LANES), F32),
        ],
        scratch_shapes=[pltpu.VMEM((tm, D_MODEL), BF16)],
        compiler_params=_cparams(2),
        name="in_projection",
    )(*([x2] * n_split), norm_w, w_all, w_dt)


def _conv5_silu(prev, cur, nxt, w_ref, b_ref, has_prev, has_next):
    zero = jnp.zeros_like(prev)
    depth = 2 * LANES
    ext = jnp.concatenate(
        [jnp.where(has_prev, prev, zero), cur, jnp.where(has_next, nxt, zero),
         jnp.zeros((depth - CHUNK - 2 * HALO, cur.shape[1]), cur.dtype)], axis=0)
    r = lax.broadcasted_iota(jnp.int32, (CHUNK, depth), 0)
    c = lax.broadcasted_iota(jnp.int32, (CHUNK, depth), 1)
    pad = SSD_CONV // 2
    acc = b_ref[...] + w_ref[pad:pad + 1, :] * cur.astype(F32)
    for j in range(SSD_CONV):
        if j != pad:
            shift = jnp.where(c == r + (HALO + j - pad), 1.0, 0.0).astype(BF16)
            acc = acc + w_ref[j:j + 1, :] * _dot(shift, ext)
    return _silu(acc)


def _expand_rows(w, tot_row, expand01):
    stacked = jnp.concatenate([w, jnp.broadcast_to(tot_row, (HALO, LANES))], axis=0)
    e = _dot(stacked.astype(BF16), expand01)
    return e[:CHUNK], e[CHUNK:CHUNK + 1]


def _state_increment(bm_f32, xw):
    parts = []
    for g in range(SSD_GROUPS):
        bm_t = bm_f32[:, g * SSD_STATE:(g + 1) * SSD_STATE].T.astype(BF16)
        parts.append(_dot(bm_t, xw[:, g * GROUP_W:(g + 1) * GROUP_W]))
    return jnp.concatenate(parts, axis=1)


def _cast_slab_specs(weights, n_steps, nc):
    in_specs, out_specs, out_shapes, jobs, flat = [], [], [], [], []
    for entry in weights:
        srcs = entry[:2] if isinstance(entry, tuple) else (entry,)
        tile = entry[2] if isinstance(entry, tuple) else 0
        n_rows, n_cols = srcs[0].shape
        rows = next(r for r in range(HALO, n_rows + 1, HALO)
                    if n_rows % r == 0 and n_rows // r <= n_steps)
        n_active = n_rows // rows

        def index(bi, i, _last=n_active - 1):
            return (jnp.minimum(bi * nc + i, _last), 0)

        for w in srcs:
            in_specs.append(pl.BlockSpec((rows, n_cols), index))
            flat.append(w)
        if tile:
            n_tiles = n_cols // tile
            out_specs.append(pl.BlockSpec((n_tiles, rows, 2 * tile),
                                          lambda bi, i, _index=index: (0,) + _index(bi, i)))
            out_shapes.append(jax.ShapeDtypeStruct((n_tiles, n_rows, 2 * tile), BF16))
        else:
            out_specs.append(pl.BlockSpec((rows, n_cols), index))
            out_shapes.append(jax.ShapeDtypeStruct((n_rows, n_cols), BF16))
        jobs.append((n_active, len(srcs), tile))
    return in_specs, out_specs, out_shapes, tuple(jobs), flat


def _sweep_with_casts(body, jobs, n_in, n_out, *refs):
    n_src = sum(job[1] for job in jobs)
    ins, refs = refs[:n_in], refs[n_in:]
    srcs, refs = refs[:n_src], refs[n_src:]
    outs, refs = refs[:n_out], refs[n_out:]
    dsts, scratch = refs[:len(jobs)], refs[len(jobs):]
    step = pl.program_id(0) * pl.num_programs(1) + pl.program_id(1)
    for (n_active, n_job_src, tile), dst in zip(jobs, dsts):
        job_srcs, srcs = srcs[:n_job_src], srcs[n_job_src:]

        @pl.when(step < n_active)
        def _():
            if tile:
                for t in range(dst.shape[0]):
                    for k, src in enumerate(job_srcs):
                        dst[t, :, k * tile:(k + 1) * tile] = (
                            src[:, t * tile:(t + 1) * tile].astype(BF16))
            else:
                dst[...] = job_srcs[0][...].astype(BF16)
    body(*ins, *outs, *scratch)


def _prep_kernel(ret_dec_b,
                 pos_ref, q_ref, k_ref, v_ref,
                 xs_ref, xsp_ref, xsn_ref, bc_ref, bcp_ref, bcn_ref, dt_ref,
                 cwx_ref, cbx_ref, cwb_ref, cbb_ref, dtbias_ref, arow2_ref,
                 freq_ref, phase_ref, kb_ref, expand_ref,
                 qr_ref, kr_ref, krt_ref, xbc_ref, rb_ref, sb_ref,
                 rb_state, sb_state):
    i = pl.program_id(1)
    nc = pl.num_programs(1)

    @pl.when(i == 0)
    def _():
        rb_state[...] = jnp.zeros_like(rb_state)
        sb_state[...] = jnp.zeros_like(sb_state)

    has_next = i > 0
    has_prev = i < nc - 1

    half = RET_HEAD_DIM // 2
    ang_t = freq_ref[...] * pos_ref[...].astype(F32) - phase_ref[...]
    cs = jnp.cos(ang_t).T
    sc = pltpu.roll(cs, half, axis=1)
    lane_lo = lax.broadcasted_iota(jnp.int32, (CHUNK, LANES), 1) < half
    cosv = jnp.where(lane_lo, cs, sc)
    sinv = jnp.where(lane_lo, -sc, cs)

    def rot(t, c, s):
        return t * c + pltpu.roll(t, half, axis=1) * s

    kscale = RET_HEAD_DIM ** -0.5
    cosk = cosv * kscale
    sink = sinv * kscale
    for h in range(RET_HEADS):
        sl = slice(h * RET_HEAD_DIM, (h + 1) * RET_HEAD_DIM)
        qr_ref[:, sl] = rot(q_ref[:, sl].astype(F32), cosv, sinv).astype(BF16)
        kr = rot(k_ref[:, sl].astype(F32), cosk, sink)
        kr_ref[:, sl] = kr.astype(BF16)
        kt = kr.T
        krt_ref[h] = kt.astype(BF16)
        rb_ref[h] = rb_state[h].astype(BF16)
        rb_state[h] = rb_state[h] * ret_dec_b[h] + _dot((kt * kb_ref[h:h + 1, :]).astype(BF16),
                                                        v_ref[:, sl])

    xs = _conv5_silu(xsp_ref[...], xs_ref[...], xsn_ref[...], cwx_ref, cbx_ref,
                     has_prev, has_next)
    bc = _conv5_silu(bcp_ref[...], bc_ref[...], bcn_ref[...], cwb_ref, cbb_ref,
                     has_prev, has_next)
    xbc_ref[:, :SSD_WIDTH] = xs.astype(BF16)
    xbc_ref[:, SSD_WIDTH:] = bc.astype(BF16)

    _, _, tri_t = _tri_masks()
    dt = _softplus(dt_ref[...] + dtbias_ref[...])
    rcs = _dot_exact_lhs(tri_t, _split3(dt * arow2_ref[...]))
    tot = rcs[0:1, :]
    w = jnp.exp2(tot - rcs) * dt
    wexp, cdec = _expand_rows(w, jnp.exp2(tot), expand_ref[...])
    xw = (xs * wexp).astype(BF16)
    ds = _state_increment(bc[:, :SSD_GROUPS * SSD_STATE], xw)
    sb_ref[...] = sb_state[...].astype(BF16)
    sb_state[...] = sb_state[...] * cdec + ds


def _prep_call(proj3, dt3, pos4, consts, ret_dec_b, cast_weights):
    b, l, _ = proj3.shape
    nc = l // CHUNK
    rows16 = l // HALO
    per16 = CHUNK // HALO

    def cix(i):
        return nc - 1 - i

    def col(cb):
        return lambda bi, i: (bi, cix(i), cb)

    def prev_halo(cb):
        return lambda bi, i: (bi, jnp.maximum(cix(i) * per16 - 1, 0), cb)

    def next_halo(cb):
        return lambda bi, i: (bi, jnp.minimum((cix(i) + 1) * per16, rows16 - 1), cb)

    def const(shape):
        return pl.BlockSpec(shape, lambda bi, i: (0,) * len(shape))

    in_specs = [
        pl.BlockSpec((None, None, 1, CHUNK), lambda bi, i: (bi, cix(i), 0, 0)),
        pl.BlockSpec((None, CHUNK, RET_WIDTH), col(COL_Q)),
        pl.BlockSpec((None, CHUNK, RET_WIDTH), col(COL_K)),
        pl.BlockSpec((None, CHUNK, RET_WIDTH), col(COL_V)),
        pl.BlockSpec((None, CHUNK, SSD_WIDTH), col(COL_XS)),
        pl.BlockSpec((None, HALO, SSD_WIDTH), prev_halo(COL_XS)),
        pl.BlockSpec((None, HALO, SSD_WIDTH), next_halo(COL_XS)),
        pl.BlockSpec((None, CHUNK, SSD_BC), col(COL_BC)),
        pl.BlockSpec((None, HALO, SSD_BC), prev_halo(COL_BC)),
        pl.BlockSpec((None, HALO, SSD_BC), next_halo(COL_BC)),
        pl.BlockSpec((None, CHUNK, LANES), col(0)),
        const((SSD_CONV, SSD_WIDTH)), const((1, SSD_WIDTH)),
        const((SSD_CONV, SSD_BC)), const((1, SSD_BC)),
        const((1, LANES)), const((1, LANES)),
        const((LANES, LANES)), const((LANES, LANES)),
        const((RET_HEADS, LANES)),
        const((LANES, SSD_WIDTH)),
    ]
    out_specs = [
        pl.BlockSpec((None, CHUNK, RET_WIDTH), col(0)),
        pl.BlockSpec((None, CHUNK, RET_WIDTH), col(0)),
        pl.BlockSpec((None, None, RET_HEADS, RET_HEAD_DIM, CHUNK),
                     lambda bi, i: (bi, cix(i), 0, 0, 0)),
        pl.BlockSpec((None, CHUNK, SSD_CONV_DIM), col(0)),
        pl.BlockSpec((None, None, RET_HEADS, RET_HEAD_DIM, RET_HEAD_DIM),
                     lambda bi, i: (bi, cix(i), 0, 0, 0)),
        pl.BlockSpec((None, None, SSD_STATE, SSD_WIDTH), lambda bi, i: (bi, cix(i), 0, 0)),
    ]
    out_shape = [
        jax.ShapeDtypeStruct((b, l, RET_WIDTH), BF16),
        jax.ShapeDtypeStruct((b, l, RET_WIDTH), BF16),
        jax.ShapeDtypeStruct((b, nc, RET_HEADS, RET_HEAD_DIM, CHUNK), BF16),
        jax.ShapeDtypeStruct((b, l, SSD_CONV_DIM), BF16),
        jax.ShapeDtypeStruct((b, nc, RET_HEADS, RET_HEAD_DIM, RET_HEAD_DIM), BF16),
        jax.ShapeDtypeStruct((b, nc, SSD_STATE, SSD_WIDTH), BF16),
    ]
    scratch = [
        pltpu.VMEM((RET_HEADS, RET_HEAD_DIM, RET_HEAD_DIM), F32),
        pltpu.VMEM((SSD_STATE, SSD_WIDTH), F32),
    ]
    cast_in, cast_out, cast_shapes, active, cast_srcs = _cast_slab_specs(cast_weights, b * nc, nc)
    outs = pl.pallas_call(
        functools.partial(_sweep_with_casts, functools.partial(_prep_kernel, ret_dec_b),
                          active, len(in_specs), len(out_specs)),
        grid=(b, nc),
        in_specs=in_specs + cast_in,
        out_specs=out_specs + cast_out,
        out_shape=out_shape + cast_shapes,
        scratch_shapes=scratch,
        compiler_params=_cparams(2),
        name="reverse_sweep",
    )(pos4, proj3, proj3, proj3, proj3, proj3, proj3, proj3, proj3, proj3, dt3,
      consts["conv_w_xs"], consts["conv_b_xs"], consts["conv_w_bc"], consts["conv_b_bc"],
      consts["dt_bias"], consts["a_row2"], consts["rot_freq"], consts["rot_phase"],
      consts["ret_kb"], consts["expand_b"], *cast_srcs)
    return outs[:len(out_specs)], outs[len(out_specs):]


def _mix_kernel(ret_dec_f,
                qr_ref, kr_ref, krt_ref, v_ref, g_ref, z_ref, xbc_ref, dt_ref, rb_ref, sb_ref,
                mask_ref, qf_ref, qb_ref, kf_ref, retnw_ref,
                dtbias_ref, arow2_ref, expand_ref, expand2_ref, dexp_ref, ssdnw_ref,
                out_ref,
                rf_state, sf_state):
    i = pl.program_id(1)

    @pl.when(i == 0)
    def _():
        rf_state[...] = jnp.zeros_like(rf_state)
        sf_state[...] = jnp.zeros_like(sf_state)

    for h in range(RET_HEADS):
        sl = slice(h * RET_HEAD_DIM, (h + 1) * RET_HEAD_DIM)
        qh = qr_ref[:, sl]
        kh = kr_ref[:, sl]
        vh = v_ref[:, sl]
        s = (_dot_nt(qh, kh) * mask_ref[h]).astype(BF16)
        lhs = jnp.concatenate([s, qh * qf_ref[h], qh * qb_ref[h]], axis=1)
        rhs = jnp.concatenate([vh, rf_state[h].astype(BF16), rb_ref[h]], axis=0)
        o = _dot(lhs, rhs)
        rf_state[h] = rf_state[h] * ret_dec_f[h] + _dot(krt_ref[h], vh * kf_ref[h])
        o = _rms(o, retnw_ref[:, sl])
        out_ref[:, sl] = (_silu(g_ref[:, sl].astype(F32)) * o).astype(BF16)

    lower, tri, tri_t = _tri_masks()
    dt = _softplus(dt_ref[...] + dtbias_ref[...])
    parts = _split3(dt * arow2_ref[...])
    lane = lax.broadcasted_iota(jnp.int32, (CHUNK, LANES), 1)
    prefix = _dot_exact_lhs(tri, parts)
    acs = jnp.where(lane < SSD_HEADS, prefix, _dot_exact_lhs(tri_t, parts))
    acs_t = acs.T
    src_t = (acs - jnp.log2(dt)).T
    edge = _dot(jnp.exp2(acs).astype(BF16), expand2_ref[...])
    xs16 = xbc_ref[:, :SSD_WIDTH]
    bm = xbc_ref[:, SSD_WIDTH:SSD_WIDTH + SSD_GROUPS * SSD_STATE]
    cm = xbc_ref[:, SSD_WIDTH + SSD_GROUPS * SSD_STATE:]
    lane_lo = lane < SSD_HEAD_DIM

    ys = []
    for g in range(SSD_GROUPS):
        gs = slice(g * SSD_STATE, (g + 1) * SSD_STATE)
        gw = slice(g * GROUP_W, (g + 1) * GROUP_W)
        gwb = slice(SSD_WIDTH + g * GROUP_W, SSD_WIDTH + (g + 1) * GROUP_W)
        cm_g = cm[:, gs]
        cb = _dot_nt(cm_g, bm[:, gs])
        y_off = (edge[:, gw] * _dot(cm_g, sf_state[:, gw].astype(BF16))
                 + edge[:, gwb] * _dot(cm_g, sb_ref[:, gw]))
        for pr in range(SSD_HEADS_PER_GROUP // 2):
            ps = slice(g * GROUP_W + pr * LANES, g * GROUP_W + (pr + 1) * LANES)
            pair = []
            for sub in range(2):
                e = g * SSD_HEADS_PER_GROUP + 2 * pr + sub
                eb = SSD_HEADS + e
                dst = jnp.where(lower, _col_bcast(acs_t[e:e + 1, :]), _col_bcast(acs_t[eb:eb + 1, :]))
                src = jnp.where(lower, src_t[e:e + 1, :], src_t[eb:eb + 1, :])
                m = (cb * jnp.exp2(dst - src)).astype(BF16)
                pair.append(_dot(m, xs16[:, ps]))
            ys.append(jnp.where(lane_lo, pair[0], pair[1]) + y_off[:, pr * LANES:(pr + 1) * LANES])
    y = jnp.concatenate(ys, axis=1)

    xs = xs16.astype(F32)
    y = (y + dexp_ref[...] * xs) * _silu(z_ref[...].astype(F32))
    for g in range(SSD_GROUPS):
        gs = slice(g * GROUP_W, (g + 1) * GROUP_W)
        out_ref[:, RET_WIDTH + g * GROUP_W:RET_WIDTH + (g + 1) * GROUP_W] = _rms(
            y[:, gs], ssdnw_ref[:, gs]).astype(BF16)

    tot = prefix[CHUNK - 1:CHUNK, :]
    w = jnp.exp2(tot - prefix) * dt
    wexp, cdec = _expand_rows(w, jnp.exp2(tot), expand_ref[...])
    xw = (xs * wexp).astype(BF16)
    ds = _state_increment(bm.astype(F32), xw)
    sf_state[...] = sf_state[...] * cdec + ds


def _mix_call(proj3, qr, kr, krt, xbc, dt3, rb, sb, consts, ret_dec_f, cast_weights):
    b, l, _ = proj3.shape
    nc = l // CHUNK

    def col(cb):
        return lambda bi, i: (bi, i, cb)

    def const(shape):
        return pl.BlockSpec(shape, lambda bi, i: (0,) * len(shape))

    in_specs = [
        pl.BlockSpec((None, CHUNK, RET_WIDTH), col(0)),
        pl.BlockSpec((None, CHUNK, RET_WIDTH), col(0)),
        pl.BlockSpec((None, None, RET_HEADS, RET_HEAD_DIM, CHUNK), lambda bi, i: (bi, i, 0, 0, 0)),
        pl.BlockSpec((None, CHUNK, RET_WIDTH), col(COL_V)),
        pl.BlockSpec((None, CHUNK, RET_WIDTH), col(COL_G)),
        pl.BlockSpec((None, CHUNK, SSD_WIDTH), col(COL_Z)),
        pl.BlockSpec((None, CHUNK, SSD_CONV_DIM), col(0)),
        pl.BlockSpec((None, CHUNK, LANES), col(0)),
        pl.BlockSpec((None, None, RET_HEADS, RET_HEAD_DIM, RET_HEAD_DIM),
                     lambda bi, i: (bi, i, 0, 0, 0)),
        pl.BlockSpec((None, None, SSD_STATE, SSD_WIDTH), lambda bi, i: (bi, i, 0, 0)),
        const((RET_HEADS, CHUNK, CHUNK)), const((RET_HEADS, CHUNK, LANES)),
        const((RET_HEADS, CHUNK, LANES)), const((RET_HEADS, CHUNK, LANES)), const((1, RET_WIDTH)),
        const((1, LANES)), const((1, LANES)), const((LANES, SSD_WIDTH)),
        const((LANES, 2 * SSD_WIDTH)), const((1, SSD_WIDTH)), const((1, SSD_WIDTH)),
    ]
    cast_in, cast_out, cast_shapes, active, cast_srcs = _cast_slab_specs(cast_weights, b * nc, nc)
    outs = pl.pallas_call(
        functools.partial(_sweep_with_casts, functools.partial(_mix_kernel, ret_dec_f),
                          active, len(in_specs), 1),
        grid=(b, nc),
        in_specs=in_specs + cast_in,
        out_specs=[pl.BlockSpec((None, CHUNK, D_MODEL), col(0))] + cast_out,
        out_shape=[jax.ShapeDtypeStruct((b, l, D_MODEL), BF16)] + cast_shapes,
        scratch_shapes=[
            pltpu.VMEM((RET_HEADS, RET_HEAD_DIM, RET_HEAD_DIM), F32),
            pltpu.VMEM((SSD_STATE, SSD_WIDTH), F32),
        ],
        compiler_params=_cparams(2),
        name="forward_sweep",
    )(qr, kr, krt, proj3, proj3, proj3, xbc, dt3, rb, sb,
      consts["ret_mask"], consts["ret_qf"], consts["ret_qb"], consts["ret_kf"],
      consts["ret_norm_w"], consts["dt_bias"], consts["a_row2"], consts["expand_f"],
      consts["expand_fb"], consts["d_exp"], consts["ssd_norm_w"], *cast_srcs)
    return outs[0], outs[1:]


def _outproj_kernel(mix_ref, w_ref, x_ref, nw_ref, h_ref, hn_ref):
    h = x_ref[...] + _dot(mix_ref[...], w_ref[...])
    h_ref[...] = h
    hn_ref[...] = _rms(h, nw_ref[...]).astype(BF16)


def _out_projection(mix2, w_out, x2, ffn_norm_w, tm=512):
    t = x2.shape[0]
    return pl.pallas_call(
        _outproj_kernel,
        grid=(t // tm,),
        in_specs=[
            pl.BlockSpec((tm, D_MODEL), lambda i: (i, 0)),
            pl.BlockSpec((D_MODEL, D_MODEL), lambda i: (0, 0)),
            pl.BlockSpec((tm, D_MODEL), lambda i: (i, 0)),
            pl.BlockSpec((1, D_MODEL), lambda i: (0, 0)),
        ],
        out_specs=[
            pl.BlockSpec((tm, D_MODEL), lambda i: (i, 0)),
            pl.BlockSpec((tm, D_MODEL), lambda i: (i, 0)),
        ],
        out_shape=[
            jax.ShapeDtypeStruct((t, D_MODEL), F32),
            jax.ShapeDtypeStruct((t, D_MODEL), BF16),
        ],
        compiler_params=_cparams(1),
        name="out_projection",
    )(mix2, w_out, x2, ffn_norm_w)


def _ffn_kernel(tiles_per_seq, hn_ref, hp_ref, hx_ref, wgu_ref, cw_ref, cb_ref, wd_ref,
                o_ref, hbuf, gu_s):
    i = pl.program_id(0)
    j = pl.program_id(1)
    tm = hn_ref.shape[0]
    tf = cw_ref.shape[1]

    @pl.when(j == 0)
    def _():
        pos_in_seq = i % tiles_per_seq
        zero = jnp.zeros((HALO, D_MODEL), BF16)
        hbuf[0:HALO, :] = jnp.where(pos_in_seq > 0, hp_ref[...], zero)
        hbuf[HALO:HALO + tm, :] = hn_ref[...]
        hbuf[HALO + tm:, :] = jnp.where(pos_in_seq < tiles_per_seq - 1, hx_ref[...], zero)
        o_ref[...] = jnp.zeros_like(o_ref)

    gu_s[...] = _dot(hbuf[...], wgu_ref[...])
    pad = FFN_CONV // 2
    gate = cb_ref[...]
    for t in range(FFN_CONV):
        o = HALO + t - pad
        gate = gate + cw_ref[t:t + 1, :] * gu_s[o:o + tm, :tf]
    act = (_gelu_tanh(gate) * gu_s[HALO:HALO + tm, tf:]).astype(BF16)
    o_ref[...] += _dot(act, wd_ref[...])


def _ffn_call(hn2, wgu, conv_w, conv_b, wd, seq_len, tm=1024):
    t = hn2.shape[0]
    tf = wgu.shape[2] // 2
    per16 = tm // HALO
    rows16 = t // HALO
    return pl.pallas_call(
        functools.partial(_ffn_kernel, seq_len // tm),
        grid=(t // tm, D_FF // tf),
        in_specs=[
            pl.BlockSpec((tm, D_MODEL), lambda i, j: (i, 0)),
            pl.BlockSpec((HALO, D_MODEL), lambda i, j: (jnp.maximum(i * per16 - 1, 0), 0)),
            pl.BlockSpec((HALO, D_MODEL), lambda i, j: (jnp.minimum((i + 1) * per16, rows16 - 1), 0)),
            pl.BlockSpec((None, D_MODEL, 2 * tf), lambda i, j: (j, 0, 0)),
            pl.BlockSpec((FFN_CONV, tf), lambda i, j: (0, j)),
            pl.BlockSpec((1, tf), lambda i, j: (0, j)),
            pl.BlockSpec((tf, D_MODEL), lambda i, j: (j, 0)),
        ],
        out_specs=pl.BlockSpec((tm, D_MODEL), lambda i, j: (i, 0)),
        out_shape=jax.ShapeDtypeStruct((t, D_MODEL), F32),
        scratch_shapes=[
            pltpu.VMEM((tm + 2 * HALO, D_MODEL), BF16),
            pltpu.VMEM((tm + 2 * HALO, 2 * tf), F32),
        ],
        compiler_params=_cparams(2),
        name="conv_glu_ffn",
    )(hn2, hn2, hn2, wgu, conv_w, conv_b, wd)


def _ple_kernel(apply_final, h_ref, d_ref, p_ref, nw_ref, wg_ref, bg_ref, wp_ref, fw_ref, o_ref):
    h = h_ref[...] + d_ref[...]
    hn = _rms(h, nw_ref[...]).astype(BF16)
    gate = jax.nn.sigmoid(_dot(hn, wg_ref[...]) + bg_ref[...])
    h = h + gate * _dot(p_ref[...].astype(BF16), wp_ref[...])
    if apply_final:
        h = _rms(h, fw_ref[...])
    o_ref[...] = h


def _ple_call(h2, delta2, p2, norm_w, wg, bg, wp, final_w, apply_final, tm=512):
    t = h2.shape[0]
    return pl.pallas_call(
        functools.partial(_ple_kernel, apply_final),
        grid=(t // tm,),
        in_specs=[
            pl.BlockSpec((tm, D_MODEL), lambda i: (i, 0)),
            pl.BlockSpec((tm, D_MODEL), lambda i: (i, 0)),
            pl.BlockSpec((tm, D_PLE), lambda i: (i, 0)),
            pl.BlockSpec((1, D_MODEL), lambda i: (0, 0)),
            pl.BlockSpec((D_MODEL, D_MODEL), lambda i: (0, 0)),
            pl.BlockSpec((1, D_MODEL), lambda i: (0, 0)),
            pl.BlockSpec((D_PLE, D_MODEL), lambda i: (0, 0)),
            pl.BlockSpec((1, D_MODEL), lambda i: (0, 0)),
        ],
        out_specs=pl.BlockSpec((tm, D_MODEL), lambda i: (i, 0)),
        out_shape=jax.ShapeDtypeStruct((t, D_MODEL), F32),
        compiler_params=_cparams(1),
        name="ple_gate",
    )(h2, delta2, p2, norm_w, wg, bg, wp, final_w)


def _retention_tables():
    hh = np.arange(RET_HEADS, dtype=np.float64)
    lf = np.log1p(-np.exp2(-5.0 - hh))
    lb = np.log1p(-np.exp2(-5.5 - hh))
    idx = np.arange(CHUNK, dtype=np.float64)
    dist = idx[:, None] - idx[None, :]
    mask = np.where(dist >= 0, np.exp(lf[:, None, None] * np.abs(dist)),
                    np.exp(lb[:, None, None] * np.abs(dist)))
    ones = np.ones((1, 1, LANES))
    qf = np.exp(lf[:, None] * (idx + 1.0)[None, :])[:, :, None] * ones
    qb = np.exp(lb[:, None] * (CHUNK - idx)[None, :])[:, :, None] * ones
    kf = np.exp(lf[:, None] * (CHUNK - 1.0 - idx)[None, :])[:, :, None] * ones
    kb = np.exp(lb[:, None] * idx[None, :])
    dec_f = tuple(float(v) for v in np.exp(lf * CHUNK))
    dec_b = tuple(float(v) for v in np.exp(lb * CHUNK))
    f = lambda a: jnp.asarray(a, F32)
    h = lambda a: jnp.asarray(a, BF16)
    return dict(ret_mask=f(mask), ret_qf=h(qf), ret_qb=h(qb), ret_kf=h(kf), ret_kb=f(kb)), dec_f, dec_b


def _expand_matrix(first_row):
    e = np.zeros((LANES, SSD_WIDTH), np.float32)
    for h in range(SSD_HEADS):
        e[first_row + h, h * SSD_HEAD_DIM:(h + 1) * SSD_HEAD_DIM] = 1.0
    return e


def _rotary_tables():
    half = RET_HEAD_DIM // 2
    inv_freq = ROPE_BASE ** (-jnp.arange(half, dtype=F32) / half)
    freq = jnp.broadcast_to(jnp.concatenate([inv_freq, inv_freq])[:, None], (LANES, CHUNK))
    phase = jnp.broadcast_to(
        jnp.concatenate([jnp.zeros((half,), F32), jnp.full((half,), math.pi / 2, F32)])[:, None],
        (LANES, CHUNK))
    return freq, phase


def _pad_lanes(v):
    return jnp.pad(v.reshape(1, -1), ((0, 0), (0, LANES - v.size)))


def kernel(x, p, positions, norm_mix_w, w_in, ret_norm_w, ssd_conv_w, ssd_conv_b, ssd_dt_bias,
           ssd_a_log, ssd_d, ssd_norm_w, w_out, norm_ffn_w, ffn_w_gate, ffn_w_up, ffn_conv_w,
           ffn_conv_b, ffn_w_down, ple_norm_w, ple_w_gate, ple_b_gate, ple_w_proj, final_norm_w):
    b, l, _ = x.shape
    depth = w_in.shape[0]
    t = b * l
    nc = l // CHUNK
    row = lambda v: v.reshape(1, -1).astype(F32)

    tables, dec_f, dec_b = _retention_tables()
    rot_freq, rot_phase = _rotary_tables()
    exp_f, exp_b = _expand_matrix(0), _expand_matrix(SSD_HEADS)
    rot = dict(
        rot_freq=rot_freq, rot_phase=rot_phase,
        expand_f=jnp.asarray(exp_f, BF16),
        expand_b=jnp.asarray(exp_b, BF16),
        expand_fb=jnp.asarray(np.concatenate([exp_f, exp_b], axis=1), BF16),
    )
    pos4 = positions.reshape(b, nc, 1, CHUNK)

    h = x.reshape(t, D_MODEL)
    for i in range(depth):
        consts = dict(tables)
        consts.update(rot)
        consts.update(
            conv_w_xs=ssd_conv_w[i][:, :SSD_WIDTH], conv_b_xs=row(ssd_conv_b[i][:SSD_WIDTH]),
            conv_w_bc=ssd_conv_w[i][:, SSD_WIDTH:], conv_b_bc=row(ssd_conv_b[i][SSD_WIDTH:]),
            dt_bias=_pad_lanes(ssd_dt_bias[i]),
            a_row2=_pad_lanes(-jnp.exp(ssd_a_log[i].astype(F32)) * LOG2E),
            ret_norm_w=row(ret_norm_w[i]),
            d_exp=row(jnp.repeat(ssd_d[i], SSD_HEAD_DIM)),
            ssd_norm_w=row(ssd_norm_w[i]),
        )
        w_all = w_in[i].astype(BF16)
        w_dt = jnp.pad(w_all[:, N_MAIN:], ((0, 0), (0, LANES - N_DT)))

        proj, dt = _in_projection(h, row(norm_mix_w[i]), w_all, w_dt)
        proj3 = proj.reshape(b, l, N_MAIN)
        dt3 = dt.reshape(b, l, LANES)
        (qr, kr, krt, xbc, rb, sb), (wgu16,) = _prep_call(
            proj3, dt3, pos4, consts, dec_b, [(ffn_w_gate[i], ffn_w_up[i], FFN_TILE)])
        mix, (wd16, wo16, wpg16) = _mix_call(
            proj3, qr, kr, krt, xbc, dt3, rb, sb, consts, dec_f,
            [ffn_w_down[i], w_out[i], ple_w_gate[i]])
        h, hn = _out_projection(mix.reshape(t, D_MODEL), wo16, h, row(norm_ffn_w[i]))
        delta = _ffn_call(hn, wgu16, ffn_conv_w[i], row(ffn_conv_b[i]), wd16, l)
        h = _ple_call(h, delta, p[i].reshape(t, D_PLE), row(ple_norm_w[i]), wpg16,
                      row(ple_b_gate[i]), ple_w_proj[i].astype(BF16), row(final_norm_w),
                      apply_final=(i == depth - 1))
    return h.reshape(b, l, D_MODEL)
```

```python
import functools
import math

import numpy as np
import jax
import jax.numpy as jnp
from jax import lax
from jax.experimental import pallas as pl
from jax.experimental.pallas import tpu as pltpu

F32 = jnp.float32
BF16 = jnp.bfloat16

D_MODEL = 2048
EPS = 1e-6
D_PLE = 256
RET_WIDTH = D_MODEL // 2
RET_HEAD_DIM = 128
RET_HEADS = RET_WIDTH // RET_HEAD_DIM
ROPE_BASE = 10000.0
SSD_WIDTH = D_MODEL - RET_WIDTH
SSD_HEAD_DIM = 64
SSD_HEADS = SSD_WIDTH // SSD_HEAD_DIM
SSD_GROUPS = 2
SSD_HEADS_PER_GROUP = SSD_HEADS // SSD_GROUPS
SSD_STATE = 128
SSD_CONV = 5
SSD_BC = 2 * SSD_GROUPS * SSD_STATE
SSD_CONV_DIM = SSD_WIDTH + SSD_BC
D_FF = (11 * D_MODEL) // 4
FFN_CONV = 3
N_MAIN = 4 * RET_WIDTH + SSD_WIDTH + SSD_CONV_DIM
N_DT = 2 * SSD_HEADS

CHUNK = 128
LANES = 128
HALO = 16
GROUP_W = SSD_WIDTH // SSD_GROUPS
FFN_TILE = 512

COL_Q, COL_K, COL_V, COL_G, COL_Z, COL_XS = 0, 1, 2, 3, 4, 5
COL_BC = (5 * RET_WIDTH + SSD_WIDTH) // SSD_BC

VMEM_LIMIT = 56 * 1024 * 1024
LOG2E = math.log2(math.e)


def _cparams(n_axes):
    return pltpu.CompilerParams(dimension_semantics=("arbitrary",) * n_axes,
                                vmem_limit_bytes=VMEM_LIMIT)


def _rms(xf, w_row):
    ms = jnp.mean(xf * xf, axis=-1, keepdims=True)
    return xf * lax.rsqrt(ms + EPS) * w_row


def _silu(x):
    return x * jax.nn.sigmoid(x)


def _softplus(x):
    return jnp.maximum(x, 0.0) + jnp.log1p(jnp.exp(-jnp.abs(x)))


def _gelu_tanh(x):
    c = math.sqrt(2.0 / math.pi)
    return 0.5 * x * (1.0 + jnp.tanh(c * (x + 0.044715 * (x * x * x))))


def _dot(a, b):
    return jnp.dot(a, b, preferred_element_type=F32)


def _dot_nt(a, b):
    return lax.dot_general(a, b, (((1,), (1,)), ((), ())), preferred_element_type=F32)


def _split3(a):
    hi = a.astype(BF16)
    r1 = a - hi.astype(F32)
    mid = r1.astype(BF16)
    lo = (r1 - mid.astype(F32)).astype(BF16)
    return hi, mid, lo


def _dot_exact_lhs(m01, parts):
    hi, mid, lo = parts
    return _dot(m01, hi) + _dot(m01, mid) + _dot(m01, lo)


def _col_bcast(row):
    return jnp.broadcast_to(row, (LANES, LANES)).T


def _tri_masks():
    r = lax.broadcasted_iota(jnp.int32, (CHUNK, CHUNK), 0)
    c = lax.broadcasted_iota(jnp.int32, (CHUNK, CHUNK), 1)
    lower = r >= c
    tri = jnp.where(lower, 1.0, 0.0).astype(BF16)
    tri_t = jnp.where(r <= c, 1.0, 0.0).astype(BF16)
    return lower, tri, tri_t


def _staggered_row_specs(tm, n_tiles, n_steps, n_split):
    assert n_split < n_steps and tm % n_split == 0
    tq = tm // n_split

    def spec(q):
        def index(i, j):
            nxt = jnp.minimum(i + (j >= n_steps - n_split + q).astype(jnp.int32), n_tiles - 1)
            return (nxt * n_split + q, 0)
        return pl.BlockSpec((tq, D_MODEL), index)

    return [spec(q) for q in range(n_split)]


def _inproj_kernel(n_split, n_col_steps, *refs):
    x_parts = refs[:n_split]
    nw_ref, w_ref, wdt_ref, proj_ref, dt_ref, hn_ref = refs[n_split:]
    tq = x_parts[0].shape[0]

    @pl.when(pl.program_id(1) == 0)
    def _():
        for q, x_ref in enumerate(x_parts):
            hn_ref[q * tq:(q + 1) * tq, :] = _rms(x_ref[...], nw_ref[...]).astype(BF16)
        dt_ref[...] = _dot(hn_ref[...], wdt_ref[...])

    tn = w_ref.shape[1]
    gate_lo, gate_hi = COL_G * RET_WIDTH, COL_XS * RET_WIDTH
    for step in range(n_col_steps):
        lo = min(max(gate_lo - step * tn, 0), tn)
        hi = min(max(gate_hi - step * tn, 0), tn)

        @pl.when(pl.program_id(1) == step)
        def _():
            r = _dot(hn_ref[...], w_ref[...])
            if lo > 0:
                proj_ref[:, :lo] = r[:, :lo].astype(BF16)
            if hi > lo:
                proj_ref[:, lo:hi] = _silu(r[:, lo:hi]).astype(BF16)
            if hi < tn:
                proj_ref[:, hi:] = r[:, hi:].astype(BF16)


def _in_projection(x2, norm_w, w_all, w_dt, tm=1024, tn=N_MAIN // 4, n_split=2):
    t = x2.shape[0]
    return pl.pallas_call(
        functools.partial(_inproj_kernel, n_split, N_MAIN // tn),
        grid=(t // tm, N_MAIN // tn),
        in_specs=_staggered_row_specs(tm, t // tm, N_MAIN // tn, n_split) + [
            pl.BlockSpec((1, D_MODEL), lambda i, j: (0, 0)),
            pl.BlockSpec((D_MODEL, tn), lambda i, j: (0, j)),
            pl.BlockSpec((D_MODEL, LANES), lambda i, j: (0, 0)),
        ],
        out_specs=[
            pl.BlockSpec((tm, tn), lambda i, j: (i, j)),
            pl.BlockSpec((tm, LANES), lambda i, j: (i, 0)),
        ],
        out_shape=[
            jax.ShapeDtypeStruct((t, N_MAIN), BF16),
            jax.ShapeDtypeStruct((t, LANES), F32),
        ],
        scratch_shapes=[pltpu.VMEM((tm, D_MODEL), BF16)],
        compiler_params=_cparams(2),
        name="in_projection",
    )(*([x2] * n_split), norm_w, w_all, w_dt)


def _conv5_silu(prev, cur, nxt, w_ref, b_ref, has_prev, has_next):
    zero = jnp.zeros_like(prev)
    depth = 2 * LANES
    ext = jnp.concatenate(
        [jnp.where(has_prev, prev, zero), cur, jnp.where(has_next, nxt, zero),
         jnp.zeros((depth - CHUNK - 2 * HALO, cur.shape[1]), cur.dtype)], axis=0)
    r = lax.broadcasted_iota(jnp.int32, (CHUNK, depth), 0)
    c = lax.broadcasted_iota(jnp.int32, (CHUNK, depth), 1)
    pad = SSD_CONV // 2
    acc = b_ref[...] + w_ref[pad:pad + 1, :] * cur.astype(F32)
    for j in range(SSD_CONV):
        if j != pad:
            shift = jnp.where(c == r + (HALO + j - pad), 1.0, 0.0).astype(BF16)
            acc = acc + w_ref[j:j + 1, :] * _dot(shift, ext)
    return _silu(acc)


def _expand_rows(w, tot_row, expand01):
    stacked = jnp.concatenate([w, jnp.broadcast_to(tot_row, (HALO, LANES))], axis=0)
    e = _dot(stacked.astype(BF16), expand01)
    return e[:CHUNK], e[CHUNK:CHUNK + 1]


def _state_increment(bm_f32, xw):
    parts = []
    for g in range(SSD_GROUPS):
        bm_t = bm_f32[:, g * SSD_STATE:(g + 1) * SSD_STATE].T.astype(BF16)
        parts.append(_dot(bm_t, xw[:, g * GROUP_W:(g + 1) * GROUP_W]))
    return jnp.concatenate(parts, axis=1)


def _cast_slab_specs(weights, n_steps, nc):
    in_specs, out_specs, out_shapes, jobs, flat = [], [], [], [], []
    for entry in weights:
        srcs = entry[:2] if isinstance(entry, tuple) else (entry,)
        tile = entry[2] if isinstance(entry, tuple) else 0
        n_rows, n_cols = srcs[0].shape
        rows = next(r for r in range(HALO, n_rows + 1, HALO)
                    if n_rows % r == 0 and n_rows // r <= n_steps)
        n_active = n_rows // rows

        def index(bi, i, _last=n_active - 1):
            return (jnp.minimum(bi * nc + i, _last), 0)

        for w in srcs:
            in_specs.append(pl.BlockSpec((rows, n_cols), index))
            flat.append(w)
        if tile:
            n_tiles = n_cols // tile
            out_specs.append(pl.BlockSpec((n_tiles, rows, 2 * tile),
                                          lambda bi, i, _index=index: (0,) + _index(bi, i)))
            out_shapes.append(jax.ShapeDtypeStruct((n_tiles, n_rows, 2 * tile), BF16))
        else:
            out_specs.append(pl.BlockSpec((rows, n_cols), index))
            out_shapes.append(jax.ShapeDtypeStruct((n_rows, n_cols), BF16))
        jobs.append((n_active, len(srcs), tile))
    return in_specs, out_specs, out_shapes, tuple(jobs), flat


def _sweep_with_casts(body, jobs, n_in, n_out, *refs):
    n_src = sum(job[1] for job in jobs)
    ins, refs = refs[:n_in], refs[n_in:]
    srcs, refs = refs[:n_src], refs[n_src:]
    outs, refs = refs[:n_out], refs[n_out:]
    dsts, scratch = refs[:len(jobs)], refs[len(jobs):]
    step = pl.program_id(0) * pl.num_programs(1) + pl.program_id(1)
    for (n_active, n_job_src, tile), dst in zip(jobs, dsts):
        job_srcs, srcs = srcs[:n_job_src], srcs[n_job_src:]

        @pl.when(step < n_active)
        def _():
            if tile:
                for t in range(dst.shape[0]):
                    for k, src in enumerate(job_srcs):
                        dst[t, :, k * tile:(k + 1) * tile] = (
                            src[:, t * tile:(t + 1) * tile].astype(BF16))
            else:
                dst[...] = job_srcs[0][...].astype(BF16)
    body(*ins, *outs, *scratch)


def _prep_kernel(ret_dec_b,
                 pos_ref, q_ref, k_ref, v_ref,
                 xs_ref, xsp_ref, xsn_ref, bc_ref, bcp_ref, bcn_ref, dt_ref,
                 cwx_ref, cbx_ref, cwb_ref, cbb_ref, dtbias_ref, arow2_ref,
                 freq_ref, phase_ref, kb_ref, expand_ref,
                 qr_ref, kr_ref, krt_ref, xbc_ref, rb_ref, sb_ref,
                 rb_state, sb_state):
    i = pl.program_id(1)
    nc = pl.num_programs(1)

    @pl.when(i == 0)
    def _():
        rb_state[...] = jnp.zeros_like(rb_state)
        sb_state[...] = jnp.zeros_like(sb_state)

    has_next = i > 0
    has_prev = i < nc - 1

    half = RET_HEAD_DIM // 2
    ang_t = freq_ref[...] * pos_ref[...].astype(F32) - phase_ref[...]
    cs = jnp.cos(ang_t).T
    sc = pltpu.roll(cs, half, axis=1)
    lane_lo = lax.broadcasted_iota(jnp.int32, (CHUNK, LANES), 1) < half
    cosv = jnp.where(lane_lo, cs, sc)
    sinv = jnp.where(lane_lo, -sc, cs)

    def rot(t, c, s):
        return t * c + pltpu.roll(t, half, axis=1) * s

    kscale = RET_HEAD_DIM ** -0.5
    cosk = cosv * kscale
    sink = sinv * kscale
    for h in range(RET_HEADS):
        sl = slice(h * RET_HEAD_DIM, (h + 1) * RET_HEAD_DIM)
        qr_ref[:, sl] = rot(q_ref[:, sl].astype(F32), cosv, sinv).astype(BF16)
        kr = rot(k_ref[:, sl].astype(F32), cosk, sink)
        kr_ref[:, sl] = kr.astype(BF16)
        kt = kr.T
        krt_ref[h] = kt.astype(BF16)
        rb_ref[h] = rb_state[h].astype(BF16)
        rb_state[h] = rb_state[h] * ret_dec_b[h] + _dot((kt * kb_ref[h:h + 1, :]).astype(BF16),
                                                        v_ref[:, sl])

    xs = _conv5_silu(xsp_ref[...], xs_ref[...], xsn_ref[...], cwx_ref, cbx_ref,
                     has_prev, has_next)
    bc = _conv5_silu(bcp_ref[...], bc_ref[...], bcn_ref[...], cwb_ref, cbb_ref,
                     has_prev, has_next)
    xbc_ref[:, :SSD_WIDTH] = xs.astype(BF16)
    xbc_ref[:, SSD_WIDTH:] = bc.astype(BF16)

    _, _, tri_t = _tri_masks()
    dt = _softplus(dt_ref[...] + dtbias_ref[...])
    rcs = _dot_exact_lhs(tri_t, _split3(dt * arow2_ref[...]))
    tot = rcs[0:1, :]
    w = jnp.exp2(tot - rcs) * dt
    wexp, cdec = _expand_rows(w, jnp.exp2(tot), expand_ref[...])
    xw = (xs * wexp).astype(BF16)
    ds = _state_increment(bc[:, :SSD_GROUPS * SSD_STATE], xw)
    sb_ref[...] = sb_state[...].astype(BF16)
    sb_state[...] = sb_state[...] * cdec + ds


def _prep_call(proj3, dt3, pos4, consts, ret_dec_b, cast_weights):
    b, l, _ = proj3.shape
    nc = l // CHUNK
    rows16 = l // HALO
    per16 = CHUNK // HALO

    def cix(i):
        return nc - 1 - i

    def col(cb):
        return lambda bi, i: (bi, cix(i), cb)

    def prev_halo(cb):
        return lambda bi, i: (bi, jnp.maximum(cix(i) * per16 - 1, 0), cb)

    def next_halo(cb):
        return lambda bi, i: (bi, jnp.minimum((cix(i) + 1) * per16, rows16 - 1), cb)

    def const(shape):
        return pl.BlockSpec(shape, lambda bi, i: (0,) * len(shape))

    in_specs = [
        pl.BlockSpec((None, None, 1, CHUNK), lambda bi, i: (bi, cix(i), 0, 0)),
        pl.BlockSpec((None, CHUNK, RET_WIDTH), col(COL_Q)),
        pl.BlockSpec((None, CHUNK, RET_WIDTH), col(COL_K)),
        pl.BlockSpec((None, CHUNK, RET_WIDTH), col(COL_V)),
        pl.BlockSpec((None, CHUNK, SSD_WIDTH), col(COL_XS)),
        pl.BlockSpec((None, HALO, SSD_WIDTH), prev_halo(COL_XS)),
        pl.BlockSpec((None, HALO, SSD_WIDTH), next_halo(COL_XS)),
        pl.BlockSpec((None, CHUNK, SSD_BC), col(COL_BC)),
        pl.BlockSpec((None, HALO, SSD_BC), prev_halo(COL_BC)),
        pl.BlockSpec((None, HALO, SSD_BC), next_halo(COL_BC)),
        pl.BlockSpec((None, CHUNK, LANES), col(0)),
        const((SSD_CONV, SSD_WIDTH)), const((1, SSD_WIDTH)),
        const((SSD_CONV, SSD_BC)), const((1, SSD_BC)),
        const((1, LANES)), const((1, LANES)),
        const((LANES, LANES)), const((LANES, LANES)),
        const((RET_HEADS, LANES)),
        const((LANES, SSD_WIDTH)),
    ]
    out_specs = [
        pl.BlockSpec((None, CHUNK, RET_WIDTH), col(0)),
        pl.BlockSpec((None, CHUNK, RET_WIDTH), col(0)),
        pl.BlockSpec((None, None, RET_HEADS, RET_HEAD_DIM, CHUNK),
                     lambda bi, i: (bi, cix(i), 0, 0, 0)),
        pl.BlockSpec((None, CHUNK, SSD_CONV_DIM), col(0)),
        pl.BlockSpec((None, None, RET_HEADS, RET_HEAD_DIM, RET_HEAD_DIM),
                     lambda bi, i: (bi, cix(i), 0, 0, 0)),
        pl.BlockSpec((None, None, SSD_STATE, SSD_WIDTH), lambda bi, i: (bi, cix(i), 0, 0)),
    ]
    out_shape = [
        jax.ShapeDtypeStruct((b, l, RET_WIDTH), BF16),
        jax.ShapeDtypeStruct((b, l, RET_WIDTH), BF16),
        jax.ShapeDtypeStruct((b, nc, RET_HEADS, RET_HEAD_DIM, CHUNK), BF16),
        jax.ShapeDtypeStruct((b, l, SSD_CONV_DIM), BF16),
        jax.ShapeDtypeStruct((b, nc, RET_HEADS, RET_HEAD_DIM, RET_HEAD_DIM), BF16),
        jax.ShapeDtypeStruct((b, nc, SSD_STATE, SSD_WIDTH), BF16),
    ]
    scratch = [
        pltpu.VMEM((RET_HEADS, RET_HEAD_DIM, RET_HEAD_DIM), F32),
        pltpu.VMEM((SSD_STATE, SSD_WIDTH), F32),
    ]
    cast_in, cast_out, cast_shapes, active, cast_srcs = _cast_slab_specs(cast_weights, b * nc, nc)
    outs = pl.pallas_call(
        functools.partial(_sweep_with_casts, functools.partial(_prep_kernel, ret_dec_b),
                          active, len(in_specs), len(out_specs)),
        grid=(b, nc),
        in_specs=in_specs + cast_in,
        out_specs=out_specs + cast_out,
        out_shape=out_shape + cast_shapes,
        scratch_shapes=scratch,
        compiler_params=_cparams(2),
        name="reverse_sweep",
    )(pos4, proj3, proj3, proj3, proj3, proj3, proj3, proj3, proj3, proj3, dt3,
      consts["conv_w_xs"], consts["conv_b_xs"], consts["conv_w_bc"], consts["conv_b_bc"],
      consts["dt_bias"], consts["a_row2"], consts["rot_freq"], consts["rot_phase"],
      consts["ret_kb"], consts["expand_b"], *cast_srcs)
    return outs[:len(out_specs)], outs[len(out_specs):]


def _mix_kernel(ret_dec_f,
                qr_ref, kr_ref, krt_ref, v_ref, g_ref, z_ref, xbc_ref, dt_ref, rb_ref, sb_ref,
                mask_ref, qf_ref, qb_ref, kf_ref, retnw_ref,
                dtbias_ref, arow2_ref, expand_ref, expand2_ref, dexp_ref, ssdnw_ref,
                out_ref,
                rf_state, sf_state):
    i = pl.program_id(1)

    @pl.when(i == 0)
    def _():
        rf_state[...] = jnp.zeros_like(rf_state)
        sf_state[...] = jnp.zeros_like(sf_state)

    for h in range(RET_HEADS):
        sl = slice(h * RET_HEAD_DIM, (h + 1) * RET_HEAD_DIM)
        qh = qr_ref[:, sl]
        kh = kr_ref[:, sl]
        vh = v_ref[:, sl]
        s = (_dot_nt(qh, kh) * mask_ref[h]).astype(BF16)
        lhs = jnp.concatenate([s, qh * qf_ref[h], qh * qb_ref[h]], axis=1)
        rhs = jnp.concatenate([vh, rf_state[h].astype(BF16), rb_ref[h]], axis=0)
        o = _dot(lhs, rhs)
        rf_state[h] = rf_state[h] * ret_dec_f[h] + _dot(krt_ref[h], vh * kf_ref[h])
        o = _rms(o, retnw_ref[:, sl])
        out_ref[:, sl] = (g_ref[:, sl].astype(F32) * o).astype(BF16)

    lower, tri, tri_t = _tri_masks()
    dt = _softplus(dt_ref[...] + dtbias_ref[...])
    parts = _split3(dt * arow2_ref[...])
    lane = lax.broadcasted_iota(jnp.int32, (CHUNK, LANES), 1)
    prefix = _dot_exact_lhs(tri, parts)
    acs = jnp.where(lane < SSD_HEADS, prefix, _dot_exact_lhs(tri_t, parts))
    acs_t = acs.T
    src_t = (acs - jnp.log2(dt)).T
    edge = _dot(jnp.exp2(acs).astype(BF16), expand2_ref[...])
    xs16 = xbc_ref[:, :SSD_WIDTH]
    bm = xbc_ref[:, SSD_WIDTH:SSD_WIDTH + SSD_GROUPS * SSD_STATE]
    cm = xbc_ref[:, SSD_WIDTH + SSD_GROUPS * SSD_STATE:]
    lane_lo = lane < SSD_HEAD_DIM

    ys = []
    for g in range(SSD_GROUPS):
        gs = slice(g * SSD_STATE, (g + 1) * SSD_STATE)
        gw = slice(g * GROUP_W, (g + 1) * GROUP_W)
        gwb = slice(SSD_WIDTH + g * GROUP_W, SSD_WIDTH + (g + 1) * GROUP_W)
        cm_g = cm[:, gs]
        cb = _dot_nt(cm_g, bm[:, gs])
        y_off = (edge[:, gw] * _dot(cm_g, sf_state[:, gw].astype(BF16))
                 + edge[:, gwb] * _dot(cm_g, sb_ref[:, gw]))
        for pr in range(SSD_HEADS_PER_GROUP // 2):
            ps = slice(g * GROUP_W + pr * LANES, g * GROUP_W + (pr + 1) * LANES)
            pair = []
            for sub in range(2):
                e = g * SSD_HEADS_PER_GROUP + 2 * pr + sub
                eb = SSD_HEADS + e
                dst = jnp.where(lower, _col_bcast(acs_t[e:e + 1, :]), _col_bcast(acs_t[eb:eb + 1, :]))
                src = jnp.where(lower, src_t[e:e + 1, :], src_t[eb:eb + 1, :])
                m = (cb * jnp.exp2(dst - src)).astype(BF16)
                pair.append(_dot(m, xs16[:, ps]))
            ys.append(jnp.where(lane_lo, pair[0], pair[1]) + y_off[:, pr * LANES:(pr + 1) * LANES])
    y = jnp.concatenate(ys, axis=1)

    xs = xs16.astype(F32)
    y = (y + dexp_ref[...] * xs) * z_ref[...].astype(F32)
    for g in range(SSD_GROUPS):
        gs = slice(g * GROUP_W, (g + 1) * GROUP_W)
        out_ref[:, RET_WIDTH + g * GROUP_W:RET_WIDTH + (g + 1) * GROUP_W] = _rms(
            y[:, gs], ssdnw_ref[:, gs]).astype(BF16)

    tot = prefix[CHUNK - 1:CHUNK, :]
    w = jnp.exp2(tot - prefix) * dt
    wexp, cdec = _expand_rows(w, jnp.exp2(tot), expand_ref[...])
    xw = (xs * wexp).astype(BF16)
    ds = _state_increment(bm.astype(F32), xw)
    sf_state[...] = sf_state[...] * cdec + ds


def _mix_call(proj3, qr, kr, krt, xbc, dt3, rb, sb, consts, ret_dec_f, cast_weights):
    b, l, _ = proj3.shape
    nc = l // CHUNK

    def col(cb):
        return lambda bi, i: (bi, i, cb)

    def const(shape):
        return pl.BlockSpec(shape, lambda bi, i: (0,) * len(shape))

    in_specs = [
        pl.BlockSpec((None, CHUNK, RET_WIDTH), col(0)),
        pl.BlockSpec((None, CHUNK, RET_WIDTH), col(0)),
        pl.BlockSpec((None, None, RET_HEADS, RET_HEAD_DIM, CHUNK), lambda bi, i: (bi, i, 0, 0, 0)),
        pl.BlockSpec((None, CHUNK, RET_WIDTH), col(COL_V)),
        pl.BlockSpec((None, CHUNK, RET_WIDTH), col(COL_G)),
        pl.BlockSpec((None, CHUNK, SSD_WIDTH), col(COL_Z)),
        pl.BlockSpec((None, CHUNK, SSD_CONV_DIM), col(0)),
        pl.BlockSpec((None, CHUNK, LANES), col(0)),
        pl.BlockSpec((None, None, RET_HEADS, RET_HEAD_DIM, RET_HEAD_DIM),
                     lambda bi, i: (bi, i, 0, 0, 0)),
        pl.BlockSpec((None, None, SSD_STATE, SSD_WIDTH), lambda bi, i: (bi, i, 0, 0)),
        const((RET_HEADS, CHUNK, CHUNK)), const((RET_HEADS, CHUNK, LANES)),
        const((RET_HEADS, CHUNK, LANES)), const((RET_HEADS, CHUNK, LANES)), const((1, RET_WIDTH)),
        const((1, LANES)), const((1, LANES)), const((LANES, SSD_WIDTH)),
        const((LANES, 2 * SSD_WIDTH)), const((1, SSD_WIDTH)), const((1, SSD_WIDTH)),
    ]
    cast_in, cast_out, cast_shapes, active, cast_srcs = _cast_slab_specs(cast_weights, b * nc, nc)
    outs = pl.pallas_call(
        functools.partial(_sweep_with_casts, functools.partial(_mix_kernel, ret_dec_f),
                          active, len(in_specs), 1),
        grid=(b, nc),
        in_specs=in_specs + cast_in,
        out_specs=[pl.BlockSpec((None, CHUNK, D_MODEL), col(0))] + cast_out,
        out_shape=[jax.ShapeDtypeStruct((b, l, D_MODEL), BF16)] + cast_shapes,
        scratch_shapes=[
            pltpu.VMEM((RET_HEADS, RET_HEAD_DIM, RET_HEAD_DIM), F32),
            pltpu.VMEM((SSD_STATE, SSD_WIDTH), F32),
        ],
        compiler_params=_cparams(2),
        name="forward_sweep",
    )(qr, kr, krt, proj3, proj3, proj3, xbc, dt3, rb, sb,
      consts["ret_mask"], consts["ret_qf"], consts["ret_qb"], consts["ret_kf"],
      consts["ret_norm_w"], consts["dt_bias"], consts["a_row2"], consts["expand_f"],
      consts["expand_fb"], consts["d_exp"], consts["ssd_norm_w"], *cast_srcs)
    return outs[0], outs[1:]


def _outproj_kernel(mix_ref, w_ref, x_ref, nw_ref, h_ref, hn_ref):
    h = x_ref[...] + _dot(mix_ref[...], w_ref[...])
    h_ref[...] = h
    hn_ref[...] = _rms(h, nw_ref[...]).astype(BF16)


def _out_projection(mix2, w_out, x2, ffn_norm_w, tm=512):
    t = x2.shape[0]
    return pl.pallas_call(
        _outproj_kernel,
        grid=(t // tm,),
        in_specs=[
            pl.BlockSpec((tm, D_MODEL), lambda i: (i, 0)),
            pl.BlockSpec((D_MODEL, D_MODEL), lambda i: (0, 0)),
            pl.BlockSpec((tm, D_MODEL), lambda i: (i, 0)),
            pl.BlockSpec((1, D_MODEL), lambda i: (0, 0)),
        ],
        out_specs=[
            pl.BlockSpec((tm, D_MODEL), lambda i: (i, 0)),
            pl.BlockSpec((tm, D_MODEL), lambda i: (i, 0)),
        ],
        out_shape=[
            jax.ShapeDtypeStruct((t, D_MODEL), F32),
            jax.ShapeDtypeStruct((t, D_MODEL), BF16),
        ],
        compiler_params=_cparams(1),
        name="out_projection",
    )(mix2, w_out, x2, ffn_norm_w)


def _ffn_kernel(tiles_per_seq, hn_ref, hp_ref, hx_ref, wgu_ref, cw_ref, cb_ref, wd_ref,
                o_ref, hbuf, gu_s):
    i = pl.program_id(0)
    j = pl.program_id(1)
    tm = hn_ref.shape[0]
    tf = cw_ref.shape[1]

    @pl.when(j == 0)
    def _():
        pos_in_seq = i % tiles_per_seq
        zero = jnp.zeros((HALO, D_MODEL), BF16)
        hbuf[0:HALO, :] = jnp.where(pos_in_seq > 0, hp_ref[...], zero)
        hbuf[HALO:HALO + tm, :] = hn_ref[...]
        hbuf[HALO + tm:, :] = jnp.where(pos_in_seq < tiles_per_seq - 1, hx_ref[...], zero)
        o_ref[...] = jnp.zeros_like(o_ref)

    gu_s[...] = _dot(hbuf[...], wgu_ref[...])
    pad = FFN_CONV // 2
    gate = cb_ref[...]
    for t in range(FFN_CONV):
        o = HALO + t - pad
        gate = gate + cw_ref[t:t + 1, :] * gu_s[o:o + tm, :tf]
    act = (_gelu_tanh(gate) * gu_s[HALO:HALO + tm, tf:]).astype(BF16)
    o_ref[...] += _dot(act, wd_ref[...])


def _ffn_call(hn2, wgu, conv_w, conv_b, wd, seq_len, tm=1024):
    t = hn2.shape[0]
    tf = wgu.shape[2] // 2
    per16 = tm // HALO
    rows16 = t // HALO
    return pl.pallas_call(
        functools.partial(_ffn_kernel, seq_len // tm),
        grid=(t // tm, D_FF // tf),
        in_specs=[
            pl.BlockSpec((tm, D_MODEL), lambda i, j: (i, 0)),
            pl.BlockSpec((HALO, D_MODEL), lambda i, j: (jnp.maximum(i * per16 - 1, 0), 0)),
            pl.BlockSpec((HALO, D_MODEL), lambda i, j: (jnp.minimum((i + 1) * per16, rows16 - 1), 0)),
            pl.BlockSpec((None, D_MODEL, 2 * tf), lambda i, j: (j, 0, 0)),
            pl.BlockSpec((FFN_CONV, tf), lambda i, j: (0, j)),
            pl.BlockSpec((1, tf), lambda i, j: (0, j)),
            pl.BlockSpec((tf, D_MODEL), lambda i, j: (j, 0)),
        ],
        out_specs=pl.BlockSpec((tm, D_MODEL), lambda i, j: (i, 0)),
        out_shape=jax.ShapeDtypeStruct((t, D_MODEL), F32),
        scratch_shapes=[
            pltpu.VMEM((tm + 2 * HALO, D_MODEL), BF16),
            pltpu.VMEM((tm + 2 * HALO, 2 * tf), F32),
        ],
        compiler_params=_cparams(2),
        name="conv_glu_ffn",
    )(hn2, hn2, hn2, wgu, conv_w, conv_b, wd)


def _ple_kernel(apply_final, h_ref, d_ref, p_ref, nw_ref, wg_ref, bg_ref, wp_ref, fw_ref, o_ref):
    h = h_ref[...] + d_ref[...]
    hn = _rms(h, nw_ref[...]).astype(BF16)
    gate = jax.nn.sigmoid(_dot(hn, wg_ref[...]) + bg_ref[...])
    h = h + gate * _dot(p_ref[...].astype(BF16), wp_ref[...])
    if apply_final:
        h = _rms(h, fw_ref[...])
    o_ref[...] = h


def _ple_call(h2, delta2, p2, norm_w, wg, bg, wp, final_w, apply_final, tm=512):
    t = h2.shape[0]
    return pl.pallas_call(
        functools.partial(_ple_kernel, apply_final),
        grid=(t // tm,),
        in_specs=[
            pl.BlockSpec((tm, D_MODEL), lambda i: (i, 0)),
            pl.BlockSpec((tm, D_MODEL), lambda i: (i, 0)),
            pl.BlockSpec((tm, D_PLE), lambda i: (i, 0)),
            pl.BlockSpec((1, D_MODEL), lambda i: (0, 0)),
            pl.BlockSpec((D_MODEL, D_MODEL), lambda i: (0, 0)),
            pl.BlockSpec((1, D_MODEL), lambda i: (0, 0)),
            pl.BlockSpec((D_PLE, D_MODEL), lambda i: (0, 0)),
            pl.BlockSpec((1, D_MODEL), lambda i: (0, 0)),
        ],
        out_specs=pl.BlockSpec((tm, D_MODEL), lambda i: (i, 0)),
        out_shape=jax.ShapeDtypeStruct((t, D_MODEL), F32),
        compiler_params=_cparams(1),
        name="ple_gate",
    )(h2, delta2, p2, norm_w, wg, bg, wp, final_w)


def _retention_tables():
    hh = np.arange(RET_HEADS, dtype=np.float64)
    lf = np.log1p(-np.exp2(-5.0 - hh))
    lb = np.log1p(-np.exp2(-5.5 - hh))
    idx = np.arange(CHUNK, dtype=np.float64)
    dist = idx[:, None] - idx[None, :]
    mask = np.where(dist >= 0, np.exp(lf[:, None, None] * np.abs(dist)),
                    np.exp(lb[:, None, None] * np.abs(dist)))
    ones = np.ones((1, 1, LANES))
    qf = np.exp(lf[:, None] * (idx + 1.0)[None, :])[:, :, None] * ones
    qb = np.exp(lb[:, None] * (CHUNK - idx)[None, :])[:, :, None] * ones
    kf = np.exp(lf[:, None] * (CHUNK - 1.0 - idx)[None, :])[:, :, None] * ones
    kb = np.exp(lb[:, None] * idx[None, :])
    dec_f = tuple(float(v) for v in np.exp(lf * CHUNK))
    dec_b = tuple(float(v) for v in np.exp(lb * CHUNK))
    f = lambda a: jnp.asarray(a, F32)
    h = lambda a: jnp.asarray(a, BF16)
    return dict(ret_mask=f(mask), ret_qf=h(qf), ret_qb=h(qb), ret_kf=h(kf), ret_kb=f(kb)), dec_f, dec_b


def _expand_matrix(first_row):
    e = np.zeros((LANES, SSD_WIDTH), np.float32)
    for h in range(SSD_HEADS):
        e[first_row + h, h * SSD_HEAD_DIM:(h + 1) * SSD_HEAD_DIM] = 1.0
    return e


def _rotary_tables():
    half = RET_HEAD_DIM // 2
    inv_freq = ROPE_BASE ** (-jnp.arange(half, dtype=F32) / half)
    freq = jnp.broadcast_to(jnp.concatenate([inv_freq, inv_freq])[:, None], (LANES, CHUNK))
    phase = jnp.broadcast_to(
        jnp.concatenate([jnp.zeros((half,), F32), jnp.full((half,), math.pi / 2, F32)])[:, None],
        (LANES, CHUNK))
    return freq, phase


def _pad_lanes(v):
    return jnp.pad(v.reshape(1, -1), ((0, 0), (0, LANES - v.size)))


def kernel(x, p, positions, norm_mix_w, w_in, ret_norm_w, ssd_conv_w, ssd_conv_b, ssd_dt_bias,
           ssd_a_log, ssd_d, ssd_norm_w, w_out, norm_ffn_w, ffn_w_gate, ffn_w_up, ffn_conv_w,
           ffn_conv_b, ffn_w_down, ple_norm_w, ple_w_gate, ple_b_gate, ple_w_proj, final_norm_w):
    b, l, _ = x.shape
    depth = w_in.shape[0]
    t = b * l
    nc = l // CHUNK
    row = lambda v: v.reshape(1, -1).astype(F32)

    tables, dec_f, dec_b = _retention_tables()
    rot_freq, rot_phase = _rotary_tables()
    exp_f, exp_b = _expand_matrix(0), _expand_matrix(SSD_HEADS)
    rot = dict(
        rot_freq=rot_freq, rot_phase=rot_phase,
        expand_f=jnp.asarray(exp_f, BF16),
        expand_b=jnp.asarray(exp_b, BF16),
        expand_fb=jnp.asarray(np.concatenate([exp_f, exp_b], axis=1), BF16),
    )
    pos4 = positions.reshape(b, nc, 1, CHUNK)

    h = x.reshape(t, D_MODEL)
    for i in range(depth):
        consts = dict(tables)
        consts.update(rot)
        consts.update(
            conv_w_xs=ssd_conv_w[i][:, :SSD_WIDTH], conv_b_xs=row(ssd_conv_b[i][:SSD_WIDTH]),
            conv_w_bc=ssd_conv_w[i][:, SSD_WIDTH:], conv_b_bc=row(ssd_conv_b[i][SSD_WIDTH:]),
            dt_bias=_pad_lanes(ssd_dt_bias[i]),
            a_row2=_pad_lanes(-jnp.exp(ssd_a_log[i].astype(F32)) * LOG2E),
            ret_norm_w=row(ret_norm_w[i]),
            d_exp=row(jnp.repeat(ssd_d[i], SSD_HEAD_DIM)),
            ssd_norm_w=row(ssd_norm_w[i]),
        )
        w_all = w_in[i].astype(BF16)
        w_dt = jnp.pad(w_all[:, N_MAIN:], ((0, 0), (0, LANES - N_DT)))

        proj, dt = _in_projection(h, row(norm_mix_w[i]), w_all, w_dt)
        proj3 = proj.reshape(b, l, N_MAIN)
        dt3 = dt.reshape(b, l, LANES)
        (qr, kr, krt, xbc, rb, sb), (wgu16,) = _prep_call(
            proj3, dt3, pos4, consts, dec_b, [(ffn_w_gate[i], ffn_w_up[i], FFN_TILE)])
        mix, (wd16, wo16, wpg16) = _mix_call(
            proj3, qr, kr, krt, xbc, dt3, rb, sb, consts, dec_f,
            [ffn_w_down[i], w_out[i], ple_w_gate[i]])
        h, hn = _out_projection(mix.reshape(t, D_MODEL), wo16, h, row(norm_ffn_w[i]))
        delta = _ffn_call(hn, wgu16, ffn_conv_w[i], row(ffn_conv_b[i]), wd16, l)
        h = _ple_call(h, delta, p[i].reshape(t, D_PLE), row(ple_norm_w[i]), wpg16,
                      row(ple_b_gate[i]), ple_w_proj[i].astype(BF16), row(final_norm_w),
                      apply_final=(i == depth - 1))
    return h.reshape(b, l, D_MODEL)
```

```python
import functools
import math

import numpy as np
import jax
import jax.numpy as jnp
from jax import lax
from jax.experimental import pallas as pl
from jax.experimental.pallas import tpu as pltpu

F32 = jnp.float32
BF16 = jnp.bfloat16

D_MODEL = 2048
EPS = 1e-6
D_PLE = 256
RET_WIDTH = D_MODEL // 2
RET_HEAD_DIM = 128
RET_HEADS = RET_WIDTH // RET_HEAD_DIM
ROPE_BASE = 10000.0
SSD_WIDTH = D_MODEL - RET_WIDTH
SSD_HEAD_DIM = 64
SSD_HEADS = SSD_WIDTH // SSD_HEAD_DIM
SSD_GROUPS = 2
SSD_HEADS_PER_GROUP = SSD_HEADS // SSD_GROUPS
SSD_STATE = 128
SSD_CONV = 5
SSD_BC = 2 * SSD_GROUPS * SSD_STATE
SSD_CONV_DIM = SSD_WIDTH + SSD_BC
D_FF = (11 * D_MODEL) // 4
FFN_CONV = 3
N_MAIN = 4 * RET_WIDTH + SSD_WIDTH + SSD_CONV_DIM
N_DT = 2 * SSD_HEADS

CHUNK = 128
LANES = 128
HALO = 16
GROUP_W = SSD_WIDTH // SSD_GROUPS
FFN_TILE = 512

COL_Q, COL_K, COL_V, COL_G, COL_Z, COL_XS = 0, 1, 2, 3, 4, 5
COL_BC = (5 * RET_WIDTH + SSD_WIDTH) // SSD_BC

VMEM_LIMIT = 56 * 1024 * 1024
LOG2E = math.log2(math.e)


def _cparams(n_axes):
    return pltpu.CompilerParams(dimension_semantics=("arbitrary",) * n_axes,
                                vmem_limit_bytes=VMEM_LIMIT)


def _rms(xf, w_row):
    ms = jnp.mean(xf * xf, axis=-1, keepdims=True)
    return xf * lax.rsqrt(ms + EPS) * w_row


def _silu(x):
    return x * jax.nn.sigmoid(x)


def _softplus(x):
    return jnp.maximum(x, 0.0) + jnp.log1p(jnp.exp(-jnp.abs(x)))


def _gelu_tanh(x):
    c = math.sqrt(2.0 / math.pi)
    return 0.5 * x * (1.0 + jnp.tanh(c * (x + 0.044715 * (x * x * x))))


def _dot(a, b):
    return jnp.dot(a, b, preferred_element_type=F32)


def _dot_nt(a, b):
    return lax.dot_general(a, b, (((1,), (1,)), ((), ())), preferred_element_type=F32)


def _split3(a):
    hi = a.astype(BF16)
    r1 = a - hi.astype(F32)
    mid = r1.astype(BF16)
    lo = (r1 - mid.astype(F32)).astype(BF16)
    return hi, mid, lo


def _dot_exact_lhs(m01, parts):
    hi, mid, lo = parts
    return _dot(m01, hi) + _dot(m01, mid) + _dot(m01, lo)


def _col_bcast(row):
    return jnp.broadcast_to(row, (LANES, LANES)).T


def _tri_masks():
    r = lax.broadcasted_iota(jnp.int32, (CHUNK, CHUNK), 0)
    c = lax.broadcasted_iota(jnp.int32, (CHUNK, CHUNK), 1)
    lower = r >= c
    tri = jnp.where(lower, 1.0, 0.0).astype(BF16)
    tri_t = jnp.where(r <= c, 1.0, 0.0).astype(BF16)
    return lower, tri, tri_t


def _staggered_row_specs(tm, n_tiles, n_steps, n_split):
    assert n_split < n_steps and tm % n_split == 0
    tq = tm // n_split

    def spec(q):
        def index(i, j):
            nxt = jnp.minimum(i + (j >= n_steps - n_split + q).astype(jnp.int32), n_tiles - 1)
            return (nxt * n_split + q, 0)
        return pl.BlockSpec((tq, D_MODEL), index)

    return [spec(q) for q in range(n_split)]


def _inproj_kernel(n_split, n_col_steps, side_work, *refs):
    x_parts = refs[:n_split]
    (nw_ref, w_ref, wdt_ref, pos_ref, freq_ref, phase_ref,
     proj_ref, dt_ref, cs_ref, hn_ref) = refs[n_split:]
    tq = x_parts[0].shape[0]

    @pl.when(pl.program_id(1) == 0)
    def _():
        for q, x_ref in enumerate(x_parts):
            hn_ref[q * tq:(q + 1) * tq, :] = _rms(x_ref[...], nw_ref[...]).astype(BF16)
        dt_ref[...] = _dot(hn_ref[...], wdt_ref[...])

    tn = w_ref.shape[1]
    gate_lo, gate_hi = COL_G * RET_WIDTH, COL_XS * RET_WIDTH
    for step in range(n_col_steps):
        lo = min(max(gate_lo - step * tn, 0), tn)
        hi = min(max(gate_hi - step * tn, 0), tn)

        @pl.when(pl.program_id(1) == step)
        def _():
            side_work()
            if step == n_col_steps - 1:
                ang_t = freq_ref[...] * pos_ref[...].astype(F32) - phase_ref[...]
                cs_ref[...] = jnp.cos(ang_t).T
            r = _dot(hn_ref[...], w_ref[...])
            if lo > 0:
                proj_ref[:, :lo] = r[:, :lo].astype(BF16)
            if hi > lo:
                proj_ref[:, lo:hi] = _silu(r[:, lo:hi]).astype(BF16)
            if hi < tn:
                proj_ref[:, hi:] = r[:, hi:].astype(BF16)


def _in_projection(x2, norm_w, w_all, w_dt, positions, cast_weights,
                   tm=1024, tn=N_MAIN // 4, n_split=2):
    t = x2.shape[0]
    n_rows, n_cols = t // tm, N_MAIN // tn
    half = RET_HEAD_DIM // 2
    inv_freq = ROPE_BASE ** (-jnp.arange(half, dtype=F32) / half)
    freq = jnp.broadcast_to(jnp.concatenate([inv_freq, inv_freq])[:, None], (LANES, tm))
    phase = jnp.broadcast_to(
        jnp.concatenate([jnp.zeros((half,), F32), jnp.full((half,), math.pi / 2, F32)])[:, None],
        (LANES, tm))
    in_specs = _staggered_row_specs(tm, n_rows, n_cols, n_split) + [
        pl.BlockSpec((1, D_MODEL), lambda i, j: (0, 0)),
        pl.BlockSpec((D_MODEL, tn), lambda i, j: (0, j)),
        pl.BlockSpec((D_MODEL, LANES), lambda i, j: (0, 0)),
        pl.BlockSpec((None, 1, tm), lambda i, j: (i, 0, 0)),
        pl.BlockSpec((LANES, tm), lambda i, j: (0, 0)),
        pl.BlockSpec((LANES, tm), lambda i, j: (0, 0)),
    ]
    out_specs = [
        pl.BlockSpec((tm, tn), lambda i, j: (i, j)),
        pl.BlockSpec((tm, LANES), lambda i, j: (i, 0)),
        pl.BlockSpec((tm, LANES), lambda i, j: (i, 0)),
    ]
    out_shape = [
        jax.ShapeDtypeStruct((t, N_MAIN), BF16),
        jax.ShapeDtypeStruct((t, LANES), F32),
        jax.ShapeDtypeStruct((t, LANES), F32),
    ]
    cast_in, cast_out, cast_shapes, jobs, cast_srcs = _cast_slab_specs(
        cast_weights, n_rows * n_cols, n_cols)
    outs = pl.pallas_call(
        functools.partial(_sweep_with_casts, functools.partial(_inproj_kernel, n_split, n_cols),
                          jobs, n_rows * n_cols, len(in_specs), len(out_specs)),
        grid=(n_rows, n_cols),
        in_specs=in_specs + cast_in,
        out_specs=out_specs + cast_out,
        out_shape=out_shape + cast_shapes,
        scratch_shapes=[pltpu.VMEM((tm, D_MODEL), BF16)],
        compiler_params=_cparams(2),
        name="in_projection",
    )(*([x2] * n_split), norm_w, w_all, w_dt, positions.reshape(n_rows, 1, tm), freq, phase,
      *cast_srcs)
    return outs[:3], outs[3:]


def _conv5_silu(prev, cur, nxt, w_ref, b_ref, has_prev, has_next):
    zero = jnp.zeros_like(prev)
    depth = 2 * LANES
    ext = jnp.concatenate(
        [jnp.where(has_prev, prev, zero), cur, jnp.where(has_next, nxt, zero),
         jnp.zeros((depth - CHUNK - 2 * HALO, cur.shape[1]), cur.dtype)], axis=0)
    r = lax.broadcasted_iota(jnp.int32, (CHUNK, depth), 0)
    c = lax.broadcasted_iota(jnp.int32, (CHUNK, depth), 1)
    pad = SSD_CONV // 2
    acc = b_ref[...] + w_ref[pad:pad + 1, :] * cur.astype(F32)
    for j in range(SSD_CONV):
        if j != pad:
            shift = jnp.where(c == r + (HALO + j - pad), 1.0, 0.0).astype(BF16)
            acc = acc + w_ref[j:j + 1, :] * _dot(shift, ext)
    return _silu(acc)


def _expand_rows(w, tot_row, expand01):
    stacked = jnp.concatenate([w, jnp.broadcast_to(tot_row, (HALO, LANES))], axis=0)
    e = _dot(stacked.astype(BF16), expand01)
    return e[:CHUNK], e[CHUNK:CHUNK + 1]


def _state_increment(bm_f32, xw):
    parts = []
    for g in range(SSD_GROUPS):
        bm_t = bm_f32[:, g * SSD_STATE:(g + 1) * SSD_STATE].T.astype(BF16)
        parts.append(_dot(bm_t, xw[:, g * GROUP_W:(g + 1) * GROUP_W]))
    return jnp.concatenate(parts, axis=1)


def _cast_slab_specs(weights, n_steps, nc):
    in_specs, out_specs, out_shapes, jobs, flat = [], [], [], [], []
    for entry in weights:
        srcs = entry[:2] if isinstance(entry, tuple) else (entry,)
        tile = entry[2] if isinstance(entry, tuple) else 0
        n_rows, n_cols = srcs[0].shape
        rows = next(r for r in range(HALO, n_rows + 1, HALO)
                    if n_rows % r == 0 and n_rows // r <= n_steps)
        n_active = n_rows // rows

        def index(bi, i, _last=n_active - 1):
            return (jnp.minimum(bi * nc + i, _last), 0)

        for w in srcs:
            in_specs.append(pl.BlockSpec((rows, n_cols), index))
            flat.append(w)
        if tile:
            n_tiles = n_cols // tile
            out_specs.append(pl.BlockSpec((n_tiles, rows, 2 * tile),
                                          lambda bi, i, _index=index: (0,) + _index(bi, i)))
            out_shapes.append(jax.ShapeDtypeStruct((n_tiles, n_rows, 2 * tile), BF16))
        else:
            out_specs.append(pl.BlockSpec((rows, n_cols), index))
            out_shapes.append(jax.ShapeDtypeStruct((n_rows, n_cols), BF16))
        jobs.append((n_active, len(srcs), tile))
    return in_specs, out_specs, out_shapes, tuple(jobs), flat


def _sweep_with_casts(body, jobs, n_steps, n_in, n_out, *refs):
    n_src = sum(job[1] for job in jobs)
    ins, refs = refs[:n_in], refs[n_in:]
    srcs, refs = refs[:n_src], refs[n_src:]
    outs, refs = refs[:n_out], refs[n_out:]
    dsts, scratch = refs[:len(jobs)], refs[len(jobs):]
    step = pl.program_id(0) * pl.num_programs(1) + pl.program_id(1)

    def narrow(job_srcs, dst, tile):
        if tile:
            for t in range(dst.shape[0]):
                for k, src in enumerate(job_srcs):
                    dst[t, :, k * tile:(k + 1) * tile] = src[:, t * tile:(t + 1) * tile].astype(BF16)
        else:
            dst[...] = job_srcs[0][...].astype(BF16)

    every_step = []
    for (n_active, n_job_src, tile), dst in zip(jobs, dsts):
        job_srcs, srcs = srcs[:n_job_src], srcs[n_job_src:]
        if n_active == n_steps:
            every_step.append(functools.partial(narrow, job_srcs, dst, tile))
        else:
            pl.when(step < n_active)(functools.partial(narrow, job_srcs, dst, tile))

    def side_work():
        for f in every_step:
            f()

    body(side_work, *ins, *outs, *scratch)


def _prep_kernel(ret_dec_b, side_work,
                 cs_ref, q_ref, k_ref, v_ref,
                 xs_ref, xsp_ref, xsn_ref, bc_ref, bcp_ref, bcn_ref, dt_ref,
                 cwx_ref, cbx_ref, cwb_ref, cbb_ref, dtbias_ref, arow2_ref,
                 kb_ref, expand_ref,
                 qr_ref, kr_ref, krt_ref, xbc_ref, rb_ref, sb_ref,
                 rb_state, sb_state):
    i = pl.program_id(1)
    nc = pl.num_programs(1)

    @pl.when(i == 0)
    def _():
        rb_state[...] = jnp.zeros_like(rb_state)
        sb_state[...] = jnp.zeros_like(sb_state)

    side_work()
    has_next = i > 0
    has_prev = i < nc - 1

    half = RET_HEAD_DIM // 2
    cs = cs_ref[...]
    sc = pltpu.roll(cs, half, axis=1)
    lane_lo = lax.broadcasted_iota(jnp.int32, (CHUNK, LANES), 1) < half
    cosv = jnp.where(lane_lo, cs, sc)
    sinv = jnp.where(lane_lo, -sc, cs)

    def rot(t, c, s):
        return t * c + pltpu.roll(t, half, axis=1) * s

    kscale = RET_HEAD_DIM ** -0.5
    cosk = cosv * kscale
    sink = sinv * kscale
    for h in range(RET_HEADS):
        sl = slice(h * RET_HEAD_DIM, (h + 1) * RET_HEAD_DIM)
        qr_ref[:, sl] = rot(q_ref[:, sl].astype(F32), cosv, sinv).astype(BF16)
        kr = rot(k_ref[:, sl].astype(F32), cosk, sink)
        kr_ref[:, sl] = kr.astype(BF16)
        kt = kr.T
        krt_ref[h] = kt.astype(BF16)
        rb_ref[h] = rb_state[h].astype(BF16)
        rb_state[h] = rb_state[h] * ret_dec_b[h] + _dot((kt * kb_ref[h:h + 1, :]).astype(BF16),
                                                        v_ref[:, sl])

    xs = _conv5_silu(xsp_ref[...], xs_ref[...], xsn_ref[...], cwx_ref, cbx_ref,
                     has_prev, has_next)
    bc = _conv5_silu(bcp_ref[...], bc_ref[...], bcn_ref[...], cwb_ref, cbb_ref,
                     has_prev, has_next)
    xbc_ref[:, :SSD_WIDTH] = xs.astype(BF16)
    xbc_ref[:, SSD_WIDTH:] = bc.astype(BF16)

    _, _, tri_t = _tri_masks()
    dt = _softplus(dt_ref[...] + dtbias_ref[...])
    rcs = _dot_exact_lhs(tri_t, _split3(dt * arow2_ref[...]))
    tot = rcs[0:1, :]
    w = jnp.exp2(tot - rcs) * dt
    wexp, cdec = _expand_rows(w, jnp.exp2(tot), expand_ref[...])
    xw = (xs * wexp).astype(BF16)
    ds = _state_increment(bc[:, :SSD_GROUPS * SSD_STATE], xw)
    sb_ref[...] = sb_state[...].astype(BF16)
    sb_state[...] = sb_state[...] * cdec + ds


def _prep_call(proj3, dt3, cs3, consts, ret_dec_b, cast_weights):
    b, l, _ = proj3.shape
    nc = l // CHUNK
    rows16 = l // HALO
    per16 = CHUNK // HALO

    def cix(i):
        return nc - 1 - i

    def col(cb):
        return lambda bi, i: (bi, cix(i), cb)

    def prev_halo(cb):
        return lambda bi, i: (bi, jnp.maximum(cix(i) * per16 - 1, 0), cb)

    def next_halo(cb):
        return lambda bi, i: (bi, jnp.minimum((cix(i) + 1) * per16, rows16 - 1), cb)

    def const(shape):
        return pl.BlockSpec(shape, lambda bi, i: (0,) * len(shape))

    in_specs = [
        pl.BlockSpec((None, CHUNK, LANES), col(0)),
        pl.BlockSpec((None, CHUNK, RET_WIDTH), col(COL_Q)),
        pl.BlockSpec((None, CHUNK, RET_WIDTH), col(COL_K)),
        pl.BlockSpec((None, CHUNK, RET_WIDTH), col(COL_V)),
        pl.BlockSpec((None, CHUNK, SSD_WIDTH), col(COL_XS)),
        pl.BlockSpec((None, HALO, SSD_WIDTH), prev_halo(COL_XS)),
        pl.BlockSpec((None, HALO, SSD_WIDTH), next_halo(COL_XS)),
        pl.BlockSpec((None, CHUNK, SSD_BC), col(COL_BC)),
        pl.BlockSpec((None, HALO, SSD_BC), prev_halo(COL_BC)),
        pl.BlockSpec((None, HALO, SSD_BC), next_halo(COL_BC)),
        pl.BlockSpec((None, CHUNK, LANES), col(0)),
        const((SSD_CONV, SSD_WIDTH)), const((1, SSD_WIDTH)),
        const((SSD_CONV, SSD_BC)), const((1, SSD_BC)),
        const((1, LANES)), const((1, LANES)),
        const((RET_HEADS, LANES)),
        const((LANES, SSD_WIDTH)),
    ]
    out_specs = [
        pl.BlockSpec((None, CHUNK, RET_WIDTH), col(0)),
        pl.BlockSpec((None, CHUNK, RET_WIDTH), col(0)),
        pl.BlockSpec((None, None, RET_HEADS, RET_HEAD_DIM, CHUNK),
                     lambda bi, i: (bi, cix(i), 0, 0, 0)),
        pl.BlockSpec((None, CHUNK, SSD_CONV_DIM), col(0)),
        pl.BlockSpec((None, None, RET_HEADS, RET_HEAD_DIM, RET_HEAD_DIM),
                     lambda bi, i: (bi, cix(i), 0, 0, 0)),
        pl.BlockSpec((None, None, SSD_STATE, SSD_WIDTH), lambda bi, i: (bi, cix(i), 0, 0)),
    ]
    out_shape = [
        jax.ShapeDtypeStruct((b, l, RET_WIDTH), BF16),
        jax.ShapeDtypeStruct((b, l, RET_WIDTH), BF16),
        jax.ShapeDtypeStruct((b, nc, RET_HEADS, RET_HEAD_DIM, CHUNK), BF16),
        jax.ShapeDtypeStruct((b, l, SSD_CONV_DIM), BF16),
        jax.ShapeDtypeStruct((b, nc, RET_HEADS, RET_HEAD_DIM, RET_HEAD_DIM), BF16),
        jax.ShapeDtypeStruct((b, nc, SSD_STATE, SSD_WIDTH), BF16),
    ]
    scratch = [
        pltpu.VMEM((RET_HEADS, RET_HEAD_DIM, RET_HEAD_DIM), F32),
        pltpu.VMEM((SSD_STATE, SSD_WIDTH), F32),
    ]
    cast_in, cast_out, cast_shapes, active, cast_srcs = _cast_slab_specs(cast_weights, b * nc, nc)
    outs = pl.pallas_call(
        functools.partial(_sweep_with_casts, functools.partial(_prep_kernel, ret_dec_b),
                          active, b * nc, len(in_specs), len(out_specs)),
        grid=(b, nc),
        in_specs=in_specs + cast_in,
        out_specs=out_specs + cast_out,
        out_shape=out_shape + cast_shapes,
        scratch_shapes=scratch,
        compiler_params=_cparams(2),
        name="reverse_sweep",
    )(cs3, proj3, proj3, proj3, proj3, proj3, proj3, proj3, proj3, proj3, dt3,
      consts["conv_w_xs"], consts["conv_b_xs"], consts["conv_w_bc"], consts["conv_b_bc"],
      consts["dt_bias"], consts["a_row2"], consts["ret_kb"], consts["expand_b"], *cast_srcs)
    return outs[:len(out_specs)], outs[len(out_specs):]


def _mix_kernel(ret_dec_f, side_work,
                qr_ref, kr_ref, krt_ref, v_ref, g_ref, z_ref, xbc_ref, dt_ref, rb_ref, sb_ref,
                mask_ref, qf_ref, qb_ref, kf_ref, retnw_ref,
                dtbias_ref, arow2_ref, expand_ref, expand2_ref, dexp_ref, ssdnw_ref,
                out_ref,
                rf_state, sf_state):
    i = pl.program_id(1)

    @pl.when(i == 0)
    def _():
        rf_state[...] = jnp.zeros_like(rf_state)
        sf_state[...] = jnp.zeros_like(sf_state)

    side_work()
    for h in range(RET_HEADS):
        sl = slice(h * RET_HEAD_DIM, (h + 1) * RET_HEAD_DIM)
        qh = qr_ref[:, sl]
        kh = kr_ref[:, sl]
        vh = v_ref[:, sl]
        s = (_dot_nt(qh, kh) * mask_ref[h]).astype(BF16)
        lhs = jnp.concatenate([s, qh * qf_ref[h], qh * qb_ref[h]], axis=1)
        rhs = jnp.concatenate([vh, rf_state[h].astype(BF16), rb_ref[h]], axis=0)
        o = _dot(lhs, rhs)
        rf_state[h] = rf_state[h] * ret_dec_f[h] + _dot(krt_ref[h], vh * kf_ref[h])
        o = _rms(o, retnw_ref[:, sl])
        out_ref[:, sl] = (g_ref[:, sl].astype(F32) * o).astype(BF16)

    lower, tri, tri_t = _tri_masks()
    dt = _softplus(dt_ref[...] + dtbias_ref[...])
    parts = _split3(dt * arow2_ref[...])
    lane = lax.broadcasted_iota(jnp.int32, (CHUNK, LANES), 1)
    prefix = _dot_exact_lhs(tri, parts)
    acs = jnp.where(lane < SSD_HEADS, prefix, _dot_exact_lhs(tri_t, parts))
    acs_t = acs.T
    src_t = (acs - jnp.log2(dt)).T
    edge = _dot(jnp.exp2(acs).astype(BF16), expand2_ref[...])
    xs16 = xbc_ref[:, :SSD_WIDTH]
    bm = xbc_ref[:, SSD_WIDTH:SSD_WIDTH + SSD_GROUPS * SSD_STATE]
    cm = xbc_ref[:, SSD_WIDTH + SSD_GROUPS * SSD_STATE:]
    lane_lo = lane < SSD_HEAD_DIM

    ys = []
    for g in range(SSD_GROUPS):
        gs = slice(g * SSD_STATE, (g + 1) * SSD_STATE)
        gw = slice(g * GROUP_W, (g + 1) * GROUP_W)
        gwb = slice(SSD_WIDTH + g * GROUP_W, SSD_WIDTH + (g + 1) * GROUP_W)
        cm_g = cm[:, gs]
        cb = _dot_nt(cm_g, bm[:, gs])
        y_off = (edge[:, gw] * _dot(cm_g, sf_state[:, gw].astype(BF16))
                 + edge[:, gwb] * _dot(cm_g, sb_ref[:, gw]))
        for pr in range(SSD_HEADS_PER_GROUP // 2):
            ps = slice(g * GROUP_W + pr * LANES, g * GROUP_W + (pr + 1) * LANES)
            pair = []
            for sub in range(2):
                e = g * SSD_HEADS_PER_GROUP + 2 * pr + sub
                eb = SSD_HEADS + e
                dst = jnp.where(lower, _col_bcast(acs_t[e:e + 1, :]), _col_bcast(acs_t[eb:eb + 1, :]))
                src = jnp.where(lower, src_t[e:e + 1, :], src_t[eb:eb + 1, :])
                m = (cb * jnp.exp2(dst - src)).astype(BF16)
                pair.append(_dot(m, xs16[:, ps]))
            ys.append(jnp.where(lane_lo, pair[0], pair[1]) + y_off[:, pr * LANES:(pr + 1) * LANES])
    y = jnp.concatenate(ys, axis=1)

    xs = xs16.astype(F32)
    y = (y + dexp_ref[...] * xs) * z_ref[...].astype(F32)
    for g in range(SSD_GROUPS):
        gs = slice(g * GROUP_W, (g + 1) * GROUP_W)
        out_ref[:, RET_WIDTH + g * GROUP_W:RET_WIDTH + (g + 1) * GROUP_W] = _rms(
            y[:, gs], ssdnw_ref[:, gs]).astype(BF16)

    tot = prefix[CHUNK - 1:CHUNK, :]
    w = jnp.exp2(tot - prefix) * dt
    wexp, cdec = _expand_rows(w, jnp.exp2(tot), expand_ref[...])
    xw = (xs * wexp).astype(BF16)
    ds = _state_increment(bm.astype(F32), xw)
    sf_state[...] = sf_state[...] * cdec + ds


def _mix_call(proj3, qr, kr, krt, xbc, dt3, rb, sb, consts, ret_dec_f, cast_weights):
    b, l, _ = proj3.shape
    nc = l // CHUNK

    def col(cb):
        return lambda bi, i: (bi, i, cb)

    def const(shape):
        return pl.BlockSpec(shape, lambda bi, i: (0,) * len(shape))

    in_specs = [
        pl.BlockSpec((None, CHUNK, RET_WIDTH), col(0)),
        pl.BlockSpec((None, CHUNK, RET_WIDTH), col(0)),
        pl.BlockSpec((None, None, RET_HEADS, RET_HEAD_DIM, CHUNK), lambda bi, i: (bi, i, 0, 0, 0)),
        pl.BlockSpec((None, CHUNK, RET_WIDTH), col(COL_V)),
        pl.BlockSpec((None, CHUNK, RET_WIDTH), col(COL_G)),
        pl.BlockSpec((None, CHUNK, SSD_WIDTH), col(COL_Z)),
        pl.BlockSpec((None, CHUNK, SSD_CONV_DIM), col(0)),
        pl.BlockSpec((None, CHUNK, LANES), col(0)),
        pl.BlockSpec((None, None, RET_HEADS, RET_HEAD_DIM, RET_HEAD_DIM),
                     lambda bi, i: (bi, i, 0, 0, 0)),
        pl.BlockSpec((None, None, SSD_STATE, SSD_WIDTH), lambda bi, i: (bi, i, 0, 0)),
        const((RET_HEADS, CHUNK, CHUNK)), const((RET_HEADS, CHUNK, LANES)),
        const((RET_HEADS, CHUNK, LANES)), const((RET_HEADS, CHUNK, LANES)), const((1, RET_WIDTH)),
        const((1, LANES)), const((1, LANES)), const((LANES, SSD_WIDTH)),
        const((LANES, 2 * SSD_WIDTH)), const((1, SSD_WIDTH)), const((1, SSD_WIDTH)),
    ]
    cast_in, cast_out, cast_shapes, active, cast_srcs = _cast_slab_specs(cast_weights, b * nc, nc)
    outs = pl.pallas_call(
        functools.partial(_sweep_with_casts, functools.partial(_mix_kernel, ret_dec_f),
                          active, b * nc, len(in_specs), 1),
        grid=(b, nc),
        in_specs=in_specs + cast_in,
        out_specs=[pl.BlockSpec((None, CHUNK, D_MODEL), col(0))] + cast_out,
        out_shape=[jax.ShapeDtypeStruct((b, l, D_MODEL), BF16)] + cast_shapes,
        scratch_shapes=[
            pltpu.VMEM((RET_HEADS, RET_HEAD_DIM, RET_HEAD_DIM), F32),
            pltpu.VMEM((SSD_STATE, SSD_WIDTH), F32),
        ],
        compiler_params=_cparams(2),
        name="forward_sweep",
    )(qr, kr, krt, proj3, proj3, proj3, xbc, dt3, rb, sb,
      consts["ret_mask"], consts["ret_qf"], consts["ret_qb"], consts["ret_kf"],
      consts["ret_norm_w"], consts["dt_bias"], consts["a_row2"], consts["expand_f"],
      consts["expand_fb"], consts["d_exp"], consts["ssd_norm_w"], *cast_srcs)
    return outs[0], outs[1:]


def _outproj_kernel(mix_ref, w_ref, x_ref, nw_ref, h_ref, hn_ref):
    h = x_ref[...] + _dot(mix_ref[...], w_ref[...])
    h_ref[...] = h
    hn_ref[...] = _rms(h, nw_ref[...]).astype(BF16)


def _out_projection(mix2, w_out, x2, ffn_norm_w, tm=512):
    t = x2.shape[0]
    return pl.pallas_call(
        _outproj_kernel,
        grid=(t // tm,),
        in_specs=[
            pl.BlockSpec((tm, D_MODEL), lambda i: (i, 0)),
            pl.BlockSpec((D_MODEL, D_MODEL), lambda i: (0, 0)),
            pl.BlockSpec((tm, D_MODEL), lambda i: (i, 0)),
            pl.BlockSpec((1, D_MODEL), lambda i: (0, 0)),
        ],
        out_specs=[
            pl.BlockSpec((tm, D_MODEL), lambda i: (i, 0)),
            pl.BlockSpec((tm, D_MODEL), lambda i: (i, 0)),
        ],
        out_shape=[
            jax.ShapeDtypeStruct((t, D_MODEL), F32),
            jax.ShapeDtypeStruct((t, D_MODEL), BF16),
        ],
        compiler_params=_cparams(1),
        name="out_projection",
    )(mix2, w_out, x2, ffn_norm_w)


def _ffn_kernel(tiles_per_seq, hn_ref, hp_ref, hx_ref, wgu_ref, cw_ref, cb_ref, wd_ref,
                o_ref, hbuf, gu_s):
    i = pl.program_id(0)
    j = pl.program_id(1)
    tm = hn_ref.shape[0]
    tf = cw_ref.shape[1]

    @pl.when(j == 0)
    def _():
        pos_in_seq = i % tiles_per_seq
        zero = jnp.zeros((HALO, D_MODEL), BF16)
        hbuf[0:HALO, :] = jnp.where(pos_in_seq > 0, hp_ref[...], zero)
        hbuf[HALO:HALO + tm, :] = hn_ref[...]
        hbuf[HALO + tm:, :] = jnp.where(pos_in_seq < tiles_per_seq - 1, hx_ref[...], zero)
        o_ref[...] = jnp.zeros_like(o_ref)

    gu_s[...] = _dot(hbuf[...], wgu_ref[...])
    pad = FFN_CONV // 2
    gate = cb_ref[...]
    for t in range(FFN_CONV):
        o = HALO + t - pad
        gate = gate + cw_ref[t:t + 1, :] * gu_s[o:o + tm, :tf]
    act = (_gelu_tanh(gate) * gu_s[HALO:HALO + tm, tf:]).astype(BF16)
    o_ref[...] += _dot(act, wd_ref[...])


def _ffn_call(hn2, wgu, conv_w, conv_b, wd, seq_len, tm=1024):
    t = hn2.shape[0]
    tf = wgu.shape[2] // 2
    per16 = tm // HALO
    rows16 = t // HALO
    return pl.pallas_call(
        functools.partial(_ffn_kernel, seq_len // tm),
        grid=(t // tm, D_FF // tf),
        in_specs=[
            pl.BlockSpec((tm, D_MODEL), lambda i, j: (i, 0)),
            pl.BlockSpec((HALO, D_MODEL), lambda i, j: (jnp.maximum(i * per16 - 1, 0), 0)),
            pl.BlockSpec((HALO, D_MODEL), lambda i, j: (jnp.minimum((i + 1) * per16, rows16 - 1), 0)),
            pl.BlockSpec((None, D_MODEL, 2 * tf), lambda i, j: (j, 0, 0)),
            pl.BlockSpec((FFN_CONV, tf), lambda i, j: (0, j)),
            pl.BlockSpec((1, tf), lambda i, j: (0, j)),
            pl.BlockSpec((tf, D_MODEL), lambda i, j: (j, 0)),
        ],
        out_specs=pl.BlockSpec((tm, D_MODEL), lambda i, j: (i, 0)),
        out_shape=jax.ShapeDtypeStruct((t, D_MODEL), F32),
        scratch_shapes=[
            pltpu.VMEM((tm + 2 * HALO, D_MODEL), BF16),
            pltpu.VMEM((tm + 2 * HALO, 2 * tf), F32),
        ],
        compiler_params=_cparams(2),
        name="conv_glu_ffn",
    )(hn2, hn2, hn2, wgu, conv_w, conv_b, wd)


def _ple_kernel(apply_final, h_ref, d_ref, p_ref, nw_ref, wg_ref, bg_ref, wp_ref, fw_ref, o_ref):
    h = h_ref[...] + d_ref[...]
    hn = _rms(h, nw_ref[...]).astype(BF16)
    gate = jax.nn.sigmoid(_dot(hn, wg_ref[...]) + bg_ref[...])
    h = h + gate * _dot(p_ref[...].astype(BF16), wp_ref[...])
    if apply_final:
        h = _rms(h, fw_ref[...])
    o_ref[...] = h


def _ple_call(h2, delta2, p2, norm_w, wg, bg, wp, final_w, apply_final, tm=512):
    t = h2.shape[0]
    return pl.pallas_call(
        functools.partial(_ple_kernel, apply_final),
        grid=(t // tm,),
        in_specs=[
            pl.BlockSpec((tm, D_MODEL), lambda i: (i, 0)),
            pl.BlockSpec((tm, D_MODEL), lambda i: (i, 0)),
            pl.BlockSpec((tm, D_PLE), lambda i: (i, 0)),
            pl.BlockSpec((1, D_MODEL), lambda i: (0, 0)),
            pl.BlockSpec((D_MODEL, D_MODEL), lambda i: (0, 0)),
            pl.BlockSpec((1, D_MODEL), lambda i: (0, 0)),
            pl.BlockSpec((D_PLE, D_MODEL), lambda i: (0, 0)),
            pl.BlockSpec((1, D_MODEL), lambda i: (0, 0)),
        ],
        out_specs=pl.BlockSpec((tm, D_MODEL), lambda i: (i, 0)),
        out_shape=jax.ShapeDtypeStruct((t, D_MODEL), F32),
        compiler_params=_cparams(1),
        name="ple_gate",
    )(h2, delta2, p2, norm_w, wg, bg, wp, final_w)


def _retention_tables():
    hh = np.arange(RET_HEADS, dtype=np.float64)
    lf = np.log1p(-np.exp2(-5.0 - hh))
    lb = np.log1p(-np.exp2(-5.5 - hh))
    idx = np.arange(CHUNK, dtype=np.float64)
    dist = idx[:, None] - idx[None, :]
    mask = np.where(dist >= 0, np.exp(lf[:, None, None] * np.abs(dist)),
                    np.exp(lb[:, None, None] * np.abs(dist)))
    ones = np.ones((1, 1, LANES))
    qf = np.exp(lf[:, None] * (idx + 1.0)[None, :])[:, :, None] * ones
    qb = np.exp(lb[:, None] * (CHUNK - idx)[None, :])[:, :, None] * ones
    kf = np.exp(lf[:, None] * (CHUNK - 1.0 - idx)[None, :])[:, :, None] * ones
    kb = np.exp(lb[:, None] * idx[None, :])
    dec_f = tuple(float(v) for v in np.exp(lf * CHUNK))
    dec_b = tuple(float(v) for v in np.exp(lb * CHUNK))
    f = lambda a: jnp.asarray(a, F32)
    h = lambda a: jnp.asarray(a, BF16)
    return dict(ret_mask=f(mask), ret_qf=h(qf), ret_qb=h(qb), ret_kf=h(kf), ret_kb=f(kb)), dec_f, dec_b


def _expand_matrix(first_row):
    e = np.zeros((LANES, SSD_WIDTH), np.float32)
    for h in range(SSD_HEADS):
        e[first_row + h, h * SSD_HEAD_DIM:(h + 1) * SSD_HEAD_DIM] = 1.0
    return e


def _pad_lanes(v):
    return jnp.pad(v.reshape(1, -1), ((0, 0), (0, LANES - v.size)))


def kernel(x, p, positions, norm_mix_w, w_in, ret_norm_w, ssd_conv_w, ssd_conv_b, ssd_dt_bias,
           ssd_a_log, ssd_d, ssd_norm_w, w_out, norm_ffn_w, ffn_w_gate, ffn_w_up, ffn_conv_w,
           ffn_conv_b, ffn_w_down, ple_norm_w, ple_w_gate, ple_b_gate, ple_w_proj, final_norm_w):
    b, l, _ = x.shape
    depth = w_in.shape[0]
    t = b * l
    nc = l // CHUNK
    row = lambda v: v.reshape(1, -1).astype(F32)

    tables, dec_f, dec_b = _retention_tables()
    exp_f, exp_b = _expand_matrix(0), _expand_matrix(SSD_HEADS)
    rot = dict(
        expand_f=jnp.asarray(exp_f, BF16),
        expand_b=jnp.asarray(exp_b, BF16),
        expand_fb=jnp.asarray(np.concatenate([exp_f, exp_b], axis=1), BF16),
    )
    h = x.reshape(t, D_MODEL)
    for i in range(depth):
        consts = dict(tables)
        consts.update(rot)
        consts.update(
            conv_w_xs=ssd_conv_w[i][:, :SSD_WIDTH], conv_b_xs=row(ssd_conv_b[i][:SSD_WIDTH]),
            conv_w_bc=ssd_conv_w[i][:, SSD_WIDTH:], conv_b_bc=row(ssd_conv_b[i][SSD_WIDTH:]),
            dt_bias=_pad_lanes(ssd_dt_bias[i]),
            a_row2=_pad_lanes(-jnp.exp(ssd_a_log[i].astype(F32)) * LOG2E),
            ret_norm_w=row(ret_norm_w[i]),
            d_exp=row(jnp.repeat(ssd_d[i], SSD_HEAD_DIM)),
            ssd_norm_w=row(ssd_norm_w[i]),
        )
        w_all = w_in[i].astype(BF16)
        w_dt = jnp.pad(w_all[:, N_MAIN:], ((0, 0), (0, LANES - N_DT)))

        (proj, dt, cs), (wgu16, wd16, wo16, wpg16) = _in_projection(
            h, row(norm_mix_w[i]), w_all, w_dt, positions,
            [(ffn_w_gate[i], ffn_w_up[i], FFN_TILE), ffn_w_down[i], w_out[i], ple_w_gate[i]])
        proj3 = proj.reshape(b, l, N_MAIN)
        dt3 = dt.reshape(b, l, LANES)
        cs3 = cs.reshape(b, l, LANES)
        (qr, kr, krt, xbc, rb, sb), _ = _prep_call(proj3, dt3, cs3, consts, dec_b, [])
        mix, _ = _mix_call(proj3, qr, kr, krt, xbc, dt3, rb, sb, consts, dec_f, [])
        h, hn = _out_projection(mix.reshape(t, D_MODEL), wo16, h, row(norm_ffn_w[i]))
        delta = _ffn_call(hn, wgu16, ffn_conv_w[i], row(ffn_conv_b[i]), wd16, l)
        h = _ple_call(h, delta, p[i].reshape(t, D_PLE), row(ple_norm_w[i]), wpg16,
                      row(ple_b_gate[i]), ple_w_proj[i].astype(BF16), row(final_norm_w),
                      apply_final=(i == depth - 1))
    return h.reshape(b, l, D_MODEL)
```

```python
import functools
import math

import numpy as np
import jax
import jax.numpy as jnp
from jax import lax
from jax.experimental import pallas as pl
from jax.experimental.pallas import tpu as pltpu

F32 = jnp.float32
BF16 = jnp.bfloat16

D_MODEL = 2048
EPS = 1e-6
D_PLE = 256
RET_WIDTH = D_MODEL // 2
RET_HEAD_DIM = 128
RET_HEADS = RET_WIDTH // RET_HEAD_DIM
ROPE_BASE = 10000.0
SSD_WIDTH = D_MODEL - RET_WIDTH
SSD_HEAD_DIM = 64
SSD_HEADS = SSD_WIDTH // SSD_HEAD_DIM
SSD_GROUPS = 2
SSD_HEADS_PER_GROUP = SSD_HEADS // SSD_GROUPS
SSD_STATE = 128
SSD_CONV = 5
SSD_BC = 2 * SSD_GROUPS * SSD_STATE
SSD_CONV_DIM = SSD_WIDTH + SSD_BC
D_FF = (11 * D_MODEL) // 4
FFN_CONV = 3
N_MAIN = 4 * RET_WIDTH + SSD_WIDTH + SSD_CONV_DIM
N_DT = 2 * SSD_HEADS

CHUNK = 128
LANES = 128
HALO = 16
GROUP_W = SSD_WIDTH // SSD_GROUPS
FFN_TILE = 512

COL_Q, COL_K, COL_V, COL_G, COL_Z, COL_XS = 0, 1, 2, 3, 4, 5
COL_BC = (5 * RET_WIDTH + SSD_WIDTH) // SSD_BC

VMEM_LIMIT = 56 * 1024 * 1024
LOG2E = math.log2(math.e)


def _cparams(n_axes):
    return pltpu.CompilerParams(dimension_semantics=("arbitrary",) * n_axes,
                                vmem_limit_bytes=VMEM_LIMIT)


def _rms(xf, w_row):
    ms = jnp.mean(xf * xf, axis=-1, keepdims=True)
    return xf * lax.rsqrt(ms + EPS) * w_row


def _silu(x):
    return x * jax.nn.sigmoid(x)


def _softplus(x):
    return jnp.maximum(x, 0.0) + jnp.log1p(jnp.exp(-jnp.abs(x)))


def _gelu_tanh(x):
    c = math.sqrt(2.0 / math.pi)
    return 0.5 * x * (1.0 + jnp.tanh(c * (x + 0.044715 * (x * x * x))))


def _dot(a, b):
    return jnp.dot(a, b, preferred_element_type=F32)


def _dot_nt(a, b):
    return lax.dot_general(a, b, (((1,), (1,)), ((), ())), preferred_element_type=F32)


def _split3(a):
    hi = a.astype(BF16)
    r1 = a - hi.astype(F32)
    mid = r1.astype(BF16)
    lo = (r1 - mid.astype(F32)).astype(BF16)
    return hi, mid, lo


def _dot_exact_lhs(m01, parts):
    hi, mid, lo = parts
    return _dot(m01, hi) + _dot(m01, mid) + _dot(m01, lo)


def _col_bcast(row):
    return jnp.broadcast_to(row, (LANES, LANES)).T


def _tri_masks():
    r = lax.broadcasted_iota(jnp.int32, (CHUNK, CHUNK), 0)
    c = lax.broadcasted_iota(jnp.int32, (CHUNK, CHUNK), 1)
    lower = r >= c
    tri = jnp.where(lower, 1.0, 0.0).astype(BF16)
    tri_t = jnp.where(r <= c, 1.0, 0.0).astype(BF16)
    return lower, tri, tri_t


def _staggered_row_specs(tm, n_tiles, n_steps, n_split):
    assert n_split < n_steps and tm % n_split == 0
    tq = tm // n_split

    def spec(q):
        def index(i, j):
            nxt = jnp.minimum(i + (j >= n_steps - n_split + q).astype(jnp.int32), n_tiles - 1)
            return (nxt * n_split + q, 0)
        return pl.BlockSpec((tq, D_MODEL), index)

    return [spec(q) for q in range(n_split)]


def _inproj_kernel(n_split, n_col_steps, side_work, *refs):
    x_parts = refs[:n_split]
    (nw_ref, w_ref, wdt_ref, pos_ref, freq_ref, phase_ref,
     proj_ref, dt_ref, cs_ref, hn_ref) = refs[n_split:]
    tq = x_parts[0].shape[0]

    @pl.when(pl.program_id(1) == 0)
    def _():
        for q, x_ref in enumerate(x_parts):
            hn_ref[q * tq:(q + 1) * tq, :] = _rms(x_ref[...], nw_ref[...]).astype(BF16)
        dt_ref[...] = _dot(hn_ref[...], wdt_ref[...])

    tn = w_ref.shape[1]
    gate_lo, gate_hi = COL_G * RET_WIDTH, COL_XS * RET_WIDTH
    for step in range(n_col_steps):
        lo = min(max(gate_lo - step * tn, 0), tn)
        hi = min(max(gate_hi - step * tn, 0), tn)

        @pl.when(pl.program_id(1) == step)
        def _():
            r = _dot(hn_ref[...], w_ref[...])
            if lo > 0:
                proj_ref[:, :lo] = r[:, :lo].astype(BF16)
            if hi > lo:
                proj_ref[:, lo:hi] = _silu(r[:, lo:hi]).astype(BF16)
            if hi < tn:
                proj_ref[:, hi:] = r[:, hi:].astype(BF16)
            side_work()
            if step == n_col_steps - 1:
                ang_t = freq_ref[...] * pos_ref[...].astype(F32) - phase_ref[...]
                cs_ref[...] = jnp.cos(ang_t).T


def _in_projection(x2, norm_w, w_all, w_dt, positions, cast_weights,
                   tm=1024, tn=N_MAIN // 4, n_split=2):
    t = x2.shape[0]
    n_rows, n_cols = t // tm, N_MAIN // tn
    half = RET_HEAD_DIM // 2
    inv_freq = ROPE_BASE ** (-jnp.arange(half, dtype=F32) / half)
    freq = jnp.broadcast_to(jnp.concatenate([inv_freq, inv_freq])[:, None], (LANES, tm))
    phase = jnp.broadcast_to(
        jnp.concatenate([jnp.zeros((half,), F32), jnp.full((half,), math.pi / 2, F32)])[:, None],
        (LANES, tm))
    in_specs = _staggered_row_specs(tm, n_rows, n_cols, n_split) + [
        pl.BlockSpec((1, D_MODEL), lambda i, j: (0, 0)),
        pl.BlockSpec((D_MODEL, tn), lambda i, j: (0, j)),
        pl.BlockSpec((D_MODEL, LANES), lambda i, j: (0, 0)),
        pl.BlockSpec((None, 1, tm), lambda i, j: (i, 0, 0)),
        pl.BlockSpec((LANES, tm), lambda i, j: (0, 0)),
        pl.BlockSpec((LANES, tm), lambda i, j: (0, 0)),
    ]
    out_specs = [
        pl.BlockSpec((tm, tn), lambda i, j: (i, j)),
        pl.BlockSpec((tm, LANES), lambda i, j: (i, 0)),
        pl.BlockSpec((tm, LANES), lambda i, j: (i, 0)),
    ]
    out_shape = [
        jax.ShapeDtypeStruct((t, N_MAIN), BF16),
        jax.ShapeDtypeStruct((t, LANES), F32),
        jax.ShapeDtypeStruct((t, LANES), F32),
    ]
    cast_in, cast_out, cast_shapes, jobs, cast_srcs = _cast_slab_specs(
        cast_weights, n_rows * n_cols, n_cols)
    outs = pl.pallas_call(
        functools.partial(_sweep_with_casts, functools.partial(_inproj_kernel, n_split, n_cols),
                          jobs, n_rows * n_cols, len(in_specs), len(out_specs)),
        grid=(n_rows, n_cols),
        in_specs=in_specs + cast_in,
        out_specs=out_specs + cast_out,
        out_shape=out_shape + cast_shapes,
        scratch_shapes=[pltpu.VMEM((tm, D_MODEL), BF16)],
        compiler_params=_cparams(2),
        name="in_projection",
    )(*([x2] * n_split), norm_w, w_all, w_dt, positions.reshape(n_rows, 1, tm), freq, phase,
      *cast_srcs)
    return outs[:3], outs[3:]


def _conv5_silu(prev, cur, nxt, w_ref, b_ref, has_prev, has_next):
    zero = jnp.zeros_like(prev)
    depth = 2 * LANES
    ext = jnp.concatenate(
        [jnp.where(has_prev, prev, zero), cur, jnp.where(has_next, nxt, zero),
         jnp.zeros((depth - CHUNK - 2 * HALO, cur.shape[1]), cur.dtype)], axis=0)
    r = lax.broadcasted_iota(jnp.int32, (CHUNK, depth), 0)
    c = lax.broadcasted_iota(jnp.int32, (CHUNK, depth), 1)
    pad = SSD_CONV // 2
    acc = b_ref[...] + w_ref[pad:pad + 1, :] * cur.astype(F32)
    for j in range(SSD_CONV):
        if j != pad:
            shift = jnp.where(c == r + (HALO + j - pad), 1.0, 0.0).astype(BF16)
            acc = acc + w_ref[j:j + 1, :] * _dot(shift, ext)
    return _silu(acc)


def _expand_rows(w, tot_row, expand01):
    stacked = jnp.concatenate([w, jnp.broadcast_to(tot_row, (HALO, LANES))], axis=0)
    e = _dot(stacked.astype(BF16), expand01)
    return e[:CHUNK], e[CHUNK:CHUNK + 1]


def _state_increment(bm_f32, xw):
    parts = []
    for g in range(SSD_GROUPS):
        bm_t = bm_f32[:, g * SSD_STATE:(g + 1) * SSD_STATE].T.astype(BF16)
        parts.append(_dot(bm_t, xw[:, g * GROUP_W:(g + 1) * GROUP_W]))
    return jnp.concatenate(parts, axis=1)


def _cast_slab_specs(weights, n_steps, nc):
    in_specs, out_specs, out_shapes, jobs, flat = [], [], [], [], []
    for entry in weights:
        srcs = entry[:2] if isinstance(entry, tuple) else (entry,)
        tile = entry[2] if isinstance(entry, tuple) else 0
        n_rows, n_cols = srcs[0].shape
        rows = next(r for r in range(HALO, n_rows + 1, HALO)
                    if n_rows % r == 0 and n_rows // r <= n_steps)
        n_active = n_rows // rows

        def index(bi, i, _last=n_active - 1):
            return (jnp.minimum(bi * nc + i, _last), 0)

        for w in srcs:
            in_specs.append(pl.BlockSpec((rows, n_cols), index))
            flat.append(w)
        if tile:
            n_tiles = n_cols // tile
            out_specs.append(pl.BlockSpec((n_tiles, rows, 2 * tile),
                                          lambda bi, i, _index=index: (0,) + _index(bi, i)))
            out_shapes.append(jax.ShapeDtypeStruct((n_tiles, n_rows, 2 * tile), BF16))
        else:
            out_specs.append(pl.BlockSpec((rows, n_cols), index))
            out_shapes.append(jax.ShapeDtypeStruct((n_rows, n_cols), BF16))
        jobs.append((n_active, len(srcs), tile))
    return in_specs, out_specs, out_shapes, tuple(jobs), flat


def _sweep_with_casts(body, jobs, n_steps, n_in, n_out, *refs):
    n_src = sum(job[1] for job in jobs)
    ins, refs = refs[:n_in], refs[n_in:]
    srcs, refs = refs[:n_src], refs[n_src:]
    outs, refs = refs[:n_out], refs[n_out:]
    dsts, scratch = refs[:len(jobs)], refs[len(jobs):]
    step = pl.program_id(0) * pl.num_programs(1) + pl.program_id(1)

    def narrow(job_srcs, dst, tile):
        if tile:
            for t in range(dst.shape[0]):
                for k, src in enumerate(job_srcs):
                    dst[t, :, k * tile:(k + 1) * tile] = src[:, t * tile:(t + 1) * tile].astype(BF16)
        else:
            dst[...] = job_srcs[0][...].astype(BF16)

    every_step = []
    for (n_active, n_job_src, tile), dst in zip(jobs, dsts):
        job_srcs, srcs = srcs[:n_job_src], srcs[n_job_src:]
        if n_active == n_steps:
            every_step.append(functools.partial(narrow, job_srcs, dst, tile))
        else:
            pl.when(step < n_active)(functools.partial(narrow, job_srcs, dst, tile))

    def side_work():
        for f in every_step:
            f()

    body(side_work, *ins, *outs, *scratch)


def _prep_kernel(ret_dec_b, side_work,
                 cs_ref, q_ref, k_ref, v_ref,
                 xs_ref, xsp_ref, xsn_ref, bc_ref, bcp_ref, bcn_ref, dt_ref,
                 cwx_ref, cbx_ref, cwb_ref, cbb_ref, dtbias_ref, arow2_ref,
                 kb_ref, expand_ref,
                 qr_ref, kr_ref, krt_ref, xbc_ref, rb_ref, sb_ref,
                 rb_state, sb_state):
    i = pl.program_id(1)
    nc = pl.num_programs(1)

    @pl.when(i == 0)
    def _():
        rb_state[...] = jnp.zeros_like(rb_state)
        sb_state[...] = jnp.zeros_like(sb_state)

    side_work()
    has_next = i > 0
    has_prev = i < nc - 1

    half = RET_HEAD_DIM // 2
    cs = cs_ref[...]
    sc = pltpu.roll(cs, half, axis=1)
    lane_lo = lax.broadcasted_iota(jnp.int32, (CHUNK, LANES), 1) < half
    cosv = jnp.where(lane_lo, cs, sc)
    sinv = jnp.where(lane_lo, -sc, cs)

    def rot(t, c, s):
        return t * c + pltpu.roll(t, half, axis=1) * s

    kscale = RET_HEAD_DIM ** -0.5
    cosk = cosv * kscale
    sink = sinv * kscale
    for h in range(RET_HEADS):
        sl = slice(h * RET_HEAD_DIM, (h + 1) * RET_HEAD_DIM)
        qr_ref[:, sl] = rot(q_ref[:, sl].astype(F32), cosv, sinv).astype(BF16)
        kr = rot(k_ref[:, sl].astype(F32), cosk, sink)
        kr_ref[:, sl] = kr.astype(BF16)
        kt = kr.T
        krt_ref[h] = kt.astype(BF16)
        rb_ref[h] = rb_state[h].astype(BF16)
        rb_state[h] = rb_state[h] * ret_dec_b[h] + _dot((kt * kb_ref[h:h + 1, :]).astype(BF16),
                                                        v_ref[:, sl])

    xs = _conv5_silu(xsp_ref[...], xs_ref[...], xsn_ref[...], cwx_ref, cbx_ref,
                     has_prev, has_next)
    bc = _conv5_silu(bcp_ref[...], bc_ref[...], bcn_ref[...], cwb_ref, cbb_ref,
                     has_prev, has_next)
    xbc_ref[:, :SSD_WIDTH] = xs.astype(BF16)
    xbc_ref[:, SSD_WIDTH:] = bc.astype(BF16)

    _, _, tri_t = _tri_masks()
    dt = _softplus(dt_ref[...] + dtbias_ref[...])
    rcs = _dot_exact_lhs(tri_t, _split3(dt * arow2_ref[...]))
    tot = rcs[0:1, :]
    w = jnp.exp2(tot - rcs) * dt
    wexp, cdec = _expand_rows(w, jnp.exp2(tot), expand_ref[...])
    xw = (xs * wexp).astype(BF16)
    ds = _state_increment(bc[:, :SSD_GROUPS * SSD_STATE], xw)
    sb_ref[...] = sb_state[...].astype(BF16)
    sb_state[...] = sb_state[...] * cdec + ds


def _prep_call(proj3, dt3, cs3, consts, ret_dec_b, cast_weights):
    b, l, _ = proj3.shape
    nc = l // CHUNK
    rows16 = l // HALO
    per16 = CHUNK // HALO

    def cix(i):
        return nc - 1 - i

    def col(cb):
        return lambda bi, i: (bi, cix(i), cb)

    def prev_halo(cb):
        return lambda bi, i: (bi, jnp.maximum(cix(i) * per16 - 1, 0), cb)

    def next_halo(cb):
        return lambda bi, i: (bi, jnp.minimum((cix(i) + 1) * per16, rows16 - 1), cb)

    def const(shape):
        return pl.BlockSpec(shape, lambda bi, i: (0,) * len(shape))

    in_specs = [
        pl.BlockSpec((None, CHUNK, LANES), col(0)),
        pl.BlockSpec((None, CHUNK, RET_WIDTH), col(COL_Q)),
        pl.BlockSpec((None, CHUNK, RET_WIDTH), col(COL_K)),
        pl.BlockSpec((None, CHUNK, RET_WIDTH), col(COL_V)),
        pl.BlockSpec((None, CHUNK, SSD_WIDTH), col(COL_XS)),
        pl.BlockSpec((None, HALO, SSD_WIDTH), prev_halo(COL_XS)),
        pl.BlockSpec((None, HALO, SSD_WIDTH), next_halo(COL_XS)),
        pl.BlockSpec((None, CHUNK, SSD_BC), col(COL_BC)),
        pl.BlockSpec((None, HALO, SSD_BC), prev_halo(COL_BC)),
        pl.BlockSpec((None, HALO, SSD_BC), next_halo(COL_BC)),
        pl.BlockSpec((None, CHUNK, LANES), col(0)),
        const((SSD_CONV, SSD_WIDTH)), const((1, SSD_WIDTH)),
        const((SSD_CONV, SSD_BC)), const((1, SSD_BC)),
        const((1, LANES)), const((1, LANES)),
        const((RET_HEADS, LANES)),
        const((LANES, SSD_WIDTH)),
    ]
    out_specs = [
        pl.BlockSpec((None, CHUNK, RET_WIDTH), col(0)),
        pl.BlockSpec((None, CHUNK, RET_WIDTH), col(0)),
        pl.BlockSpec((None, None, RET_HEADS, RET_HEAD_DIM, CHUNK),
                     lambda bi, i: (bi, cix(i), 0, 0, 0)),
        pl.BlockSpec((None, CHUNK, SSD_CONV_DIM), col(0)),
        pl.BlockSpec((None, None, RET_HEADS, RET_HEAD_DIM, RET_HEAD_DIM),
                     lambda bi, i: (bi, cix(i), 0, 0, 0)),
        pl.BlockSpec((None, None, SSD_STATE, SSD_WIDTH), lambda bi, i: (bi, cix(i), 0, 0)),
    ]
    out_shape = [
        jax.ShapeDtypeStruct((b, l, RET_WIDTH), BF16),
        jax.ShapeDtypeStruct((b, l, RET_WIDTH), BF16),
        jax.ShapeDtypeStruct((b, nc, RET_HEADS, RET_HEAD_DIM, CHUNK), BF16),
        jax.ShapeDtypeStruct((b, l, SSD_CONV_DIM), BF16),
        jax.ShapeDtypeStruct((b, nc, RET_HEADS, RET_HEAD_DIM, RET_HEAD_DIM), BF16),
        jax.ShapeDtypeStruct((b, nc, SSD_STATE, SSD_WIDTH), BF16),
    ]
    scratch = [
        pltpu.VMEM((RET_HEADS, RET_HEAD_DIM, RET_HEAD_DIM), F32),
        pltpu.VMEM((SSD_STATE, SSD_WIDTH), F32),
    ]
    cast_in, cast_out, cast_shapes, active, cast_srcs = _cast_slab_specs(cast_weights, b * nc, nc)
    outs = pl.pallas_call(
        functools.partial(_sweep_with_casts, functools.partial(_prep_kernel, ret_dec_b),
                          active, b * nc, len(in_specs), len(out_specs)),
        grid=(b, nc),
        in_specs=in_specs + cast_in,
        out_specs=out_specs + cast_out,
        out_shape=out_shape + cast_shapes,
        scratch_shapes=scratch,
        compiler_params=_cparams(2),
        name="reverse_sweep",
    )(cs3, proj3, proj3, proj3, proj3, proj3, proj3, proj3, proj3, proj3, dt3,
      consts["conv_w_xs"], consts["conv_b_xs"], consts["conv_w_bc"], consts["conv_b_bc"],
      consts["dt_bias"], consts["a_row2"], consts["ret_kb"], consts["expand_b"], *cast_srcs)
    return outs[:len(out_specs)], outs[len(out_specs):]


def _mix_kernel(ret_dec_f, side_work,
                qr_ref, kr_ref, krt_ref, v_ref, g_ref, z_ref, xbc_ref, dt_ref, rb_ref, sb_ref,
                mask_ref, qf_ref, qb_ref, kf_ref, retnw_ref,
                dtbias_ref, arow2_ref, expand_ref, expand2_ref, dexp_ref, ssdnw_ref,
                out_ref,
                rf_state, sf_state):
    i = pl.program_id(1)

    @pl.when(i == 0)
    def _():
        rf_state[...] = jnp.zeros_like(rf_state)
        sf_state[...] = jnp.zeros_like(sf_state)

    side_work()
    for h in range(RET_HEADS):
        sl = slice(h * RET_HEAD_DIM, (h + 1) * RET_HEAD_DIM)
        qh = qr_ref[:, sl]
        kh = kr_ref[:, sl]
        vh = v_ref[:, sl]
        s = (_dot_nt(qh, kh) * mask_ref[h]).astype(BF16)
        lhs = jnp.concatenate([s, qh * qf_ref[h], qh * qb_ref[h]], axis=1)
        rhs = jnp.concatenate([vh, rf_state[h].astype(BF16), rb_ref[h]], axis=0)
        o = _dot(lhs, rhs)
        rf_state[h] = rf_state[h] * ret_dec_f[h] + _dot(krt_ref[h], vh * kf_ref[h])
        o = _rms(o, retnw_ref[:, sl])
        out_ref[:, sl] = (g_ref[:, sl].astype(F32) * o).astype(BF16)

    lower, tri, tri_t = _tri_masks()
    dt = _softplus(dt_ref[...] + dtbias_ref[...])
    parts = _split3(dt * arow2_ref[...])
    lane = lax.broadcasted_iota(jnp.int32, (CHUNK, LANES), 1)
    prefix = _dot_exact_lhs(tri, parts)
    acs = jnp.where(lane < SSD_HEADS, prefix, _dot_exact_lhs(tri_t, parts))
    acs_t = acs.T
    src_t = (acs - jnp.log2(dt)).T
    edge = _dot(jnp.exp2(acs).astype(BF16), expand2_ref[...])
    xs16 = xbc_ref[:, :SSD_WIDTH]
    bm = xbc_ref[:, SSD_WIDTH:SSD_WIDTH + SSD_GROUPS * SSD_STATE]
    cm = xbc_ref[:, SSD_WIDTH + SSD_GROUPS * SSD_STATE:]
    lane_lo = lane < SSD_HEAD_DIM

    ys = []
    for g in range(SSD_GROUPS):
        gs = slice(g * SSD_STATE, (g + 1) * SSD_STATE)
        gw = slice(g * GROUP_W, (g + 1) * GROUP_W)
        gwb = slice(SSD_WIDTH + g * GROUP_W, SSD_WIDTH + (g + 1) * GROUP_W)
        cm_g = cm[:, gs]
        cb = _dot_nt(cm_g, bm[:, gs])
        y_off = (edge[:, gw] * _dot(cm_g, sf_state[:, gw].astype(BF16))
                 + edge[:, gwb] * _dot(cm_g, sb_ref[:, gw]))
        for pr in range(SSD_HEADS_PER_GROUP // 2):
            ps = slice(g * GROUP_W + pr * LANES, g * GROUP_W + (pr + 1) * LANES)
            pair = []
            for sub in range(2):
                e = g * SSD_HEADS_PER_GROUP + 2 * pr + sub
                eb = SSD_HEADS + e
                dst = jnp.where(lower, _col_bcast(acs_t[e:e + 1, :]), _col_bcast(acs_t[eb:eb + 1, :]))
                src = jnp.where(lower, src_t[e:e + 1, :], src_t[eb:eb + 1, :])
                m = (cb * jnp.exp2(dst - src)).astype(BF16)
                pair.append(_dot(m, xs16[:, ps]))
            ys.append(jnp.where(lane_lo, pair[0], pair[1]) + y_off[:, pr * LANES:(pr + 1) * LANES])
    y = jnp.concatenate(ys, axis=1)

    xs = xs16.astype(F32)
    y = (y + dexp_ref[...] * xs) * z_ref[...].astype(F32)
    for g in range(SSD_GROUPS):
        gs = slice(g * GROUP_W, (g + 1) * GROUP_W)
        out_ref[:, RET_WIDTH + g * GROUP_W:RET_WIDTH + (g + 1) * GROUP_W] = _rms(
            y[:, gs], ssdnw_ref[:, gs]).astype(BF16)

    tot = prefix[CHUNK - 1:CHUNK, :]
    w = jnp.exp2(tot - prefix) * dt
    wexp, cdec = _expand_rows(w, jnp.exp2(tot), expand_ref[...])
    xw = (xs * wexp).astype(BF16)
    ds = _state_increment(bm.astype(F32), xw)
    sf_state[...] = sf_state[...] * cdec + ds


def _mix_call(proj3, qr, kr, krt, xbc, dt3, rb, sb, consts, ret_dec_f, cast_weights):
    b, l, _ = proj3.shape
    nc = l // CHUNK

    def col(cb):
        return lambda bi, i: (bi, i, cb)

    def const(shape):
        return pl.BlockSpec(shape, lambda bi, i: (0,) * len(shape))

    in_specs = [
        pl.BlockSpec((None, CHUNK, RET_WIDTH), col(0)),
        pl.BlockSpec((None, CHUNK, RET_WIDTH), col(0)),
        pl.BlockSpec((None, None, RET_HEADS, RET_HEAD_DIM, CHUNK), lambda bi, i: (bi, i, 0, 0, 0)),
        pl.BlockSpec((None, CHUNK, RET_WIDTH), col(COL_V)),
        pl.BlockSpec((None, CHUNK, RET_WIDTH), col(COL_G)),
        pl.BlockSpec((None, CHUNK, SSD_WIDTH), col(COL_Z)),
        pl.BlockSpec((None, CHUNK, SSD_CONV_DIM), col(0)),
        pl.BlockSpec((None, CHUNK, LANES), col(0)),
        pl.BlockSpec((None, None, RET_HEADS, RET_HEAD_DIM, RET_HEAD_DIM),
                     lambda bi, i: (bi, i, 0, 0, 0)),
        pl.BlockSpec((None, None, SSD_STATE, SSD_WIDTH), lambda bi, i: (bi, i, 0, 0)),
        const((RET_HEADS, CHUNK, CHUNK)), const((RET_HEADS, CHUNK, LANES)),
        const((RET_HEADS, CHUNK, LANES)), const((RET_HEADS, CHUNK, LANES)), const((1, RET_WIDTH)),
        const((1, LANES)), const((1, LANES)), const((LANES, SSD_WIDTH)),
        const((LANES, 2 * SSD_WIDTH)), const((1, SSD_WIDTH)), const((1, SSD_WIDTH)),
    ]
    cast_in, cast_out, cast_shapes, active, cast_srcs = _cast_slab_specs(cast_weights, b * nc, nc)
    outs = pl.pallas_call(
        functools.partial(_sweep_with_casts, functools.partial(_mix_kernel, ret_dec_f),
                          active, b * nc, len(in_specs), 1),
        grid=(b, nc),
        in_specs=in_specs + cast_in,
        out_specs=[pl.BlockSpec((None, CHUNK, D_MODEL), col(0))] + cast_out,
        out_shape=[jax.ShapeDtypeStruct((b, l, D_MODEL), BF16)] + cast_shapes,
        scratch_shapes=[
            pltpu.VMEM((RET_HEADS, RET_HEAD_DIM, RET_HEAD_DIM), F32),
            pltpu.VMEM((SSD_STATE, SSD_WIDTH), F32),
        ],
        compiler_params=_cparams(2),
        name="forward_sweep",
    )(qr, kr, krt, proj3, proj3, proj3, xbc, dt3, rb, sb,
      consts["ret_mask"], consts["ret_qf"], consts["ret_qb"], consts["ret_kf"],
      consts["ret_norm_w"], consts["dt_bias"], consts["a_row2"], consts["expand_f"],
      consts["expand_fb"], consts["d_exp"], consts["ssd_norm_w"], *cast_srcs)
    return outs[0], outs[1:]


def _outproj_kernel(mix_ref, w_ref, x_ref, nw_ref, h_ref, hn_ref):
    h = x_ref[...] + _dot(mix_ref[...], w_ref[...])
    h_ref[...] = h
    hn_ref[...] = _rms(h, nw_ref[...]).astype(BF16)


def _out_projection(mix2, w_out, x2, ffn_norm_w, tm=512):
    t = x2.shape[0]
    return pl.pallas_call(
        _outproj_kernel,
        grid=(t // tm,),
        in_specs=[
            pl.BlockSpec((tm, D_MODEL), lambda i: (i, 0)),
            pl.BlockSpec((D_MODEL, D_MODEL), lambda i: (0, 0)),
            pl.BlockSpec((tm, D_MODEL), lambda i: (i, 0)),
            pl.BlockSpec((1, D_MODEL), lambda i: (0, 0)),
        ],
        out_specs=[
            pl.BlockSpec((tm, D_MODEL), lambda i: (i, 0)),
            pl.BlockSpec((tm, D_MODEL), lambda i: (i, 0)),
        ],
        out_shape=[
            jax.ShapeDtypeStruct((t, D_MODEL), F32),
            jax.ShapeDtypeStruct((t, D_MODEL), BF16),
        ],
        compiler_params=_cparams(1),
        name="out_projection",
    )(mix2, w_out, x2, ffn_norm_w)


def _ffn_kernel(tiles_per_seq, hn_ref, hp_ref, hx_ref, wgu_ref, cw_ref, cb_ref, wd_ref,
                o_ref, hbuf, gu_s):
    i = pl.program_id(0)
    j = pl.program_id(1)
    tm = hn_ref.shape[0]
    tf = cw_ref.shape[1]

    def column_step(first):
        if first:
            pos_in_seq = i % tiles_per_seq
            zero = jnp.zeros((HALO, D_MODEL), BF16)
            hbuf[0:HALO, :] = jnp.where(pos_in_seq > 0, hp_ref[...], zero)
            hbuf[HALO:HALO + tm, :] = hn_ref[...]
            hbuf[HALO + tm:, :] = jnp.where(pos_in_seq < tiles_per_seq - 1, hx_ref[...], zero)
        gu_s[...] = _dot(hbuf[...], wgu_ref[...])
        pad = FFN_CONV // 2
        gate = cb_ref[...]
        for t in range(FFN_CONV):
            o = HALO + t - pad
            gate = gate + cw_ref[t:t + 1, :] * gu_s[o:o + tm, :tf]
        act = (_gelu_tanh(gate) * gu_s[HALO:HALO + tm, tf:]).astype(BF16)
        if first:
            o_ref[...] = _dot(act, wd_ref[...])
        else:
            o_ref[...] += _dot(act, wd_ref[...])

    pl.when(j == 0)(functools.partial(column_step, True))
    pl.when(j > 0)(functools.partial(column_step, False))


def _ffn_call(hn2, wgu, conv_w, conv_b, wd, seq_len, tm=1024):
    t = hn2.shape[0]
    tf = wgu.shape[2] // 2
    per16 = tm // HALO
    rows16 = t // HALO
    return pl.pallas_call(
        functools.partial(_ffn_kernel, seq_len // tm),
        grid=(t // tm, D_FF // tf),
        in_specs=[
            pl.BlockSpec((tm, D_MODEL), lambda i, j: (i, 0)),
            pl.BlockSpec((HALO, D_MODEL), lambda i, j: (jnp.maximum(i * per16 - 1, 0), 0)),
            pl.BlockSpec((HALO, D_MODEL), lambda i, j: (jnp.minimum((i + 1) * per16, rows16 - 1), 0)),
            pl.BlockSpec((None, D_MODEL, 2 * tf), lambda i, j: (j, 0, 0)),
            pl.BlockSpec((FFN_CONV, tf), lambda i, j: (0, j)),
            pl.BlockSpec((1, tf), lambda i, j: (0, j)),
            pl.BlockSpec((tf, D_MODEL), lambda i, j: (j, 0)),
        ],
        out_specs=pl.BlockSpec((tm, D_MODEL), lambda i, j: (i, 0)),
        out_shape=jax.ShapeDtypeStruct((t, D_MODEL), F32),
        scratch_shapes=[
            pltpu.VMEM((tm + 2 * HALO, D_MODEL), BF16),
            pltpu.VMEM((tm + 2 * HALO, 2 * tf), F32),
        ],
        compiler_params=_cparams(2),
        name="conv_glu_ffn",
    )(hn2, hn2, hn2, wgu, conv_w, conv_b, wd)


def _ple_kernel(apply_final, h_ref, d_ref, p_ref, nw_ref, wg_ref, bg_ref, wp_ref, fw_ref, o_ref):
    h = h_ref[...] + d_ref[...]
    hn = _rms(h, nw_ref[...]).astype(BF16)
    gate = jax.nn.sigmoid(_dot(hn, wg_ref[...]) + bg_ref[...])
    h = h + gate * _dot(p_ref[...].astype(BF16), wp_ref[...])
    if apply_final:
        h = _rms(h, fw_ref[...])
    o_ref[...] = h


def _ple_call(h2, delta2, p2, norm_w, wg, bg, wp, final_w, apply_final, tm=512):
    t = h2.shape[0]
    return pl.pallas_call(
        functools.partial(_ple_kernel, apply_final),
        grid=(t // tm,),
        in_specs=[
            pl.BlockSpec((tm, D_MODEL), lambda i: (i, 0)),
            pl.BlockSpec((tm, D_MODEL), lambda i: (i, 0)),
            pl.BlockSpec((tm, D_PLE), lambda i: (i, 0)),
            pl.BlockSpec((1, D_MODEL), lambda i: (0, 0)),
            pl.BlockSpec((D_MODEL, D_MODEL), lambda i: (0, 0)),
            pl.BlockSpec((1, D_MODEL), lambda i: (0, 0)),
            pl.BlockSpec((D_PLE, D_MODEL), lambda i: (0, 0)),
            pl.BlockSpec((1, D_MODEL), lambda i: (0, 0)),
        ],
        out_specs=pl.BlockSpec((tm, D_MODEL), lambda i: (i, 0)),
        out_shape=jax.ShapeDtypeStruct((t, D_MODEL), F32),
        compiler_params=_cparams(1),
        name="ple_gate",
    )(h2, delta2, p2, norm_w, wg, bg, wp, final_w)


def _retention_tables():
    hh = np.arange(RET_HEADS, dtype=np.float64)
    lf = np.log1p(-np.exp2(-5.0 - hh))
    lb = np.log1p(-np.exp2(-5.5 - hh))
    idx = np.arange(CHUNK, dtype=np.float64)
    dist = idx[:, None] - idx[None, :]
    mask = np.where(dist >= 0, np.exp(lf[:, None, None] * np.abs(dist)),
                    np.exp(lb[:, None, None] * np.abs(dist)))
    ones = np.ones((1, 1, LANES))
    qf = np.exp(lf[:, None] * (idx + 1.0)[None, :])[:, :, None] * ones
    qb = np.exp(lb[:, None] * (CHUNK - idx)[None, :])[:, :, None] * ones
    kf = np.exp(lf[:, None] * (CHUNK - 1.0 - idx)[None, :])[:, :, None] * ones
    kb = np.exp(lb[:, None] * idx[None, :])
    dec_f = tuple(float(v) for v in np.exp(lf * CHUNK))
    dec_b = tuple(float(v) for v in np.exp(lb * CHUNK))
    f = lambda a: jnp.asarray(a, F32)
    h = lambda a: jnp.asarray(a, BF16)
    return dict(ret_mask=f(mask), ret_qf=h(qf), ret_qb=h(qb), ret_kf=h(kf), ret_kb=f(kb)), dec_f, dec_b


def _expand_matrix(first_row):
    e = np.zeros((LANES, SSD_WIDTH), np.float32)
    for h in range(SSD_HEADS):
        e[first_row + h, h * SSD_HEAD_DIM:(h + 1) * SSD_HEAD_DIM] = 1.0
    return e


def _pad_lanes(v):
    return jnp.pad(v.reshape(1, -1), ((0, 0), (0, LANES - v.size)))


def kernel(x, p, positions, norm_mix_w, w_in, ret_norm_w, ssd_conv_w, ssd_conv_b, ssd_dt_bias,
           ssd_a_log, ssd_d, ssd_norm_w, w_out, norm_ffn_w, ffn_w_gate, ffn_w_up, ffn_conv_w,
           ffn_conv_b, ffn_w_down, ple_norm_w, ple_w_gate, ple_b_gate, ple_w_proj, final_norm_w):
    b, l, _ = x.shape
    depth = w_in.shape[0]
    t = b * l
    nc = l // CHUNK
    row = lambda v: v.reshape(1, -1).astype(F32)

    tables, dec_f, dec_b = _retention_tables()
    exp_f, exp_b = _expand_matrix(0), _expand_matrix(SSD_HEADS)
    rot = dict(
        expand_f=jnp.asarray(exp_f, BF16),
        expand_b=jnp.asarray(exp_b, BF16),
        expand_fb=jnp.asarray(np.concatenate([exp_f, exp_b], axis=1), BF16),
    )
    h = x.reshape(t, D_MODEL)
    for i in range(depth):
        consts = dict(tables)
        consts.update(rot)
        consts.update(
            conv_w_xs=ssd_conv_w[i][:, :SSD_WIDTH], conv_b_xs=row(ssd_conv_b[i][:SSD_WIDTH]),
            conv_w_bc=ssd_conv_w[i][:, SSD_WIDTH:], conv_b_bc=row(ssd_conv_b[i][SSD_WIDTH:]),
            dt_bias=_pad_lanes(ssd_dt_bias[i]),
            a_row2=_pad_lanes(-jnp.exp(ssd_a_log[i].astype(F32)) * LOG2E),
            ret_norm_w=row(ret_norm_w[i]),
            d_exp=row(jnp.repeat(ssd_d[i], SSD_HEAD_DIM)),
            ssd_norm_w=row(ssd_norm_w[i]),
        )
        w_all = w_in[i].astype(BF16)
        w_dt = jnp.pad(w_all[:, N_MAIN:], ((0, 0), (0, LANES - N_DT)))

        (proj, dt, cs), (wgu16, wd16, wo16, wpg16) = _in_projection(
            h, row(norm_mix_w[i]), w_all, w_dt, positions,
            [(ffn_w_gate[i], ffn_w_up[i], FFN_TILE), ffn_w_down[i], w_out[i], ple_w_gate[i]])
        proj3 = proj.reshape(b, l, N_MAIN)
        dt3 = dt.reshape(b, l, LANES)
        cs3 = cs.reshape(b, l, LANES)
        (qr, kr, krt, xbc, rb, sb), _ = _prep_call(proj3, dt3, cs3, consts, dec_b, [])
        mix, _ = _mix_call(proj3, qr, kr, krt, xbc, dt3, rb, sb, consts, dec_f, [])
        h, hn = _out_projection(mix.reshape(t, D_MODEL), wo16, h, row(norm_ffn_w[i]))
        delta = _ffn_call(hn, wgu16, ffn_conv_w[i], row(ffn_conv_b[i]), wd16, l)
        h = _ple_call(h, delta, p[i].reshape(t, D_PLE), row(ple_norm_w[i]), wpg16,
                      row(ple_b_gate[i]), ple_w_proj[i].astype(BF16), row(final_norm_w),
                      apply_final=(i == depth - 1))
    return h.reshape(b, l, D_MODEL)
```

```python
import functools
import math

import numpy as np
import jax
import jax.numpy as jnp
from jax import lax
from jax.experimental import pallas as pl
from jax.experimental.pallas import tpu as pltpu

F32 = jnp.float32
BF16 = jnp.bfloat16

D_MODEL = 2048
EPS = 1e-6
D_PLE = 256
RET_WIDTH = D_MODEL // 2
RET_HEAD_DIM = 128
RET_HEADS = RET_WIDTH // RET_HEAD_DIM
ROPE_BASE = 10000.0
SSD_WIDTH = D_MODEL - RET_WIDTH
SSD_HEAD_DIM = 64
SSD_HEADS = SSD_WIDTH // SSD_HEAD_DIM
SSD_GROUPS = 2
SSD_HEADS_PER_GROUP = SSD_HEADS // SSD_GROUPS
SSD_STATE = 128
SSD_CONV = 5
SSD_BC = 2 * SSD_GROUPS * SSD_STATE
SSD_CONV_DIM = SSD_WIDTH + SSD_BC
D_FF = (11 * D_MODEL) // 4
FFN_CONV = 3
N_MAIN = 4 * RET_WIDTH + SSD_WIDTH + SSD_CONV_DIM
N_DT = 2 * SSD_HEADS

CHUNK = 128
LANES = 128
HALO = 16
GROUP_W = SSD_WIDTH // SSD_GROUPS
FFN_TILE = 512

COL_Q, COL_K, COL_V, COL_G, COL_Z, COL_XS = 0, 1, 2, 3, 4, 5
COL_BC = (5 * RET_WIDTH + SSD_WIDTH) // SSD_BC

VMEM_LIMIT = 56 * 1024 * 1024
LOG2E = math.log2(math.e)


def _cparams(n_axes):
    return pltpu.CompilerParams(dimension_semantics=("arbitrary",) * n_axes,
                                vmem_limit_bytes=VMEM_LIMIT)


def _rms(xf, w_row):
    ms = jnp.mean(xf * xf, axis=-1, keepdims=True)
    return xf * lax.rsqrt(ms + EPS) * w_row


def _silu(x):
    return x * jax.nn.sigmoid(x)


def _softplus(x):
    return jnp.maximum(x, 0.0) + jnp.log1p(jnp.exp(-jnp.abs(x)))


def _gelu_tanh(x):
    c = math.sqrt(2.0 / math.pi)
    return 0.5 * x * (1.0 + jnp.tanh(c * (x + 0.044715 * (x * x * x))))


def _dot(a, b):
    return jnp.dot(a, b, preferred_element_type=F32)


def _dot_nt(a, b):
    return lax.dot_general(a, b, (((1,), (1,)), ((), ())), preferred_element_type=F32)


def _split3(a):
    hi = a.astype(BF16)
    r1 = a - hi.astype(F32)
    mid = r1.astype(BF16)
    lo = (r1 - mid.astype(F32)).astype(BF16)
    return hi, mid, lo


def _dot_exact_lhs(m01, parts):
    hi, mid, lo = parts
    return _dot(m01, hi) + _dot(m01, mid) + _dot(m01, lo)


def _col_bcast(row):
    return jnp.broadcast_to(row, (LANES, LANES)).T


def _tri_masks():
    r = lax.broadcasted_iota(jnp.int32, (CHUNK, CHUNK), 0)
    c = lax.broadcasted_iota(jnp.int32, (CHUNK, CHUNK), 1)
    lower = r >= c
    tri = jnp.where(lower, 1.0, 0.0).astype(BF16)
    tri_t = jnp.where(r <= c, 1.0, 0.0).astype(BF16)
    return lower, tri, tri_t


def _staggered_row_specs(tm, n_tiles, n_steps, n_split):
    assert n_split < n_steps and tm % n_split == 0
    tq = tm // n_split

    def spec(q):
        def index(i, j):
            nxt = jnp.minimum(i + (j >= n_steps - n_split + q).astype(jnp.int32), n_tiles - 1)
            return (nxt * n_split + q, 0)
        return pl.BlockSpec((tq, D_MODEL), index)

    return [spec(q) for q in range(n_split)]


def _inproj_kernel(n_split, n_col_steps, side_work, *refs):
    x_parts = refs[:n_split]
    (nw_ref, w_ref, wdt_ref, pos_ref, freq_ref, phase_ref,
     proj_ref, dt_ref, cs_ref, hn_ref) = refs[n_split:]
    tq = x_parts[0].shape[0]

    @pl.when(pl.program_id(1) == 0)
    def _():
        for q, x_ref in enumerate(x_parts):
            hn_ref[q * tq:(q + 1) * tq, :] = _rms(x_ref[...], nw_ref[...]).astype(BF16)
        dt_ref[...] = _dot(hn_ref[...], wdt_ref[...])

    tn = w_ref.shape[1]
    gate_lo, gate_hi = COL_G * RET_WIDTH, COL_XS * RET_WIDTH
    for step in range(n_col_steps):
        lo = min(max(gate_lo - step * tn, 0), tn)
        hi = min(max(gate_hi - step * tn, 0), tn)

        @pl.when(pl.program_id(1) == step)
        def _():
            r = _dot(hn_ref[...], w_ref[...])
            if lo > 0:
                proj_ref[:, :lo] = r[:, :lo].astype(BF16)
            if hi > lo:
                proj_ref[:, lo:hi] = _silu(r[:, lo:hi]).astype(BF16)
            if hi < tn:
                proj_ref[:, hi:] = r[:, hi:].astype(BF16)
            side_work()
            if step == n_col_steps - 1:
                ang_t = freq_ref[...] * pos_ref[...].astype(F32) - phase_ref[...]
                cs_ref[...] = jnp.cos(ang_t).T


def _in_projection(x2, norm_w, w_all, w_dt, positions, cast_weights,
                   tm=1024, tn=N_MAIN // 4, n_split=2):
    t = x2.shape[0]
    n_rows, n_cols = t // tm, N_MAIN // tn
    half = RET_HEAD_DIM // 2
    inv_freq = ROPE_BASE ** (-jnp.arange(half, dtype=F32) / half)
    freq = jnp.broadcast_to(jnp.concatenate([inv_freq, inv_freq])[:, None], (LANES, tm))
    phase = jnp.broadcast_to(
        jnp.concatenate([jnp.zeros((half,), F32), jnp.full((half,), math.pi / 2, F32)])[:, None],
        (LANES, tm))
    in_specs = _staggered_row_specs(tm, n_rows, n_cols, n_split) + [
        pl.BlockSpec((1, D_MODEL), lambda i, j: (0, 0)),
        pl.BlockSpec((D_MODEL, tn), lambda i, j: (0, j)),
        pl.BlockSpec((D_MODEL, LANES), lambda i, j: (0, 0)),
        pl.BlockSpec((None, 1, tm), lambda i, j: (i, 0, 0)),
        pl.BlockSpec((LANES, tm), lambda i, j: (0, 0)),
        pl.BlockSpec((LANES, tm), lambda i, j: (0, 0)),
    ]
    out_specs = [
        pl.BlockSpec((tm, tn), lambda i, j: (i, j)),
        pl.BlockSpec((tm, LANES), lambda i, j: (i, 0)),
        pl.BlockSpec((tm, LANES), lambda i, j: (i, 0)),
    ]
    out_shape = [
        jax.ShapeDtypeStruct((t, N_MAIN), BF16),
        jax.ShapeDtypeStruct((t, LANES), F32),
        jax.ShapeDtypeStruct((t, LANES), F32),
    ]
    cast_in, cast_out, cast_shapes, jobs, cast_srcs = _cast_slab_specs(
        cast_weights, n_rows * n_cols, n_cols)
    outs = pl.pallas_call(
        functools.partial(_sweep_with_casts, functools.partial(_inproj_kernel, n_split, n_cols),
                          jobs, n_rows * n_cols, len(in_specs), len(out_specs)),
        grid=(n_rows, n_cols),
        in_specs=in_specs + cast_in,
        out_specs=out_specs + cast_out,
        out_shape=out_shape + cast_shapes,
        scratch_shapes=[pltpu.VMEM((tm, D_MODEL), BF16)],
        compiler_params=_cparams(2),
        name="in_projection",
    )(*([x2] * n_split), norm_w, w_all, w_dt, positions.reshape(n_rows, 1, tm), freq, phase,
      *cast_srcs)
    return outs[:3], outs[3:]


def _conv5_silu(prev, cur, nxt, w_ref, b_ref, has_prev, has_next):
    zero = jnp.zeros_like(prev)
    depth = 2 * LANES
    ext = jnp.concatenate(
        [jnp.where(has_prev, prev, zero), cur, jnp.where(has_next, nxt, zero),
         jnp.zeros((depth - CHUNK - 2 * HALO, cur.shape[1]), cur.dtype)], axis=0)
    r = lax.broadcasted_iota(jnp.int32, (CHUNK, depth), 0)
    c = lax.broadcasted_iota(jnp.int32, (CHUNK, depth), 1)
    pad = SSD_CONV // 2
    acc = b_ref[...] + w_ref[pad:pad + 1, :] * cur.astype(F32)
    for j in range(SSD_CONV):
        if j != pad:
            shift = jnp.where(c == r + (HALO + j - pad), 1.0, 0.0).astype(BF16)
            acc = acc + w_ref[j:j + 1, :] * _dot(shift, ext)
    return _silu(acc)


def _expand_rows(w, tot_row, expand01):
    stacked = jnp.concatenate([w, jnp.broadcast_to(tot_row, (HALO, LANES))], axis=0)
    e = _dot(stacked.astype(BF16), expand01)
    return e[:CHUNK], e[CHUNK:CHUNK + 1]


def _state_increment(bm_f32, xw):
    parts = []
    for g in range(SSD_GROUPS):
        bm_t = bm_f32[:, g * SSD_STATE:(g + 1) * SSD_STATE].T.astype(BF16)
        parts.append(_dot(bm_t, xw[:, g * GROUP_W:(g + 1) * GROUP_W]))
    return jnp.concatenate(parts, axis=1)


def _cast_slab_specs(weights, n_steps, nc):
    in_specs, out_specs, out_shapes, jobs, flat = [], [], [], [], []
    for entry in weights:
        srcs = entry[:2] if isinstance(entry, tuple) else (entry,)
        tile = entry[2] if isinstance(entry, tuple) else 0
        n_rows, n_cols = srcs[0].shape
        rows = next(r for r in range(HALO, n_rows + 1, HALO)
                    if n_rows % r == 0 and n_rows // r <= n_steps)
        n_active = n_rows // rows

        def index(bi, i, _last=n_active - 1):
            return (jnp.minimum(bi * nc + i, _last), 0)

        for w in srcs:
            in_specs.append(pl.BlockSpec((rows, n_cols), index))
            flat.append(w)
        if tile:
            n_tiles = n_cols // tile
            out_specs.append(pl.BlockSpec((n_tiles, rows, 2 * tile),
                                          lambda bi, i, _index=index: (0,) + _index(bi, i)))
            out_shapes.append(jax.ShapeDtypeStruct((n_tiles, n_rows, 2 * tile), BF16))
        else:
            out_specs.append(pl.BlockSpec((rows, n_cols), index))
            out_shapes.append(jax.ShapeDtypeStruct((n_rows, n_cols), BF16))
        jobs.append((n_active, len(srcs), tile))
    return in_specs, out_specs, out_shapes, tuple(jobs), flat


def _sweep_with_casts(body, jobs, n_steps, n_in, n_out, *refs):
    n_src = sum(job[1] for job in jobs)
    ins, refs = refs[:n_in], refs[n_in:]
    srcs, refs = refs[:n_src], refs[n_src:]
    outs, refs = refs[:n_out], refs[n_out:]
    dsts, scratch = refs[:len(jobs)], refs[len(jobs):]
    step = pl.program_id(0) * pl.num_programs(1) + pl.program_id(1)

    def narrow(job_srcs, dst, tile):
        if tile:
            for t in range(dst.shape[0]):
                for k, src in enumerate(job_srcs):
                    dst[t, :, k * tile:(k + 1) * tile] = src[:, t * tile:(t + 1) * tile].astype(BF16)
        else:
            dst[...] = job_srcs[0][...].astype(BF16)

    every_step = []
    for (n_active, n_job_src, tile), dst in zip(jobs, dsts):
        job_srcs, srcs = srcs[:n_job_src], srcs[n_job_src:]
        if n_active == n_steps:
            every_step.append(functools.partial(narrow, job_srcs, dst, tile))
        else:
            pl.when(step < n_active)(functools.partial(narrow, job_srcs, dst, tile))

    def side_work():
        for f in every_step:
            f()

    body(side_work, *ins, *outs, *scratch)


def _prep_kernel(ret_dec_b, side_work,
                 cs_ref, q_ref, k_ref, v_ref,
                 xs_ref, xsp_ref, xsn_ref, bc_ref, bcp_ref, bcn_ref, dt_ref,
                 cwx_ref, cbx_ref, cwb_ref, cbb_ref, dtbias_ref, arow2_ref,
                 kb_ref, expand_ref,
                 qr_ref, kr_ref, krt_ref, xbc_ref, rb_ref, sb_ref,
                 rb_state, sb_state):
    i = pl.program_id(1)
    nc = pl.num_programs(1)

    @pl.when(i == 0)
    def _():
        rb_state[...] = jnp.zeros_like(rb_state)
        sb_state[...] = jnp.zeros_like(sb_state)

    side_work()
    has_next = i > 0
    has_prev = i < nc - 1

    half = RET_HEAD_DIM // 2
    cs = cs_ref[...]
    sc = pltpu.roll(cs, half, axis=1)
    lane_lo = lax.broadcasted_iota(jnp.int32, (CHUNK, LANES), 1) < half
    cosv = jnp.where(lane_lo, cs, sc)
    sinv = jnp.where(lane_lo, -sc, cs)

    def rot(t, c, s):
        return t * c + pltpu.roll(t, half, axis=1) * s

    kscale = RET_HEAD_DIM ** -0.5
    cosk = cosv * kscale
    sink = sinv * kscale

    def rotate_head(h):
        sl = slice(h * RET_HEAD_DIM, (h + 1) * RET_HEAD_DIM)
        qr_ref[:, sl] = rot(q_ref[:, sl].astype(F32), cosv, sinv).astype(BF16)
        kr = rot(k_ref[:, sl].astype(F32), cosk, sink)
        kr_ref[:, sl] = kr.astype(BF16)
        kt = kr.T
        krt_ref[h] = kt.astype(BF16)
        rb_ref[h] = rb_state[h].astype(BF16)
        rb_state[h] = rb_state[h] * ret_dec_b[h] + _dot((kt * kb_ref[h:h + 1, :]).astype(BF16),
                                                        v_ref[:, sl])

    slab_w = 2 * LANES
    n_x_slabs = SSD_WIDTH // slab_w

    def conv_slab(c):
        if c < n_x_slabs:
            prev, cur, nxt, w_ref, b_ref = xsp_ref, xs_ref, xsn_ref, cwx_ref, cbx_ref
        else:
            c -= n_x_slabs
            prev, cur, nxt, w_ref, b_ref = bcp_ref, bc_ref, bcn_ref, cwb_ref, cbb_ref
        cs = slice(c * slab_w, (c + 1) * slab_w)
        return _conv5_silu(prev[:, cs], cur[:, cs], nxt[:, cs], w_ref.at[:, cs], b_ref.at[:, cs],
                           has_prev, has_next)

    slabs = []
    n_slabs = SSD_CONV_DIM // slab_w
    assert n_slabs <= RET_HEADS
    for h in range(RET_HEADS):
        if h < n_slabs:
            slabs.append(conv_slab(h))
        rotate_head(h)
    xs = jnp.concatenate(slabs[:n_x_slabs], axis=1)
    bc = jnp.concatenate(slabs[n_x_slabs:], axis=1)
    xbc_ref[:, :SSD_WIDTH] = xs.astype(BF16)
    xbc_ref[:, SSD_WIDTH:] = bc.astype(BF16)

    _, _, tri_t = _tri_masks()
    dt = _softplus(dt_ref[...] + dtbias_ref[...])
    rcs = _dot_exact_lhs(tri_t, _split3(dt * arow2_ref[...]))
    tot = rcs[0:1, :]
    w = jnp.exp2(tot - rcs) * dt
    wexp, cdec = _expand_rows(w, jnp.exp2(tot), expand_ref[...])
    xw = (xs * wexp).astype(BF16)
    ds = _state_increment(bc[:, :SSD_GROUPS * SSD_STATE], xw)
    sb_ref[...] = sb_state[...].astype(BF16)
    sb_state[...] = sb_state[...] * cdec + ds


def _prep_call(proj3, dt3, cs3, consts, ret_dec_b, cast_weights):
    b, l, _ = proj3.shape
    nc = l // CHUNK
    rows16 = l // HALO
    per16 = CHUNK // HALO

    def cix(i):
        return nc - 1 - i

    def col(cb):
        return lambda bi, i: (bi, cix(i), cb)

    def prev_halo(cb):
        return lambda bi, i: (bi, jnp.maximum(cix(i) * per16 - 1, 0), cb)

    def next_halo(cb):
        return lambda bi, i: (bi, jnp.minimum((cix(i) + 1) * per16, rows16 - 1), cb)

    def const(shape):
        return pl.BlockSpec(shape, lambda bi, i: (0,) * len(shape))

    in_specs = [
        pl.BlockSpec((None, CHUNK, LANES), col(0)),
        pl.BlockSpec((None, CHUNK, RET_WIDTH), col(COL_Q)),
        pl.BlockSpec((None, CHUNK, RET_WIDTH), col(COL_K)),
        pl.BlockSpec((None, CHUNK, RET_WIDTH), col(COL_V)),
        pl.BlockSpec((None, CHUNK, SSD_WIDTH), col(COL_XS)),
        pl.BlockSpec((None, HALO, SSD_WIDTH), prev_halo(COL_XS)),
        pl.BlockSpec((None, HALO, SSD_WIDTH), next_halo(COL_XS)),
        pl.BlockSpec((None, CHUNK, SSD_BC), col(COL_BC)),
        pl.BlockSpec((None, HALO, SSD_BC), prev_halo(COL_BC)),
        pl.BlockSpec((None, HALO, SSD_BC), next_halo(COL_BC)),
        pl.BlockSpec((None, CHUNK, LANES), col(0)),
        const((SSD_CONV, SSD_WIDTH)), const((1, SSD_WIDTH)),
        const((SSD_CONV, SSD_BC)), const((1, SSD_BC)),
        const((1, LANES)), const((1, LANES)),
        const((RET_HEADS, LANES)),
        const((LANES, SSD_WIDTH)),
    ]
    out_specs = [
        pl.BlockSpec((None, CHUNK, RET_WIDTH), col(0)),
        pl.BlockSpec((None, CHUNK, RET_WIDTH), col(0)),
        pl.BlockSpec((None, None, RET_HEADS, RET_HEAD_DIM, CHUNK),
                     lambda bi, i: (bi, cix(i), 0, 0, 0)),
        pl.BlockSpec((None, CHUNK, SSD_CONV_DIM), col(0)),
        pl.BlockSpec((None, None, RET_HEADS, RET_HEAD_DIM, RET_HEAD_DIM),
                     lambda bi, i: (bi, cix(i), 0, 0, 0)),
        pl.BlockSpec((None, None, SSD_STATE, SSD_WIDTH), lambda bi, i: (bi, cix(i), 0, 0)),
    ]
    out_shape = [
        jax.ShapeDtypeStruct((b, l, RET_WIDTH), BF16),
        jax.ShapeDtypeStruct((b, l, RET_WIDTH), BF16),
        jax.ShapeDtypeStruct((b, nc, RET_HEADS, RET_HEAD_DIM, CHUNK), BF16),
        jax.ShapeDtypeStruct((b, l, SSD_CONV_DIM), BF16),
        jax.ShapeDtypeStruct((b, nc, RET_HEADS, RET_HEAD_DIM, RET_HEAD_DIM), BF16),
        jax.ShapeDtypeStruct((b, nc, SSD_STATE, SSD_WIDTH), BF16),
    ]
    scratch = [
        pltpu.VMEM((RET_HEADS, RET_HEAD_DIM, RET_HEAD_DIM), F32),
        pltpu.VMEM((SSD_STATE, SSD_WIDTH), F32),
    ]
    cast_in, cast_out, cast_shapes, active, cast_srcs = _cast_slab_specs(cast_weights, b * nc, nc)
    outs = pl.pallas_call(
        functools.partial(_sweep_with_casts, functools.partial(_prep_kernel, ret_dec_b),
                          active, b * nc, len(in_specs), len(out_specs)),
        grid=(b, nc),
        in_specs=in_specs + cast_in,
        out_specs=out_specs + cast_out,
        out_shape=out_shape + cast_shapes,
        scratch_shapes=scratch,
        compiler_params=_cparams(2),
        name="reverse_sweep",
    )(cs3, proj3, proj3, proj3, proj3, proj3, proj3, proj3, proj3, proj3, dt3,
      consts["conv_w_xs"], consts["conv_b_xs"], consts["conv_w_bc"], consts["conv_b_bc"],
      consts["dt_bias"], consts["a_row2"], consts["ret_kb"], consts["expand_b"], *cast_srcs)
    return outs[:len(out_specs)], outs[len(out_specs):]


def _mix_kernel(ret_dec_f, side_work,
                qr_ref, kr_ref, krt_ref, v_ref, g_ref, z_ref, xbc_ref, dt_ref, rb_ref, sb_ref,
                mask_ref, qf_ref, qb_ref, kf_ref, retnw_ref,
                dtbias_ref, arow2_ref, expand_ref, expand2_ref, dexp_ref, ssdnw_ref,
                out_ref,
                rf_state, sf_state):
    i = pl.program_id(1)

    @pl.when(i == 0)
    def _():
        rf_state[...] = jnp.zeros_like(rf_state)
        sf_state[...] = jnp.zeros_like(sf_state)

    side_work()
    lower, tri, tri_t = _tri_masks()
    dt = _softplus(dt_ref[...] + dtbias_ref[...])
    parts = _split3(dt * arow2_ref[...])
    lane = lax.broadcasted_iota(jnp.int32, (CHUNK, LANES), 1)
    prefix = _dot_exact_lhs(tri, parts)
    acs = jnp.where(lane < SSD_HEADS, prefix, _dot_exact_lhs(tri_t, parts))
    acs_t = acs.T
    src_t = (acs - jnp.log2(dt)).T
    edge = _dot(jnp.exp2(acs).astype(BF16), expand2_ref[...])
    xs16 = xbc_ref[:, :SSD_WIDTH]
    bm = xbc_ref[:, SSD_WIDTH:SSD_WIDTH + SSD_GROUPS * SSD_STATE]
    cm = xbc_ref[:, SSD_WIDTH + SSD_GROUPS * SSD_STATE:]
    lane_lo = lane < SSD_HEAD_DIM

    cbs, y_offs = [], []
    for g in range(SSD_GROUPS):
        gs = slice(g * SSD_STATE, (g + 1) * SSD_STATE)
        gw = slice(g * GROUP_W, (g + 1) * GROUP_W)
        gwb = slice(SSD_WIDTH + g * GROUP_W, SSD_WIDTH + (g + 1) * GROUP_W)
        cm_g = cm[:, gs]
        cbs.append(_dot_nt(cm_g, bm[:, gs]))
        y_offs.append(edge[:, gw] * _dot(cm_g, sf_state[:, gw].astype(BF16))
                      + edge[:, gwb] * _dot(cm_g, sb_ref[:, gw]))

    def decay_matrix(e):
        eb = SSD_HEADS + e
        dst = jnp.where(lower, _col_bcast(acs_t[e:e + 1, :]), _col_bcast(acs_t[eb:eb + 1, :]))
        src = jnp.where(lower, src_t[e:e + 1, :], src_t[eb:eb + 1, :])
        return (cbs[e // SSD_HEADS_PER_GROUP] * jnp.exp2(dst - src)).astype(BF16)

    ys = []
    for k in range(RET_HEADS):
        sl = slice(k * RET_HEAD_DIM, (k + 1) * RET_HEAD_DIM)
        ms = [decay_matrix(2 * k), decay_matrix(2 * k + 1)]
        qh = qr_ref[:, sl]
        kh = kr_ref[:, sl]
        vh = v_ref[:, sl]
        s = (_dot_nt(qh, kh) * mask_ref[k]).astype(BF16)
        lhs = jnp.concatenate([s, qh * qf_ref[k], qh * qb_ref[k]], axis=1)
        rhs = jnp.concatenate([vh, rf_state[k].astype(BF16), rb_ref[k]], axis=0)
        o = _dot(lhs, rhs)
        rf_state[k] = rf_state[k] * ret_dec_f[k] + _dot(krt_ref[k], vh * kf_ref[k])
        xs_pair = xs16[:, k * LANES:(k + 1) * LANES]
        prods = [_dot(m, xs_pair) for m in ms]
        o = _rms(o, retnw_ref[:, sl])
        out_ref[:, sl] = (g_ref[:, sl].astype(F32) * o).astype(BF16)
        g, q = divmod(k, SSD_HEADS_PER_GROUP // 2)
        ys.append(jnp.where(lane_lo, prods[0], prods[1]) + y_offs[g][:, q * LANES:(q + 1) * LANES])
    y = jnp.concatenate(ys, axis=1)

    xs = xs16.astype(F32)
    y = (y + dexp_ref[...] * xs) * z_ref[...].astype(F32)
    for g in range(SSD_GROUPS):
        gs = slice(g * GROUP_W, (g + 1) * GROUP_W)
        out_ref[:, RET_WIDTH + g * GROUP_W:RET_WIDTH + (g + 1) * GROUP_W] = _rms(
            y[:, gs], ssdnw_ref[:, gs]).astype(BF16)

    tot = prefix[CHUNK - 1:CHUNK, :]
    w = jnp.exp2(tot - prefix) * dt
    wexp, cdec = _expand_rows(w, jnp.exp2(tot), expand_ref[...])
    xw = (xs * wexp).astype(BF16)
    ds = _state_increment(bm.astype(F32), xw)
    sf_state[...] = sf_state[...] * cdec + ds


def _mix_call(proj3, qr, kr, krt, xbc, dt3, rb, sb, consts, ret_dec_f, cast_weights):
    b, l, _ = proj3.shape
    nc = l // CHUNK

    def col(cb):
        return lambda bi, i: (bi, i, cb)

    def const(shape):
        return pl.BlockSpec(shape, lambda bi, i: (0,) * len(shape))

    in_specs = [
        pl.BlockSpec((None, CHUNK, RET_WIDTH), col(0)),
        pl.BlockSpec((None, CHUNK, RET_WIDTH), col(0)),
        pl.BlockSpec((None, None, RET_HEADS, RET_HEAD_DIM, CHUNK), lambda bi, i: (bi, i, 0, 0, 0)),
        pl.BlockSpec((None, CHUNK, RET_WIDTH), col(COL_V)),
        pl.BlockSpec((None, CHUNK, RET_WIDTH), col(COL_G)),
        pl.BlockSpec((None, CHUNK, SSD_WIDTH), col(COL_Z)),
        pl.BlockSpec((None, CHUNK, SSD_CONV_DIM), col(0)),
        pl.BlockSpec((None, CHUNK, LANES), col(0)),
        pl.BlockSpec((None, None, RET_HEADS, RET_HEAD_DIM, RET_HEAD_DIM),
                     lambda bi, i: (bi, i, 0, 0, 0)),
        pl.BlockSpec((None, None, SSD_STATE, SSD_WIDTH), lambda bi, i: (bi, i, 0, 0)),
        const((RET_HEADS, CHUNK, CHUNK)), const((RET_HEADS, CHUNK, LANES)),
        const((RET_HEADS, CHUNK, LANES)), const((RET_HEADS, CHUNK, LANES)), const((1, RET_WIDTH)),
        const((1, LANES)), const((1, LANES)), const((LANES, SSD_WIDTH)),
        const((LANES, 2 * SSD_WIDTH)), const((1, SSD_WIDTH)), const((1, SSD_WIDTH)),
    ]
    cast_in, cast_out, cast_shapes, active, cast_srcs = _cast_slab_specs(cast_weights, b * nc, nc)
    outs = pl.pallas_call(
        functools.partial(_sweep_with_casts, functools.partial(_mix_kernel, ret_dec_f),
                          active, b * nc, len(in_specs), 1),
        grid=(b, nc),
        in_specs=in_specs + cast_in,
        out_specs=[pl.BlockSpec((None, CHUNK, D_MODEL), col(0))] + cast_out,
        out_shape=[jax.ShapeDtypeStruct((b, l, D_MODEL), BF16)] + cast_shapes,
        scratch_shapes=[
            pltpu.VMEM((RET_HEADS, RET_HEAD_DIM, RET_HEAD_DIM), F32),
            pltpu.VMEM((SSD_STATE, SSD_WIDTH), F32),
        ],
        compiler_params=_cparams(2),
        name="forward_sweep",
    )(qr, kr, krt, proj3, proj3, proj3, xbc, dt3, rb, sb,
      consts["ret_mask"], consts["ret_qf"], consts["ret_qb"], consts["ret_kf"],
      consts["ret_norm_w"], consts["dt_bias"], consts["a_row2"], consts["expand_f"],
      consts["expand_fb"], consts["d_exp"], consts["ssd_norm_w"], *cast_srcs)
    return outs[0], outs[1:]


def _outproj_kernel(mix_ref, w_ref, x_ref, nw_ref, h_ref, hn_ref):
    h = x_ref[...] + _dot(mix_ref[...], w_ref[...])
    h_ref[...] = h
    hn_ref[...] = _rms(h, nw_ref[...]).astype(BF16)


def _out_projection(mix2, w_out, x2, ffn_norm_w, tm=512):
    t = x2.shape[0]
    return pl.pallas_call(
        _outproj_kernel,
        grid=(t // tm,),
        in_specs=[
            pl.BlockSpec((tm, D_MODEL), lambda i: (i, 0)),
            pl.BlockSpec((D_MODEL, D_MODEL), lambda i: (0, 0)),
            pl.BlockSpec((tm, D_MODEL), lambda i: (i, 0)),
            pl.BlockSpec((1, D_MODEL), lambda i: (0, 0)),
        ],
        out_specs=[
            pl.BlockSpec((tm, D_MODEL), lambda i: (i, 0)),
            pl.BlockSpec((tm, D_MODEL), lambda i: (i, 0)),
        ],
        out_shape=[
            jax.ShapeDtypeStruct((t, D_MODEL), F32),
            jax.ShapeDtypeStruct((t, D_MODEL), BF16),
        ],
        compiler_params=_cparams(1),
        name="out_projection",
    )(mix2, w_out, x2, ffn_norm_w)


def _ffn_kernel(tiles_per_seq, hn_ref, hp_ref, hx_ref, wgu_ref, cw_ref, cb_ref, wd_ref,
                o_ref, hbuf, gu_s):
    i = pl.program_id(0)
    j = pl.program_id(1)
    tm = hn_ref.shape[0]
    tf = cw_ref.shape[1]

    def column_step(first):
        if first:
            pos_in_seq = i % tiles_per_seq
            zero = jnp.zeros((HALO, D_MODEL), BF16)
            hbuf[0:HALO, :] = jnp.where(pos_in_seq > 0, hp_ref[...], zero)
            hbuf[HALO:HALO + tm, :] = hn_ref[...]
            hbuf[HALO + tm:, :] = jnp.where(pos_in_seq < tiles_per_seq - 1, hx_ref[...], zero)
        gu_s[...] = _dot(hbuf[...], wgu_ref[...])
        pad = FFN_CONV // 2
        gate = cb_ref[...]
        for t in range(FFN_CONV):
            o = HALO + t - pad
            gate = gate + cw_ref[t:t + 1, :] * gu_s[o:o + tm, :tf]
        act = (_gelu_tanh(gate) * gu_s[HALO:HALO + tm, tf:]).astype(BF16)
        if first:
            o_ref[...] = _dot(act, wd_ref[...])
        else:
            o_ref[...] += _dot(act, wd_ref[...])

    pl.when(j == 0)(functools.partial(column_step, True))
    pl.when(j > 0)(functools.partial(column_step, False))


def _ffn_call(hn2, wgu, conv_w, conv_b, wd, seq_len, tm=1024):
    t = hn2.shape[0]
    tf = wgu.shape[2] // 2
    per16 = tm // HALO
    rows16 = t // HALO
    return pl.pallas_call(
        functools.partial(_ffn_kernel, seq_len // tm),
        grid=(t // tm, D_FF // tf),
        in_specs=[
            pl.BlockSpec((tm, D_MODEL), lambda i, j: (i, 0)),
            pl.BlockSpec((HALO, D_MODEL), lambda i, j: (jnp.maximum(i * per16 - 1, 0), 0)),
            pl.BlockSpec((HALO, D_MODEL), lambda i, j: (jnp.minimum((i + 1) * per16, rows16 - 1), 0)),
            pl.BlockSpec((None, D_MODEL, 2 * tf), lambda i, j: (j, 0, 0)),
            pl.BlockSpec((FFN_CONV, tf), lambda i, j: (0, j)),
            pl.BlockSpec((1, tf), lambda i, j: (0, j)),
            pl.BlockSpec((tf, D_MODEL), lambda i, j: (j, 0)),
        ],
        out_specs=pl.BlockSpec((tm, D_MODEL), lambda i, j: (i, 0)),
        out_shape=jax.ShapeDtypeStruct((t, D_MODEL), F32),
        scratch_shapes=[
            pltpu.VMEM((tm + 2 * HALO, D_MODEL), BF16),
            pltpu.VMEM((tm + 2 * HALO, 2 * tf), F32),
        ],
        compiler_params=_cparams(2),
        name="conv_glu_ffn",
    )(hn2, hn2, hn2, wgu, conv_w, conv_b, wd)


def _ple_kernel(apply_final, h_ref, d_ref, p_ref, nw_ref, wg_ref, bg_ref, wp_ref, fw_ref, o_ref):
    h = h_ref[...] + d_ref[...]
    hn = _rms(h, nw_ref[...]).astype(BF16)
    gate = jax.nn.sigmoid(_dot(hn, wg_ref[...]) + bg_ref[...])
    h = h + gate * _dot(p_ref[...].astype(BF16), wp_ref[...])
    if apply_final:
        h = _rms(h, fw_ref[...])
    o_ref[...] = h


def _ple_call(h2, delta2, p2, norm_w, wg, bg, wp, final_w, apply_final, tm=512):
    t = h2.shape[0]
    return pl.pallas_call(
        functools.partial(_ple_kernel, apply_final),
        grid=(t // tm,),
        in_specs=[
            pl.BlockSpec((tm, D_MODEL), lambda i: (i, 0)),
            pl.BlockSpec((tm, D_MODEL), lambda i: (i, 0)),
            pl.BlockSpec((tm, D_PLE), lambda i: (i, 0)),
            pl.BlockSpec((1, D_MODEL), lambda i: (0, 0)),
            pl.BlockSpec((D_MODEL, D_MODEL), lambda i: (0, 0)),
            pl.BlockSpec((1, D_MODEL), lambda i: (0, 0)),
            pl.BlockSpec((D_PLE, D_MODEL), lambda i: (0, 0)),
            pl.BlockSpec((1, D_MODEL), lambda i: (0, 0)),
        ],
        out_specs=pl.BlockSpec((tm, D_MODEL), lambda i: (i, 0)),
        out_shape=jax.ShapeDtypeStruct((t, D_MODEL), F32),
        compiler_params=_cparams(1),
        name="ple_gate",
    )(h2, delta2, p2, norm_w, wg, bg, wp, final_w)


def _retention_tables():
    hh = np.arange(RET_HEADS, dtype=np.float64)
    lf = np.log1p(-np.exp2(-5.0 - hh))
    lb = np.log1p(-np.exp2(-5.5 - hh))
    idx = np.arange(CHUNK, dtype=np.float64)
    dist = idx[:, None] - idx[None, :]
    mask = np.where(dist >= 0, np.exp(lf[:, None, None] * np.abs(dist)),
                    np.exp(lb[:, None, None] * np.abs(dist)))
    ones = np.ones((1, 1, LANES))
    qf = np.exp(lf[:, None] * (idx + 1.0)[None, :])[:, :, None] * ones
    qb = np.exp(lb[:, None] * (CHUNK - idx)[None, :])[:, :, None] * ones
    kf = np.exp(lf[:, None] * (CHUNK - 1.0 - idx)[None, :])[:, :, None] * ones
    kb = np.exp(lb[:, None] * idx[None, :])
    dec_f = tuple(float(v) for v in np.exp(lf * CHUNK))
    dec_b = tuple(float(v) for v in np.exp(lb * CHUNK))
    f = lambda a: jnp.asarray(a, F32)
    h = lambda a: jnp.asarray(a, BF16)
    return dict(ret_mask=f(mask), ret_qf=h(qf), ret_qb=h(qb), ret_kf=h(kf), ret_kb=f(kb)), dec_f, dec_b


def _expand_matrix(first_row):
    e = np.zeros((LANES, SSD_WIDTH), np.float32)
    for h in range(SSD_HEADS):
        e[first_row + h, h * SSD_HEAD_DIM:(h + 1) * SSD_HEAD_DIM] = 1.0
    return e


def _pad_lanes(v):
    return jnp.pad(v.reshape(1, -1), ((0, 0), (0, LANES - v.size)))


def kernel(x, p, positions, norm_mix_w, w_in, ret_norm_w, ssd_conv_w, ssd_conv_b, ssd_dt_bias,
           ssd_a_log, ssd_d, ssd_norm_w, w_out, norm_ffn_w, ffn_w_gate, ffn_w_up, ffn_conv_w,
           ffn_conv_b, ffn_w_down, ple_norm_w, ple_w_gate, ple_b_gate, ple_w_proj, final_norm_w):
    b, l, _ = x.shape
    depth = w_in.shape[0]
    t = b * l
    nc = l // CHUNK
    row = lambda v: v.reshape(1, -1).astype(F32)

    tables, dec_f, dec_b = _retention_tables()
    exp_f, exp_b = _expand_matrix(0), _expand_matrix(SSD_HEADS)
    rot = dict(
        expand_f=jnp.asarray(exp_f, BF16),
        expand_b=jnp.asarray(exp_b, BF16),
        expand_fb=jnp.asarray(np.concatenate([exp_f, exp_b], axis=1), BF16),
    )
    h = x.reshape(t, D_MODEL)
    for i in range(depth):
        consts = dict(tables)
        consts.update(rot)
        consts.update(
            conv_w_xs=ssd_conv_w[i][:, :SSD_WIDTH], conv_b_xs=row(ssd_conv_b[i][:SSD_WIDTH]),
            conv_w_bc=ssd_conv_w[i][:, SSD_WIDTH:], conv_b_bc=row(ssd_conv_b[i][SSD_WIDTH:]),
            dt_bias=_pad_lanes(ssd_dt_bias[i]),
            a_row2=_pad_lanes(-jnp.exp(ssd_a_log[i].astype(F32)) * LOG2E),
            ret_norm_w=row(ret_norm_w[i]),
            d_exp=row(jnp.repeat(ssd_d[i], SSD_HEAD_DIM)),
            ssd_norm_w=row(ssd_norm_w[i]),
        )
        w_all = w_in[i].astype(BF16)
        w_dt = jnp.pad(w_all[:, N_MAIN:], ((0, 0), (0, LANES - N_DT)))

        (proj, dt, cs), (wgu16, wd16, wo16, wpg16) = _in_projection(
            h, row(norm_mix_w[i]), w_all, w_dt, positions,
            [(ffn_w_gate[i], ffn_w_up[i], FFN_TILE), ffn_w_down[i], w_out[i], ple_w_gate[i]])
        proj3 = proj.reshape(b, l, N_MAIN)
        dt3 = dt.reshape(b, l, LANES)
        cs3 = cs.reshape(b, l, LANES)
        (qr, kr, krt, xbc, rb, sb), _ = _prep_call(proj3, dt3, cs3, consts, dec_b, [])
        mix, _ = _mix_call(proj3, qr, kr, krt, xbc, dt3, rb, sb, consts, dec_f, [])
        h, hn = _out_projection(mix.reshape(t, D_MODEL), wo16, h, row(norm_ffn_w[i]))
        delta = _ffn_call(hn, wgu16, ffn_conv_w[i], row(ffn_conv_b[i]), wd16, l)
        h = _ple_call(h, delta, p[i].reshape(t, D_PLE), row(ple_norm_w[i]), wpg16,
                      row(ple_b_gate[i]), ple_w_proj[i].astype(BF16), row(final_norm_w),
                      apply_final=(i == depth - 1))
    return h.reshape(b, l, D_MODEL)
```

```python
import functools
import math

import numpy as np
import jax
import jax.numpy as jnp
from jax import lax
from jax.experimental import pallas as pl
from jax.experimental.pallas import tpu as pltpu

F32 = jnp.float32
BF16 = jnp.bfloat16

D_MODEL = 2048
EPS = 1e-6
D_PLE = 256
RET_WIDTH = D_MODEL // 2
RET_HEAD_DIM = 128
RET_HEADS = RET_WIDTH // RET_HEAD_DIM
ROPE_BASE = 10000.0
SSD_WIDTH = D_MODEL - RET_WIDTH
SSD_HEAD_DIM = 64
SSD_HEADS = SSD_WIDTH // SSD_HEAD_DIM
SSD_GROUPS = 2
SSD_HEADS_PER_GROUP = SSD_HEADS // SSD_GROUPS
SSD_STATE = 128
SSD_CONV = 5
SSD_BC = 2 * SSD_GROUPS * SSD_STATE
SSD_CONV_DIM = SSD_WIDTH + SSD_BC
D_FF = (11 * D_MODEL) // 4
FFN_CONV = 3
N_MAIN = 4 * RET_WIDTH + SSD_WIDTH + SSD_CONV_DIM
N_DT = 2 * SSD_HEADS

CHUNK = 128
LANES = 128
HALO = 16
GROUP_W = SSD_WIDTH // SSD_GROUPS
FFN_TILE = 512

COL_Q, COL_K, COL_V, COL_G, COL_Z, COL_XS = 0, 1, 2, 3, 4, 5
COL_BC = (5 * RET_WIDTH + SSD_WIDTH) // SSD_BC
OFF_K_END = 2 * RET_WIDTH

VMEM_LIMIT = 60 * 1024 * 1024
LOG2E = math.log2(math.e)


def _cparams(n_axes):
    return pltpu.CompilerParams(dimension_semantics=("arbitrary",) * n_axes,
                                vmem_limit_bytes=VMEM_LIMIT)


def _rms(xf, w_row):
    ms = jnp.mean(xf * xf, axis=-1, keepdims=True)
    return xf * lax.rsqrt(ms + EPS) * w_row


def _silu(x):
    return x * jax.nn.sigmoid(x)


def _softplus(x):
    return jnp.maximum(x, 0.0) + jnp.log1p(jnp.exp(-jnp.abs(x)))


def _gelu_tanh(x):
    c = math.sqrt(2.0 / math.pi)
    return 0.5 * x * (1.0 + jnp.tanh(c * (x + 0.044715 * (x * x * x))))


def _dot(a, b):
    return jnp.dot(a, b, preferred_element_type=F32)


def _dot_nt(a, b):
    return lax.dot_general(a, b, (((1,), (1,)), ((), ())), preferred_element_type=F32)


def _split3(a):
    hi = a.astype(BF16)
    r1 = a - hi.astype(F32)
    mid = r1.astype(BF16)
    lo = (r1 - mid.astype(F32)).astype(BF16)
    return hi, mid, lo


def _dot_exact_lhs(m01, parts):
    hi, mid, lo = parts
    return _dot(m01, hi) + _dot(m01, mid) + _dot(m01, lo)


def _col_bcast(row):
    return jnp.broadcast_to(row, (LANES, LANES)).T


def _tri_masks():
    r = lax.broadcasted_iota(jnp.int32, (CHUNK, CHUNK), 0)
    c = lax.broadcasted_iota(jnp.int32, (CHUNK, CHUNK), 1)
    lower = r >= c
    tri = jnp.where(lower, 1.0, 0.0).astype(BF16)
    tri_t = jnp.where(r <= c, 1.0, 0.0).astype(BF16)
    return lower, tri, tri_t


def _staggered_row_specs(tm, n_tiles, n_steps, n_split):
    assert n_split < n_steps and tm % n_split == 0
    tq = tm // n_split

    def spec(q):
        def index(i, j):
            nxt = jnp.minimum(i + (j >= n_steps - n_split + q).astype(jnp.int32), n_tiles - 1)
            return (nxt * n_split + q, 0)
        return pl.BlockSpec((tq, D_MODEL), index)

    return [spec(q) for q in range(n_split)]


def _inproj_kernel(n_split, n_col_steps, side_work, *refs):
    x_parts = refs[:n_split]
    (nw_ref, w_ref, wdt_ref, pos_ref, freq_ref, phase_ref,
     proj_ref, dt_ref, hn_ref, cos_ref, sin_ref) = refs[n_split:]
    tq = x_parts[0].shape[0]
    half = RET_HEAD_DIM // 2

    @pl.when(pl.program_id(1) == 0)
    def _():
        for q, x_ref in enumerate(x_parts):
            hn_ref[q * tq:(q + 1) * tq, :] = _rms(x_ref[...], nw_ref[...]).astype(BF16)
        dt_ref[...] = _dot(hn_ref[...], wdt_ref[...])
        ang_t = freq_ref[...] * pos_ref[...].astype(F32) - phase_ref[...]
        cs = jnp.cos(ang_t).T
        sc = pltpu.roll(cs, half, axis=1)
        lane_lo = lax.broadcasted_iota(jnp.int32, cs.shape, 1) < half
        cos_ref[...] = jnp.where(lane_lo, cs, sc)
        sin_ref[...] = jnp.where(lane_lo, -sc, cs)

    tn = w_ref.shape[1]
    kinds = ((COL_G * RET_WIDTH, "plain"), (COL_XS * RET_WIDTH, "silu"), (N_MAIN, "plain"))

    def finish(block, kind):
        if kind == "silu":
            return _silu(block)
        if kind in ("q", "k"):
            block = block * cos_ref[...] + pltpu.roll(block, half, axis=1) * sin_ref[...]
            return block * RET_HEAD_DIM ** -0.5 if kind == "k" else block
        return block

    for step in range(n_col_steps):
        segments, col = [], step * tn
        while col < (step + 1) * tn:
            if col < RET_WIDTH:
                kind, end = "q", col + RET_HEAD_DIM
            elif col < OFF_K_END:
                kind, end = "k", col + RET_HEAD_DIM
            else:
                end, kind = next((e, k) for e, k in kinds if col < e)
            end = min(end, (step + 1) * tn)
            segments.append((col - step * tn, end - step * tn, kind))
            col = end

        @pl.when(pl.program_id(1) == step)
        def _():
            r = _dot(hn_ref[...], w_ref[...])
            for lo, hi, kind in segments:
                proj_ref[:, lo:hi] = finish(r[:, lo:hi], kind).astype(BF16)
            side_work()


def _in_projection(x2, norm_w, w_all, w_dt, positions, cast_weights,
                   tm=1024, tn=N_MAIN // 4, n_split=2):
    t = x2.shape[0]
    n_rows, n_cols = t // tm, N_MAIN // tn
    half = RET_HEAD_DIM // 2
    inv_freq = ROPE_BASE ** (-jnp.arange(half, dtype=F32) / half)
    freq = jnp.broadcast_to(jnp.concatenate([inv_freq, inv_freq])[:, None], (LANES, tm))
    phase = jnp.broadcast_to(
        jnp.concatenate([jnp.zeros((half,), F32), jnp.full((half,), math.pi / 2, F32)])[:, None],
        (LANES, tm))
    in_specs = _staggered_row_specs(tm, n_rows, n_cols, n_split) + [
        pl.BlockSpec((1, D_MODEL), lambda i, j: (0, 0)),
        pl.BlockSpec((D_MODEL, tn), lambda i, j: (0, j)),
        pl.BlockSpec((D_MODEL, LANES), lambda i, j: (0, 0)),
        pl.BlockSpec((None, 1, tm), lambda i, j: (i, 0, 0)),
        pl.BlockSpec((LANES, tm), lambda i, j: (0, 0)),
        pl.BlockSpec((LANES, tm), lambda i, j: (0, 0)),
    ]
    out_specs = [
        pl.BlockSpec((tm, tn), lambda i, j: (i, j)),
        pl.BlockSpec((tm, LANES), lambda i, j: (i, 0)),
    ]
    out_shape = [
        jax.ShapeDtypeStruct((t, N_MAIN), BF16),
        jax.ShapeDtypeStruct((t, LANES), F32),
    ]
    cast_in, cast_out, cast_shapes, jobs, cast_srcs = _cast_slab_specs(
        cast_weights, n_rows * n_cols, n_cols)
    outs = pl.pallas_call(
        functools.partial(_sweep_with_casts, functools.partial(_inproj_kernel, n_split, n_cols),
                          jobs, n_rows * n_cols, len(in_specs), len(out_specs)),
        grid=(n_rows, n_cols),
        in_specs=in_specs + cast_in,
        out_specs=out_specs + cast_out,
        out_shape=out_shape + cast_shapes,
        scratch_shapes=[pltpu.VMEM((tm, D_MODEL), BF16),
                        pltpu.VMEM((tm, LANES), F32), pltpu.VMEM((tm, LANES), F32)],
        compiler_params=_cparams(2),
        name="in_projection",
    )(*([x2] * n_split), norm_w, w_all, w_dt, positions.reshape(n_rows, 1, tm), freq, phase,
      *cast_srcs)
    return outs[:2], outs[2:]


def _conv5_silu(prev, cur, nxt, w_ref, b_ref, has_prev, has_next):
    zero = jnp.zeros_like(prev)
    depth = 2 * LANES
    ext = jnp.concatenate(
        [jnp.where(has_prev, prev, zero), cur, jnp.where(has_next, nxt, zero),
         jnp.zeros((depth - CHUNK - 2 * HALO, cur.shape[1]), cur.dtype)], axis=0)
    r = lax.broadcasted_iota(jnp.int32, (CHUNK, depth), 0)
    c = lax.broadcasted_iota(jnp.int32, (CHUNK, depth), 1)
    pad = SSD_CONV // 2
    acc = b_ref[...] + w_ref[pad:pad + 1, :] * cur.astype(F32)
    for j in range(SSD_CONV):
        if j != pad:
            shift = jnp.where(c == r + (HALO + j - pad), 1.0, 0.0).astype(BF16)
            acc = acc + w_ref[j:j + 1, :] * _dot(shift, ext)
    return _silu(acc)


def _expand_rows(w, tot_row, expand01):
    stacked = jnp.concatenate([w, jnp.broadcast_to(tot_row, (HALO, LANES))], axis=0)
    e = _dot(stacked.astype(BF16), expand01)
    return e[:CHUNK], e[CHUNK:CHUNK + 1]


def _state_increment(bm_f32, xw):
    parts = []
    for g in range(SSD_GROUPS):
        bm_t = bm_f32[:, g * SSD_STATE:(g + 1) * SSD_STATE].T.astype(BF16)
        parts.append(_dot(bm_t, xw[:, g * GROUP_W:(g + 1) * GROUP_W]))
    return jnp.concatenate(parts, axis=1)


def _cast_slab_specs(weights, n_steps, nc):
    in_specs, out_specs, out_shapes, jobs, flat = [], [], [], [], []
    for entry in weights:
        srcs = entry[:2] if isinstance(entry, tuple) else (entry,)
        tile = entry[2] if isinstance(entry, tuple) else 0
        n_rows, n_cols = srcs[0].shape
        rows = next(r for r in range(HALO, n_rows + 1, HALO)
                    if n_rows % r == 0 and n_rows // r <= n_steps)
        n_active = n_rows // rows

        def index(bi, i, _last=n_active - 1):
            return (jnp.minimum(bi * nc + i, _last), 0)

        for w in srcs:
            in_specs.append(pl.BlockSpec((rows, n_cols), index))
            flat.append(w)
        if tile:
            n_tiles = n_cols // tile
            out_specs.append(pl.BlockSpec((n_tiles, rows, 2 * tile),
                                          lambda bi, i, _index=index: (0,) + _index(bi, i)))
            out_shapes.append(jax.ShapeDtypeStruct((n_tiles, n_rows, 2 * tile), BF16))
        else:
            out_specs.append(pl.BlockSpec((rows, n_cols), index))
            out_shapes.append(jax.ShapeDtypeStruct((n_rows, n_cols), BF16))
        jobs.append((n_active, len(srcs), tile))
    return in_specs, out_specs, out_shapes, tuple(jobs), flat


def _sweep_with_casts(body, jobs, n_steps, n_in, n_out, *refs):
    n_src = sum(job[1] for job in jobs)
    ins, refs = refs[:n_in], refs[n_in:]
    srcs, refs = refs[:n_src], refs[n_src:]
    outs, refs = refs[:n_out], refs[n_out:]
    dsts, scratch = refs[:len(jobs)], refs[len(jobs):]
    step = pl.program_id(0) * pl.num_programs(1) + pl.program_id(1)

    def narrow(job_srcs, dst, tile):
        if tile:
            for t in range(dst.shape[0]):
                for k, src in enumerate(job_srcs):
                    dst[t, :, k * tile:(k + 1) * tile] = src[:, t * tile:(t + 1) * tile].astype(BF16)
        else:
            dst[...] = job_srcs[0][...].astype(BF16)

    every_step = []
    for (n_active, n_job_src, tile), dst in zip(jobs, dsts):
        job_srcs, srcs = srcs[:n_job_src], srcs[n_job_src:]
        if n_active == n_steps:
            every_step.append(functools.partial(narrow, job_srcs, dst, tile))
        else:
            pl.when(step < n_active)(functools.partial(narrow, job_srcs, dst, tile))

    def side_work():
        for f in every_step:
            f()

    body(side_work, *ins, *outs, *scratch)


def _prep_kernel(ret_dec_b, side_work,
                 k_ref, v_ref,
                 xs_ref, xsp_ref, xsn_ref, bc_ref, bcp_ref, bcn_ref, dt_ref,
                 cwx_ref, cbx_ref, cwb_ref, cbb_ref, dtbias_ref, arow2_ref,
                 kb_ref, expand_ref,
                 krt_ref, xbc_ref, rb_ref, sb_ref,
                 rb_state, sb_state):
    i = pl.program_id(1)
    nc = pl.num_programs(1)

    @pl.when(i == 0)
    def _():
        rb_state[...] = jnp.zeros_like(rb_state)
        sb_state[...] = jnp.zeros_like(sb_state)

    side_work()
    has_next = i > 0
    has_prev = i < nc - 1

    def rotate_head(h):
        sl = slice(h * RET_HEAD_DIM, (h + 1) * RET_HEAD_DIM)
        kt = k_ref[:, sl].astype(F32).T
        krt_ref[h] = kt.astype(BF16)
        rb_ref[h] = rb_state[h].astype(BF16)
        rb_state[h] = rb_state[h] * ret_dec_b[h] + _dot((kt * kb_ref[h:h + 1, :]).astype(BF16),
                                                        v_ref[:, sl])

    slab_w = 2 * LANES
    n_x_slabs = SSD_WIDTH // slab_w

    def conv_slab(c):
        if c < n_x_slabs:
            prev, cur, nxt, w_ref, b_ref = xsp_ref, xs_ref, xsn_ref, cwx_ref, cbx_ref
        else:
            c -= n_x_slabs
            prev, cur, nxt, w_ref, b_ref = bcp_ref, bc_ref, bcn_ref, cwb_ref, cbb_ref
        cs = slice(c * slab_w, (c + 1) * slab_w)
        return _conv5_silu(prev[:, cs], cur[:, cs], nxt[:, cs], w_ref.at[:, cs], b_ref.at[:, cs],
                           has_prev, has_next)

    slabs = []
    n_slabs = SSD_CONV_DIM // slab_w
    assert n_slabs <= RET_HEADS
    for h in range(RET_HEADS):
        if h < n_slabs:
            slabs.append(conv_slab(h))
        rotate_head(h)
    xs = jnp.concatenate(slabs[:n_x_slabs], axis=1)
    bc = jnp.concatenate(slabs[n_x_slabs:], axis=1)
    xbc_ref[:, :SSD_WIDTH] = xs.astype(BF16)
    xbc_ref[:, SSD_WIDTH:] = bc.astype(BF16)

    _, _, tri_t = _tri_masks()
    dt = _softplus(dt_ref[...] + dtbias_ref[...])
    rcs = _dot_exact_lhs(tri_t, _split3(dt * arow2_ref[...]))
    tot = rcs[0:1, :]
    w = jnp.exp2(tot - rcs) * dt
    wexp, cdec = _expand_rows(w, jnp.exp2(tot), expand_ref[...])
    xw = (xs * wexp).astype(BF16)
    ds = _state_increment(bc[:, :SSD_GROUPS * SSD_STATE], xw)
    sb_ref[...] = sb_state[...].astype(BF16)
    sb_state[...] = sb_state[...] * cdec + ds


def _prep_call(proj3, dt3, consts, ret_dec_b, cast_weights):
    b, l, _ = proj3.shape
    nc = l // CHUNK
    rows16 = l // HALO
    per16 = CHUNK // HALO

    def cix(i):
        return nc - 1 - i

    def col(cb):
        return lambda bi, i: (bi, cix(i), cb)

    def prev_halo(cb):
        return lambda bi, i: (bi, jnp.maximum(cix(i) * per16 - 1, 0), cb)

    def next_halo(cb):
        return lambda bi, i: (bi, jnp.minimum((cix(i) + 1) * per16, rows16 - 1), cb)

    def const(shape):
        return pl.BlockSpec(shape, lambda bi, i: (0,) * len(shape))

    in_specs = [
        pl.BlockSpec((None, CHUNK, RET_WIDTH), col(COL_K)),
        pl.BlockSpec((None, CHUNK, RET_WIDTH), col(COL_V)),
        pl.BlockSpec((None, CHUNK, SSD_WIDTH), col(COL_XS)),
        pl.BlockSpec((None, HALO, SSD_WIDTH), prev_halo(COL_XS)),
        pl.BlockSpec((None, HALO, SSD_WIDTH), next_halo(COL_XS)),
        pl.BlockSpec((None, CHUNK, SSD_BC), col(COL_BC)),
        pl.BlockSpec((None, HALO, SSD_BC), prev_halo(COL_BC)),
        pl.BlockSpec((None, HALO, SSD_BC), next_halo(COL_BC)),
        pl.BlockSpec((None, CHUNK, LANES), col(0)),
        const((SSD_CONV, SSD_WIDTH)), const((1, SSD_WIDTH)),
        const((SSD_CONV, SSD_BC)), const((1, SSD_BC)),
        const((1, LANES)), const((1, LANES)),
        const((RET_HEADS, LANES)),
        const((LANES, SSD_WIDTH)),
    ]
    out_specs = [
        pl.BlockSpec((None, None, RET_HEADS, RET_HEAD_DIM, CHUNK),
                     lambda bi, i: (bi, cix(i), 0, 0, 0)),
        pl.BlockSpec((None, CHUNK, SSD_CONV_DIM), col(0)),
        pl.BlockSpec((None, None, RET_HEADS, RET_HEAD_DIM, RET_HEAD_DIM),
                     lambda bi, i: (bi, cix(i), 0, 0, 0)),
        pl.BlockSpec((None, None, SSD_STATE, SSD_WIDTH), lambda bi, i: (bi, cix(i), 0, 0)),
    ]
    out_shape = [
        jax.ShapeDtypeStruct((b, nc, RET_HEADS, RET_HEAD_DIM, CHUNK), BF16),
        jax.ShapeDtypeStruct((b, l, SSD_CONV_DIM), BF16),
        jax.ShapeDtypeStruct((b, nc, RET_HEADS, RET_HEAD_DIM, RET_HEAD_DIM), BF16),
        jax.ShapeDtypeStruct((b, nc, SSD_STATE, SSD_WIDTH), BF16),
    ]
    scratch = [
        pltpu.VMEM((RET_HEADS, RET_HEAD_DIM, RET_HEAD_DIM), F32),
        pltpu.VMEM((SSD_STATE, SSD_WIDTH), F32),
    ]
    cast_in, cast_out, cast_shapes, active, cast_srcs = _cast_slab_specs(cast_weights, b * nc, nc)
    outs = pl.pallas_call(
        functools.partial(_sweep_with_casts, functools.partial(_prep_kernel, ret_dec_b),
                          active, b * nc, len(in_specs), len(out_specs)),
        grid=(b, nc),
        in_specs=in_specs + cast_in,
        out_specs=out_specs + cast_out,
        out_shape=out_shape + cast_shapes,
        scratch_shapes=scratch,
        compiler_params=_cparams(2),
        name="reverse_sweep",
    )(proj3, proj3, proj3, proj3, proj3, proj3, proj3, proj3, dt3,
      consts["conv_w_xs"], consts["conv_b_xs"], consts["conv_w_bc"], consts["conv_b_bc"],
      consts["dt_bias"], consts["a_row2"], consts["ret_kb"], consts["expand_b"], *cast_srcs)
    return outs[:len(out_specs)], outs[len(out_specs):]


def _mix_kernel(ret_dec_f, side_work,
                qr_ref, kr_ref, krt_ref, v_ref, g_ref, z_ref, xbc_ref, dt_ref, rb_ref, sb_ref,
                mask_ref, qf_ref, qb_ref, kf_ref, retnw_ref,
                dtbias_ref, arow2_ref, expand_ref, expand2_ref, dexp_ref, ssdnw_ref,
                out_ref,
                rf_state, sf_state):
    i = pl.program_id(1)

    @pl.when(i == 0)
    def _():
        rf_state[...] = jnp.zeros_like(rf_state)
        sf_state[...] = jnp.zeros_like(sf_state)

    side_work()
    lower, tri, tri_t = _tri_masks()
    dt = _softplus(dt_ref[...] + dtbias_ref[...])
    parts = _split3(dt * arow2_ref[...])
    lane = lax.broadcasted_iota(jnp.int32, (CHUNK, LANES), 1)
    prefix = _dot_exact_lhs(tri, parts)
    acs = jnp.where(lane < SSD_HEADS, prefix, _dot_exact_lhs(tri_t, parts))
    acs_t = acs.T
    src_t = (acs - jnp.log2(dt)).T
    edge = _dot(jnp.exp2(acs).astype(BF16), expand2_ref[...])
    xs16 = xbc_ref[:, :SSD_WIDTH]
    bm = xbc_ref[:, SSD_WIDTH:SSD_WIDTH + SSD_GROUPS * SSD_STATE]
    cm = xbc_ref[:, SSD_WIDTH + SSD_GROUPS * SSD_STATE:]
    lane_lo = lane < SSD_HEAD_DIM

    cbs, y_offs = [], []
    for g in range(SSD_GROUPS):
        gs = slice(g * SSD_STATE, (g + 1) * SSD_STATE)
        gw = slice(g * GROUP_W, (g + 1) * GROUP_W)
        gwb = slice(SSD_WIDTH + g * GROUP_W, SSD_WIDTH + (g + 1) * GROUP_W)
        cm_g = cm[:, gs]
        cbs.append(_dot_nt(cm_g, bm[:, gs]))
        y_offs.append(edge[:, gw] * _dot(cm_g, sf_state[:, gw].astype(BF16))
                      + edge[:, gwb] * _dot(cm_g, sb_ref[:, gw]))

    def decay_matrix(e):
        eb = SSD_HEADS + e
        dst = jnp.where(lower, _col_bcast(acs_t[e:e + 1, :]), _col_bcast(acs_t[eb:eb + 1, :]))
        src = jnp.where(lower, src_t[e:e + 1, :], src_t[eb:eb + 1, :])
        return (cbs[e // SSD_HEADS_PER_GROUP] * jnp.exp2(dst - src)).astype(BF16)

    ys = []
    for k in range(RET_HEADS):
        sl = slice(k * RET_HEAD_DIM, (k + 1) * RET_HEAD_DIM)
        ms = [decay_matrix(2 * k), decay_matrix(2 * k + 1)]
        qh = qr_ref[:, sl]
        kh = kr_ref[:, sl]
        vh = v_ref[:, sl]
        s = (_dot_nt(qh, kh) * mask_ref[k]).astype(BF16)
        lhs = jnp.concatenate([s, qh * qf_ref[k], qh * qb_ref[k]], axis=1)
        rhs = jnp.concatenate([vh, rf_state[k].astype(BF16), rb_ref[k]], axis=0)
        o = _dot(lhs, rhs)
        rf_state[k] = rf_state[k] * ret_dec_f[k] + _dot(krt_ref[k], vh * kf_ref[k])
        xs_pair = xs16[:, k * LANES:(k + 1) * LANES]
        prods = [_dot(m, xs_pair) for m in ms]
        o = _rms(o, retnw_ref[:, sl])
        out_ref[:, sl] = (g_ref[:, sl].astype(F32) * o).astype(BF16)
        g, q = divmod(k, SSD_HEADS_PER_GROUP // 2)
        ys.append(jnp.where(lane_lo, prods[0], prods[1]) + y_offs[g][:, q * LANES:(q + 1) * LANES])
    y = jnp.concatenate(ys, axis=1)

    xs = xs16.astype(F32)
    y = (y + dexp_ref[...] * xs) * z_ref[...].astype(F32)
    for g in range(SSD_GROUPS):
        gs = slice(g * GROUP_W, (g + 1) * GROUP_W)
        out_ref[:, RET_WIDTH + g * GROUP_W:RET_WIDTH + (g + 1) * GROUP_W] = _rms(
            y[:, gs], ssdnw_ref[:, gs]).astype(BF16)

    tot = prefix[CHUNK - 1:CHUNK, :]
    w = jnp.exp2(tot - prefix) * dt
    wexp, cdec = _expand_rows(w, jnp.exp2(tot), expand_ref[...])
    xw = (xs * wexp).astype(BF16)
    ds = _state_increment(bm.astype(F32), xw)
    sf_state[...] = sf_state[...] * cdec + ds


def _mix_call(proj3, krt, xbc, dt3, rb, sb, consts, ret_dec_f, cast_weights):
    b, l, _ = proj3.shape
    nc = l // CHUNK

    def col(cb):
        return lambda bi, i: (bi, i, cb)

    def const(shape):
        return pl.BlockSpec(shape, lambda bi, i: (0,) * len(shape))

    in_specs = [
        pl.BlockSpec((None, CHUNK, RET_WIDTH), col(COL_Q)),
        pl.BlockSpec((None, CHUNK, RET_WIDTH), col(COL_K)),
        pl.BlockSpec((None, None, RET_HEADS, RET_HEAD_DIM, CHUNK), lambda bi, i: (bi, i, 0, 0, 0)),
        pl.BlockSpec((None, CHUNK, RET_WIDTH), col(COL_V)),
        pl.BlockSpec((None, CHUNK, RET_WIDTH), col(COL_G)),
        pl.BlockSpec((None, CHUNK, SSD_WIDTH), col(COL_Z)),
        pl.BlockSpec((None, CHUNK, SSD_CONV_DIM), col(0)),
        pl.BlockSpec((None, CHUNK, LANES), col(0)),
        pl.BlockSpec((None, None, RET_HEADS, RET_HEAD_DIM, RET_HEAD_DIM),
                     lambda bi, i: (bi, i, 0, 0, 0)),
        pl.BlockSpec((None, None, SSD_STATE, SSD_WIDTH), lambda bi, i: (bi, i, 0, 0)),
        const((RET_HEADS, CHUNK, CHUNK)), const((RET_HEADS, CHUNK, LANES)),
        const((RET_HEADS, CHUNK, LANES)), const((RET_HEADS, CHUNK, LANES)), const((1, RET_WIDTH)),
        const((1, LANES)), const((1, LANES)), const((LANES, SSD_WIDTH)),
        const((LANES, 2 * SSD_WIDTH)), const((1, SSD_WIDTH)), const((1, SSD_WIDTH)),
    ]
    cast_in, cast_out, cast_shapes, active, cast_srcs = _cast_slab_specs(cast_weights, b * nc, nc)
    outs = pl.pallas_call(
        functools.partial(_sweep_with_casts, functools.partial(_mix_kernel, ret_dec_f),
                          active, b * nc, len(in_specs), 1),
        grid=(b, nc),
        in_specs=in_specs + cast_in,
        out_specs=[pl.BlockSpec((None, CHUNK, D_MODEL), col(0))] + cast_out,
        out_shape=[jax.ShapeDtypeStruct((b, l, D_MODEL), BF16)] + cast_shapes,
        scratch_shapes=[
            pltpu.VMEM((RET_HEADS, RET_HEAD_DIM, RET_HEAD_DIM), F32),
            pltpu.VMEM((SSD_STATE, SSD_WIDTH), F32),
        ],
        compiler_params=_cparams(2),
        name="forward_sweep",
    )(proj3, proj3, krt, proj3, proj3, proj3, xbc, dt3, rb, sb,
      consts["ret_mask"], consts["ret_qf"], consts["ret_qb"], consts["ret_kf"],
      consts["ret_norm_w"], consts["dt_bias"], consts["a_row2"], consts["expand_f"],
      consts["expand_fb"], consts["d_exp"], consts["ssd_norm_w"], *cast_srcs)
    return outs[0], outs[1:]


def _outproj_kernel(mix_ref, w_ref, x_ref, nw_ref, h_ref, hn_ref):
    h = x_ref[...] + _dot(mix_ref[...], w_ref[...])
    h_ref[...] = h
    hn_ref[...] = _rms(h, nw_ref[...]).astype(BF16)


def _out_projection(mix2, w_out, x2, ffn_norm_w, tm=512):
    t = x2.shape[0]
    return pl.pallas_call(
        _outproj_kernel,
        grid=(t // tm,),
        in_specs=[
            pl.BlockSpec((tm, D_MODEL), lambda i: (i, 0)),
            pl.BlockSpec((D_MODEL, D_MODEL), lambda i: (0, 0)),
            pl.BlockSpec((tm, D_MODEL), lambda i: (i, 0)),
            pl.BlockSpec((1, D_MODEL), lambda i: (0, 0)),
        ],
        out_specs=[
            pl.BlockSpec((tm, D_MODEL), lambda i: (i, 0)),
            pl.BlockSpec((tm, D_MODEL), lambda i: (i, 0)),
        ],
        out_shape=[
            jax.ShapeDtypeStruct((t, D_MODEL), F32),
            jax.ShapeDtypeStruct((t, D_MODEL), BF16),
        ],
        compiler_params=_cparams(1),
        name="out_projection",
    )(mix2, w_out, x2, ffn_norm_w)


def _ffn_kernel(tiles_per_seq, hn_ref, hp_ref, hx_ref, wgu_ref, cw_ref, cb_ref, wd_ref,
                o_ref, hbuf, gu_s):
    i = pl.program_id(0)
    j = pl.program_id(1)
    tm = hn_ref.shape[0]
    tf = cw_ref.shape[1]

    def column_step(first):
        if first:
            pos_in_seq = i % tiles_per_seq
            zero = jnp.zeros((HALO, D_MODEL), BF16)
            hbuf[0:HALO, :] = jnp.where(pos_in_seq > 0, hp_ref[...], zero)
            hbuf[HALO:HALO + tm, :] = hn_ref[...]
            hbuf[HALO + tm:, :] = jnp.where(pos_in_seq < tiles_per_seq - 1, hx_ref[...], zero)
        gu_s[...] = _dot(hbuf[...], wgu_ref[...])
        pad = FFN_CONV // 2
        gate = cb_ref[...]
        for t in range(FFN_CONV):
            o = HALO + t - pad
            gate = gate + cw_ref[t:t + 1, :] * gu_s[o:o + tm, :tf]
        act = (_gelu_tanh(gate) * gu_s[HALO:HALO + tm, tf:]).astype(BF16)
        if first:
            o_ref[...] = _dot(act, wd_ref[...])
        else:
            o_ref[...] += _dot(act, wd_ref[...])

    pl.when(j == 0)(functools.partial(column_step, True))
    pl.when(j > 0)(functools.partial(column_step, False))


def _ffn_call(hn2, wgu, conv_w, conv_b, wd, seq_len, tm=1024):
    t = hn2.shape[0]
    tf = wgu.shape[2] // 2
    per16 = tm // HALO
    rows16 = t // HALO
    return pl.pallas_call(
        functools.partial(_ffn_kernel, seq_len // tm),
        grid=(t // tm, D_FF // tf),
        in_specs=[
            pl.BlockSpec((tm, D_MODEL), lambda i, j: (i, 0)),
            pl.BlockSpec((HALO, D_MODEL), lambda i, j: (jnp.maximum(i * per16 - 1, 0), 0)),
            pl.BlockSpec((HALO, D_MODEL), lambda i, j: (jnp.minimum((i + 1) * per16, rows16 - 1), 0)),
            pl.BlockSpec((None, D_MODEL, 2 * tf), lambda i, j: (j, 0, 0)),
            pl.BlockSpec((FFN_CONV, tf), lambda i, j: (0, j)),
            pl.BlockSpec((1, tf), lambda i, j: (0, j)),
            pl.BlockSpec((tf, D_MODEL), lambda i, j: (j, 0)),
        ],
        out_specs=pl.BlockSpec((tm, D_MODEL), lambda i, j: (i, 0)),
        out_shape=jax.ShapeDtypeStruct((t, D_MODEL), F32),
        scratch_shapes=[
            pltpu.VMEM((tm + 2 * HALO, D_MODEL), BF16),
            pltpu.VMEM((tm + 2 * HALO, 2 * tf), F32),
        ],
        compiler_params=_cparams(2),
        name="conv_glu_ffn",
    )(hn2, hn2, hn2, wgu, conv_w, conv_b, wd)


def _ple_kernel(apply_final, h_ref, d_ref, p_ref, nw_ref, wg_ref, bg_ref, wp_ref, fw_ref, o_ref):
    h = h_ref[...] + d_ref[...]
    hn = _rms(h, nw_ref[...]).astype(BF16)
    gate = jax.nn.sigmoid(_dot(hn, wg_ref[...]) + bg_ref[...])
    h = h + gate * _dot(p_ref[...].astype(BF16), wp_ref[...])
    if apply_final:
        h = _rms(h, fw_ref[...])
    o_ref[...] = h


def _ple_call(h2, delta2, p2, norm_w, wg, bg, wp, final_w, apply_final, tm=512):
    t = h2.shape[0]
    return pl.pallas_call(
        functools.partial(_ple_kernel, apply_final),
        grid=(t // tm,),
        in_specs=[
            pl.BlockSpec((tm, D_MODEL), lambda i: (i, 0)),
            pl.BlockSpec((tm, D_MODEL), lambda i: (i, 0)),
            pl.BlockSpec((tm, D_PLE), lambda i: (i, 0)),
            pl.BlockSpec((1, D_MODEL), lambda i: (0, 0)),
            pl.BlockSpec((D_MODEL, D_MODEL), lambda i: (0, 0)),
            pl.BlockSpec((1, D_MODEL), lambda i: (0, 0)),
            pl.BlockSpec((D_PLE, D_MODEL), lambda i: (0, 0)),
            pl.BlockSpec((1, D_MODEL), lambda i: (0, 0)),
        ],
        out_specs=pl.BlockSpec((tm, D_MODEL), lambda i: (i, 0)),
        out_shape=jax.ShapeDtypeStruct((t, D_MODEL), F32),
        compiler_params=_cparams(1),
        name="ple_gate",
    )(h2, delta2, p2, norm_w, wg, bg, wp, final_w)


def _retention_tables():
    hh = np.arange(RET_HEADS, dtype=np.float64)
    lf = np.log1p(-np.exp2(-5.0 - hh))
    lb = np.log1p(-np.exp2(-5.5 - hh))
    idx = np.arange(CHUNK, dtype=np.float64)
    dist = idx[:, None] - idx[None, :]
    mask = np.where(dist >= 0, np.exp(lf[:, None, None] * np.abs(dist)),
                    np.exp(lb[:, None, None] * np.abs(dist)))
    ones = np.ones((1, 1, LANES))
    qf = np.exp(lf[:, None] * (idx + 1.0)[None, :])[:, :, None] * ones
    qb = np.exp(lb[:, None] * (CHUNK - idx)[None, :])[:, :, None] * ones
    kf = np.exp(lf[:, None] * (CHUNK - 1.0 - idx)[None, :])[:, :, None] * ones
    kb = np.exp(lb[:, None] * idx[None, :])
    dec_f = tuple(float(v) for v in np.exp(lf * CHUNK))
    dec_b = tuple(float(v) for v in np.exp(lb * CHUNK))
    f = lambda a: jnp.asarray(a, F32)
    h = lambda a: jnp.asarray(a, BF16)
    return dict(ret_mask=f(mask), ret_qf=h(qf), ret_qb=h(qb), ret_kf=h(kf), ret_kb=f(kb)), dec_f, dec_b


def _expand_matrix(first_row):
    e = np.zeros((LANES, SSD_WIDTH), np.float32)
    for h in range(SSD_HEADS):
        e[first_row + h, h * SSD_HEAD_DIM:(h + 1) * SSD_HEAD_DIM] = 1.0
    return e


def _pad_lanes(v):
    return jnp.pad(v.reshape(1, -1), ((0, 0), (0, LANES - v.size)))


def kernel(x, p, positions, norm_mix_w, w_in, ret_norm_w, ssd_conv_w, ssd_conv_b, ssd_dt_bias,
           ssd_a_log, ssd_d, ssd_norm_w, w_out, norm_ffn_w, ffn_w_gate, ffn_w_up, ffn_conv_w,
           ffn_conv_b, ffn_w_down, ple_norm_w, ple_w_gate, ple_b_gate, ple_w_proj, final_norm_w):
    b, l, _ = x.shape
    depth = w_in.shape[0]
    t = b * l
    nc = l // CHUNK
    row = lambda v: v.reshape(1, -1).astype(F32)

    tables, dec_f, dec_b = _retention_tables()
    exp_f, exp_b = _expand_matrix(0), _expand_matrix(SSD_HEADS)
    rot = dict(
        expand_f=jnp.asarray(exp_f, BF16),
        expand_b=jnp.asarray(exp_b, BF16),
        expand_fb=jnp.asarray(np.concatenate([exp_f, exp_b], axis=1), BF16),
    )
    h = x.reshape(t, D_MODEL)
    for i in range(depth):
        consts = dict(tables)
        consts.update(rot)
        consts.update(
            conv_w_xs=ssd_conv_w[i][:, :SSD_WIDTH], conv_b_xs=row(ssd_conv_b[i][:SSD_WIDTH]),
            conv_w_bc=ssd_conv_w[i][:, SSD_WIDTH:], conv_b_bc=row(ssd_conv_b[i][SSD_WIDTH:]),
            dt_bias=_pad_lanes(ssd_dt_bias[i]),
            a_row2=_pad_lanes(-jnp.exp(ssd_a_log[i].astype(F32)) * LOG2E),
            ret_norm_w=row(ret_norm_w[i]),
            d_exp=row(jnp.repeat(ssd_d[i], SSD_HEAD_DIM)),
            ssd_norm_w=row(ssd_norm_w[i]),
        )
        w_all = w_in[i].astype(BF16)
        w_dt = jnp.pad(w_all[:, N_MAIN:], ((0, 0), (0, LANES - N_DT)))

        (proj, dt), (wgu16, wd16, wo16, wpg16) = _in_projection(
            h, row(norm_mix_w[i]), w_all, w_dt, positions,
            [(ffn_w_gate[i], ffn_w_up[i], FFN_TILE), ffn_w_down[i], w_out[i], ple_w_gate[i]])
        proj3 = proj.reshape(b, l, N_MAIN)
        dt3 = dt.reshape(b, l, LANES)
        (krt, xbc, rb, sb), _ = _prep_call(proj3, dt3, consts, dec_b, [])
        mix, _ = _mix_call(proj3, krt, xbc, dt3, rb, sb, consts, dec_f, [])
        h, hn = _out_projection(mix.reshape(t, D_MODEL), wo16, h, row(norm_ffn_w[i]))
        delta = _ffn_call(hn, wgu16, ffn_conv_w[i], row(ffn_conv_b[i]), wd16, l)
        h = _ple_call(h, delta, p[i].reshape(t, D_PLE), row(ple_norm_w[i]), wpg16,
                      row(ple_b_gate[i]), ple_w_proj[i].astype(BF16), row(final_norm_w),
                      apply_final=(i == depth - 1))
    return h.reshape(b, l, D_MODEL)
```

```python
import functools
import math

import numpy as np
import jax
import jax.numpy as jnp
from jax import lax
from jax.experimental import pallas as pl
from jax.experimental.pallas import tpu as pltpu

F32 = jnp.float32
BF16 = jnp.bfloat16

D_MODEL = 2048
EPS = 1e-6
D_PLE = 256
RET_WIDTH = D_MODEL // 2
RET_HEAD_DIM = 128
RET_HEADS = RET_WIDTH // RET_HEAD_DIM
ROPE_BASE = 10000.0
SSD_WIDTH = D_MODEL - RET_WIDTH
SSD_HEAD_DIM = 64
SSD_HEADS = SSD_WIDTH // SSD_HEAD_DIM
SSD_GROUPS = 2
SSD_HEADS_PER_GROUP = SSD_HEADS // SSD_GROUPS
SSD_STATE = 128
SSD_CONV = 5
SSD_BC = 2 * SSD_GROUPS * SSD_STATE
SSD_CONV_DIM = SSD_WIDTH + SSD_BC
D_FF = (11 * D_MODEL) // 4
FFN_CONV = 3
N_MAIN = 4 * RET_WIDTH + SSD_WIDTH + SSD_CONV_DIM
N_DT = 2 * SSD_HEADS

CHUNK = 128
LANES = 128
HALO = 16
GROUP_W = SSD_WIDTH // SSD_GROUPS
FFN_TILE = 512

COL_Q, COL_K, COL_V, COL_G, COL_Z, COL_XS = 0, 1, 2, 3, 4, 5
COL_BC = (5 * RET_WIDTH + SSD_WIDTH) // SSD_BC
OFF_K_END = 2 * RET_WIDTH

VMEM_LIMIT = 60 * 1024 * 1024
LOG2E = math.log2(math.e)


def _cparams(n_axes):
    return pltpu.CompilerParams(dimension_semantics=("arbitrary",) * n_axes,
                                vmem_limit_bytes=VMEM_LIMIT)


def _rms(xf, w_row):
    ms = jnp.mean(xf * xf, axis=-1, keepdims=True)
    return xf * lax.rsqrt(ms + EPS) * w_row


def _silu(x):
    return x * jax.nn.sigmoid(x)


def _softplus(x):
    return jnp.maximum(x, 0.0) + jnp.log1p(jnp.exp(-jnp.abs(x)))


def _gelu_tanh(x):
    c = math.sqrt(2.0 / math.pi)
    return 0.5 * x * (1.0 + jnp.tanh(c * (x + 0.044715 * (x * x * x))))


def _dot(a, b):
    return jnp.dot(a, b, preferred_element_type=F32)


def _dot_nt(a, b):
    return lax.dot_general(a, b, (((1,), (1,)), ((), ())), preferred_element_type=F32)


def _split3(a):
    hi = a.astype(BF16)
    r1 = a - hi.astype(F32)
    mid = r1.astype(BF16)
    lo = (r1 - mid.astype(F32)).astype(BF16)
    return hi, mid, lo


def _dot_exact_lhs(m01, parts):
    hi, mid, lo = parts
    return _dot(m01, hi) + _dot(m01, mid) + _dot(m01, lo)


def _col_bcast(row):
    return jnp.broadcast_to(row, (LANES, LANES)).T


def _tri_masks():
    r = lax.broadcasted_iota(jnp.int32, (CHUNK, CHUNK), 0)
    c = lax.broadcasted_iota(jnp.int32, (CHUNK, CHUNK), 1)
    lower = r >= c
    tri = jnp.where(lower, 1.0, 0.0).astype(BF16)
    tri_t = jnp.where(r <= c, 1.0, 0.0).astype(BF16)
    return lower, tri, tri_t


def _staggered_row_specs(tm, n_tiles, n_steps, n_split):
    assert n_split < n_steps and tm % n_split == 0
    tq = tm // n_split

    def spec(q):
        def index(i, j):
            nxt = jnp.minimum(i + (j >= n_steps - n_split + q).astype(jnp.int32), n_tiles - 1)
            return (nxt * n_split + q, 0)
        return pl.BlockSpec((tq, D_MODEL), index)

    return [spec(q) for q in range(n_split)]


def _inproj_kernel(n_split, n_col_steps, side_work, *refs):
    x_parts = refs[:n_split]
    (nw_ref, w_ref, wdt_ref, pos_ref, freq_ref, phase_ref,
     proj_ref, dt_ref, hn_ref, cos_ref, sin_ref) = refs[n_split:]
    tq = x_parts[0].shape[0]
    half = RET_HEAD_DIM // 2

    @pl.when(pl.program_id(1) == 0)
    def _():
        for q, x_ref in enumerate(x_parts):
            hn_ref[q * tq:(q + 1) * tq, :] = _rms(x_ref[...], nw_ref[...]).astype(BF16)
        dt_ref[...] = _dot(hn_ref[...], wdt_ref[...])
        ang_t = freq_ref[...] * pos_ref[...].astype(F32) - phase_ref[...]
        cs = jnp.cos(ang_t).T
        sc = pltpu.roll(cs, half, axis=1)
        lane_lo = lax.broadcasted_iota(jnp.int32, cs.shape, 1) < half
        cos_ref[...] = jnp.where(lane_lo, cs, sc)
        sin_ref[...] = jnp.where(lane_lo, -sc, cs)

    tn = w_ref.shape[1]
    kinds = ((COL_G * RET_WIDTH, "plain"), (COL_XS * RET_WIDTH, "silu"), (N_MAIN, "plain"))

    def finish(block, kind):
        if kind == "silu":
            return _silu(block)
        if kind in ("q", "k"):
            block = block * cos_ref[...] + pltpu.roll(block, half, axis=1) * sin_ref[...]
            return block * RET_HEAD_DIM ** -0.5 if kind == "k" else block
        return block

    for step in range(n_col_steps):
        segments, col = [], step * tn
        while col < (step + 1) * tn:
            if col < RET_WIDTH:
                kind, end = "q", col + RET_HEAD_DIM
            elif col < OFF_K_END:
                kind, end = "k", col + RET_HEAD_DIM
            else:
                end, kind = next((e, k) for e, k in kinds if col < e)
            end = min(end, (step + 1) * tn)
            segments.append((col - step * tn, end - step * tn, kind))
            col = end

        @pl.when(pl.program_id(1) == step)
        def _():
            r = _dot(hn_ref[...], w_ref[...])
            for lo, hi, kind in segments:
                proj_ref[:, lo:hi] = finish(r[:, lo:hi], kind).astype(BF16)
            side_work()


def _in_projection(x2, norm_w, w_all, w_dt, positions, cast_weights,
                   tm=1024, tn=N_MAIN // 4, n_split=2):
    t = x2.shape[0]
    n_rows, n_cols = t // tm, N_MAIN // tn
    half = RET_HEAD_DIM // 2
    inv_freq = ROPE_BASE ** (-jnp.arange(half, dtype=F32) / half)
    freq = jnp.broadcast_to(jnp.concatenate([inv_freq, inv_freq])[:, None], (LANES, tm))
    phase = jnp.broadcast_to(
        jnp.concatenate([jnp.zeros((half,), F32), jnp.full((half,), math.pi / 2, F32)])[:, None],
        (LANES, tm))
    in_specs = _staggered_row_specs(tm, n_rows, n_cols, n_split) + [
        pl.BlockSpec((1, D_MODEL), lambda i, j: (0, 0)),
        pl.BlockSpec((D_MODEL, tn), lambda i, j: (0, j)),
        pl.BlockSpec((D_MODEL, LANES), lambda i, j: (0, 0)),
        pl.BlockSpec((None, 1, tm), lambda i, j: (i, 0, 0)),
        pl.BlockSpec((LANES, tm), lambda i, j: (0, 0)),
        pl.BlockSpec((LANES, tm), lambda i, j: (0, 0)),
    ]
    out_specs = [
        pl.BlockSpec((tm, tn), lambda i, j: (i, j)),
        pl.BlockSpec((tm, LANES), lambda i, j: (i, 0)),
    ]
    out_shape = [
        jax.ShapeDtypeStruct((t, N_MAIN), BF16),
        jax.ShapeDtypeStruct((t, LANES), F32),
    ]
    cast_in, cast_out, cast_shapes, jobs, cast_srcs = _cast_slab_specs(
        cast_weights, n_rows * n_cols, n_cols)
    outs = pl.pallas_call(
        functools.partial(_sweep_with_casts, functools.partial(_inproj_kernel, n_split, n_cols),
                          jobs, n_rows * n_cols, len(in_specs), len(out_specs)),
        grid=(n_rows, n_cols),
        in_specs=in_specs + cast_in,
        out_specs=out_specs + cast_out,
        out_shape=out_shape + cast_shapes,
        scratch_shapes=[pltpu.VMEM((tm, D_MODEL), BF16),
                        pltpu.VMEM((tm, LANES), F32), pltpu.VMEM((tm, LANES), F32)],
        compiler_params=_cparams(2),
        name="in_projection",
    )(*([x2] * n_split), norm_w, w_all, w_dt, positions.reshape(n_rows, 1, tm), freq, phase,
      *cast_srcs)
    return outs[:2], outs[2:]


def _conv5_silu(prev, cur, nxt, w_ref, b_ref, has_prev, has_next):
    zero = jnp.zeros_like(prev)
    depth = 2 * LANES
    ext = jnp.concatenate(
        [jnp.where(has_prev, prev, zero), cur, jnp.where(has_next, nxt, zero),
         jnp.zeros((depth - CHUNK - 2 * HALO, cur.shape[1]), cur.dtype)], axis=0)
    r = lax.broadcasted_iota(jnp.int32, (CHUNK, depth), 0)
    c = lax.broadcasted_iota(jnp.int32, (CHUNK, depth), 1)
    pad = SSD_CONV // 2
    acc = b_ref[...] + w_ref[pad:pad + 1, :] * cur.astype(F32)
    for j in range(SSD_CONV):
        if j != pad:
            shift = jnp.where(c == r + (HALO + j - pad), 1.0, 0.0).astype(BF16)
            acc = acc + w_ref[j:j + 1, :] * _dot(shift, ext)
    return _silu(acc)


def _expand_rows(w, tot_row, expand01):
    stacked = jnp.concatenate([w, jnp.broadcast_to(tot_row, (HALO, LANES))], axis=0)
    e = _dot(stacked.astype(BF16), expand01)
    return e[:CHUNK], e[CHUNK:CHUNK + 1]


def _state_increment(bm_f32, xw):
    parts = []
    for g in range(SSD_GROUPS):
        bm_t = bm_f32[:, g * SSD_STATE:(g + 1) * SSD_STATE].T.astype(BF16)
        parts.append(_dot(bm_t, xw[:, g * GROUP_W:(g + 1) * GROUP_W]))
    return jnp.concatenate(parts, axis=1)


def _cast_slab_specs(weights, n_steps, nc):
    in_specs, out_specs, out_shapes, jobs, flat = [], [], [], [], []
    for entry in weights:
        srcs = entry[:2] if isinstance(entry, tuple) else (entry,)
        tile = entry[2] if isinstance(entry, tuple) else 0
        n_rows, n_cols = srcs[0].shape
        rows = next(r for r in range(HALO, n_rows + 1, HALO)
                    if n_rows % r == 0 and n_rows // r <= n_steps)
        n_active = n_rows // rows

        def index(bi, i, _last=n_active - 1):
            return (jnp.minimum(bi * nc + i, _last), 0)

        for w in srcs:
            in_specs.append(pl.BlockSpec((rows, n_cols), index))
            flat.append(w)
        if tile:
            n_tiles = n_cols // tile
            out_specs.append(pl.BlockSpec((n_tiles, rows, 2 * tile),
                                          lambda bi, i, _index=index: (0,) + _index(bi, i)))
            out_shapes.append(jax.ShapeDtypeStruct((n_tiles, n_rows, 2 * tile), BF16))
        else:
            out_specs.append(pl.BlockSpec((rows, n_cols), index))
            out_shapes.append(jax.ShapeDtypeStruct((n_rows, n_cols), BF16))
        jobs.append((n_active, len(srcs), tile))
    return in_specs, out_specs, out_shapes, tuple(jobs), flat


def _sweep_with_casts(body, jobs, n_steps, n_in, n_out, *refs):
    n_src = sum(job[1] for job in jobs)
    ins, refs = refs[:n_in], refs[n_in:]
    srcs, refs = refs[:n_src], refs[n_src:]
    outs, refs = refs[:n_out], refs[n_out:]
    dsts, scratch = refs[:len(jobs)], refs[len(jobs):]
    step = pl.program_id(0) * pl.num_programs(1) + pl.program_id(1)

    def narrow(job_srcs, dst, tile):
        if tile:
            for t in range(dst.shape[0]):
                for k, src in enumerate(job_srcs):
                    dst[t, :, k * tile:(k + 1) * tile] = src[:, t * tile:(t + 1) * tile].astype(BF16)
        else:
            dst[...] = job_srcs[0][...].astype(BF16)

    every_step = []
    for (n_active, n_job_src, tile), dst in zip(jobs, dsts):
        job_srcs, srcs = srcs[:n_job_src], srcs[n_job_src:]
        if n_active == n_steps:
            every_step.append(functools.partial(narrow, job_srcs, dst, tile))
        else:
            pl.when(step < n_active)(functools.partial(narrow, job_srcs, dst, tile))

    def side_work():
        for f in every_step:
            f()

    body(side_work, *ins, *outs, *scratch)


def _prep_kernel(ret_dec_b, side_work,
                 k_ref, v_ref,
                 xs_ref, xsp_ref, xsn_ref, bc_ref, bcp_ref, bcn_ref, dt_ref,
                 cwx_ref, cbx_ref, cwb_ref, cbb_ref, dtbias_ref, arow2_ref,
                 kb_ref, expand_ref,
                 krt_ref, xbc_ref, rb_ref, sb_ref,
                 rb_state, sb_state):
    i = pl.program_id(1)
    nc = pl.num_programs(1)

    @pl.when(i == 0)
    def _():
        rb_state[...] = jnp.zeros_like(rb_state)
        sb_state[...] = jnp.zeros_like(sb_state)

    side_work()
    has_next = i > 0
    has_prev = i < nc - 1

    def retention_head(h):
        sl = slice(h * RET_HEAD_DIM, (h + 1) * RET_HEAD_DIM)
        kt = k_ref[:, sl].astype(F32).T
        krt_ref[h] = kt.astype(BF16)
        rb_ref[h] = rb_state[h].astype(BF16)
        rb_state[h] = rb_state[h] * ret_dec_b[h] + _dot((kt * kb_ref[h:h + 1, :]).astype(BF16),
                                                        v_ref[:, sl])

    slab_w = 2 * LANES
    n_x_slabs = SSD_WIDTH // slab_w

    def conv_slab(c):
        if c < n_x_slabs:
            prev, cur, nxt, w_ref, b_ref = xsp_ref, xs_ref, xsn_ref, cwx_ref, cbx_ref
        else:
            c -= n_x_slabs
            prev, cur, nxt, w_ref, b_ref = bcp_ref, bc_ref, bcn_ref, cwb_ref, cbb_ref
        cs = slice(c * slab_w, (c + 1) * slab_w)
        return _conv5_silu(prev[:, cs], cur[:, cs], nxt[:, cs], w_ref.at[:, cs], b_ref.at[:, cs],
                           has_prev, has_next)

    slabs = []
    n_slabs = SSD_CONV_DIM // slab_w
    assert n_slabs <= RET_HEADS
    for h in range(RET_HEADS):
        if h < n_slabs:
            slabs.append(conv_slab(h))
        retention_head(h)
    xs = jnp.concatenate(slabs[:n_x_slabs], axis=1)
    bc = jnp.concatenate(slabs[n_x_slabs:], axis=1)
    xbc_ref[:, :SSD_WIDTH] = xs.astype(BF16)
    xbc_ref[:, SSD_WIDTH:] = bc.astype(BF16)

    _, _, tri_t = _tri_masks()
    dt = _softplus(dt_ref[...] + dtbias_ref[...])
    rcs = _dot_exact_lhs(tri_t, _split3(dt * arow2_ref[...]))
    tot = rcs[0:1, :]
    w = jnp.exp2(tot - rcs) * dt
    wexp, cdec = _expand_rows(w, jnp.exp2(tot), expand_ref[...])
    xw = (xs * wexp).astype(BF16)
    ds = _state_increment(bc[:, :SSD_GROUPS * SSD_STATE], xw)
    sb_ref[...] = sb_state[...].astype(BF16)
    sb_state[...] = sb_state[...] * cdec + ds


def _prep_call(proj3, dt3, consts, ret_dec_b, cast_weights):
    b, l, _ = proj3.shape
    nc = l // CHUNK
    rows16 = l // HALO
    per16 = CHUNK // HALO

    def cix(i):
        return nc - 1 - i

    def col(cb):
        return lambda bi, i: (bi, cix(i), cb)

    def prev_halo(cb):
        return lambda bi, i: (bi, jnp.maximum(cix(i) * per16 - 1, 0), cb)

    def next_halo(cb):
        return lambda bi, i: (bi, jnp.minimum((cix(i) + 1) * per16, rows16 - 1), cb)

    def const(shape):
        return pl.BlockSpec(shape, lambda bi, i: (0,) * len(shape))

    in_specs = [
        pl.BlockSpec((None, CHUNK, RET_WIDTH), col(COL_K)),
        pl.BlockSpec((None, CHUNK, RET_WIDTH), col(COL_V)),
        pl.BlockSpec((None, CHUNK, SSD_WIDTH), col(COL_XS)),
        pl.BlockSpec((None, HALO, SSD_WIDTH), prev_halo(COL_XS)),
        pl.BlockSpec((None, HALO, SSD_WIDTH), next_halo(COL_XS)),
        pl.BlockSpec((None, CHUNK, SSD_BC), col(COL_BC)),
        pl.BlockSpec((None, HALO, SSD_BC), prev_halo(COL_BC)),
        pl.BlockSpec((None, HALO, SSD_BC), next_halo(COL_BC)),
        pl.BlockSpec((None, CHUNK, LANES), col(0)),
        const((SSD_CONV, SSD_WIDTH)), const((1, SSD_WIDTH)),
        const((SSD_CONV, SSD_BC)), const((1, SSD_BC)),
        const((1, LANES)), const((1, LANES)),
        const((RET_HEADS, LANES)),
        const((LANES, SSD_WIDTH)),
    ]
    out_specs = [
        pl.BlockSpec((None, None, RET_HEADS, RET_HEAD_DIM, CHUNK),
                     lambda bi, i: (bi, cix(i), 0, 0, 0)),
        pl.BlockSpec((None, CHUNK, SSD_CONV_DIM), col(0)),
        pl.BlockSpec((None, None, RET_HEADS, RET_HEAD_DIM, RET_HEAD_DIM),
                     lambda bi, i: (bi, cix(i), 0, 0, 0)),
        pl.BlockSpec((None, None, SSD_STATE, SSD_WIDTH), lambda bi, i: (bi, cix(i), 0, 0)),
    ]
    out_shape = [
        jax.ShapeDtypeStruct((b, nc, RET_HEADS, RET_HEAD_DIM, CHUNK), BF16),
        jax.ShapeDtypeStruct((b, l, SSD_CONV_DIM), BF16),
        jax.ShapeDtypeStruct((b, nc, RET_HEADS, RET_HEAD_DIM, RET_HEAD_DIM), BF16),
        jax.ShapeDtypeStruct((b, nc, SSD_STATE, SSD_WIDTH), BF16),
    ]
    scratch = [
        pltpu.VMEM((RET_HEADS, RET_HEAD_DIM, RET_HEAD_DIM), F32),
        pltpu.VMEM((SSD_STATE, SSD_WIDTH), F32),
    ]
    cast_in, cast_out, cast_shapes, active, cast_srcs = _cast_slab_specs(cast_weights, b * nc, nc)
    outs = pl.pallas_call(
        functools.partial(_sweep_with_casts, functools.partial(_prep_kernel, ret_dec_b),
                          active, b * nc, len(in_specs), len(out_specs)),
        grid=(b, nc),
        in_specs=in_specs + cast_in,
        out_specs=out_specs + cast_out,
        out_shape=out_shape + cast_shapes,
        scratch_shapes=scratch,
        compiler_params=_cparams(2),
        name="reverse_sweep",
    )(proj3, proj3, proj3, proj3, proj3, proj3, proj3, proj3, dt3,
      consts["conv_w_xs"], consts["conv_b_xs"], consts["conv_w_bc"], consts["conv_b_bc"],
      consts["dt_bias"], consts["a_row2"], consts["ret_kb"], consts["expand_b"], *cast_srcs)
    return outs[:len(out_specs)], outs[len(out_specs):]


def _mix_kernel(ret_dec_f, side_work,
                qr_ref, kr_ref, krt_ref, v_ref, g_ref, z_ref, xbc_ref, dt_ref, rb_ref, sb_ref,
                mask_ref, qf_ref, qb_ref, kf_ref, retnw_ref,
                dtbias_ref, arow2_ref, expand_ref, expand2_ref, dexp_ref, ssdnw_ref,
                out_ref,
                rf_state, sf_state):
    i = pl.program_id(1)

    @pl.when(i == 0)
    def _():
        rf_state[...] = jnp.zeros_like(rf_state)
        sf_state[...] = jnp.zeros_like(sf_state)

    side_work()
    lower, tri, tri_t = _tri_masks()
    dt = _softplus(dt_ref[...] + dtbias_ref[...])
    parts = _split3(dt * arow2_ref[...])
    lane = lax.broadcasted_iota(jnp.int32, (CHUNK, LANES), 1)
    prefix = _dot_exact_lhs(tri, parts)
    acs = jnp.where(lane < SSD_HEADS, prefix, _dot_exact_lhs(tri_t, parts))
    acs_t = acs.T
    src_t = (acs - jnp.log2(dt)).T
    edge = _dot(jnp.exp2(acs).astype(BF16), expand2_ref[...])
    xs16 = xbc_ref[:, :SSD_WIDTH]
    bm = xbc_ref[:, SSD_WIDTH:SSD_WIDTH + SSD_GROUPS * SSD_STATE]
    cm = xbc_ref[:, SSD_WIDTH + SSD_GROUPS * SSD_STATE:]
    lane_lo = lane < SSD_HEAD_DIM

    cbs, y_offs = [], []
    for g in range(SSD_GROUPS):
        gs = slice(g * SSD_STATE, (g + 1) * SSD_STATE)
        gw = slice(g * GROUP_W, (g + 1) * GROUP_W)
        gwb = slice(SSD_WIDTH + g * GROUP_W, SSD_WIDTH + (g + 1) * GROUP_W)
        cm_g = cm[:, gs]
        cbs.append(_dot_nt(cm_g, bm[:, gs]))
        y_offs.append(edge[:, gw] * _dot(cm_g, sf_state[:, gw].astype(BF16))
                      + edge[:, gwb] * _dot(cm_g, sb_ref[:, gw]))

    def decay_matrix(e):
        eb = SSD_HEADS + e
        dst = jnp.where(lower, _col_bcast(acs_t[e:e + 1, :]), _col_bcast(acs_t[eb:eb + 1, :]))
        src = jnp.where(lower, src_t[e:e + 1, :], src_t[eb:eb + 1, :])
        return (cbs[e // SSD_HEADS_PER_GROUP] * jnp.exp2(dst - src)).astype(BF16)

    ys = []
    for k in range(RET_HEADS):
        sl = slice(k * RET_HEAD_DIM, (k + 1) * RET_HEAD_DIM)
        ms = [decay_matrix(2 * k), decay_matrix(2 * k + 1)]
        qh = qr_ref[:, sl]
        kh = kr_ref[:, sl]
        vh = v_ref[:, sl]
        s = (_dot_nt(qh, kh) * mask_ref[k]).astype(BF16)
        lhs = jnp.concatenate([s, qh * qf_ref[k], qh * qb_ref[k]], axis=1)
        rhs = jnp.concatenate([vh, rf_state[k].astype(BF16), rb_ref[k]], axis=0)
        o = _dot(lhs, rhs)
        rf_state[k] = rf_state[k] * ret_dec_f[k] + _dot(krt_ref[k], vh * kf_ref[k])
        xs_pair = xs16[:, k * LANES:(k + 1) * LANES]
        prods = [_dot(m, xs_pair) for m in ms]
        o = _rms(o, retnw_ref[:, sl])
        out_ref[:, sl] = (g_ref[:, sl].astype(F32) * o).astype(BF16)
        g, q = divmod(k, SSD_HEADS_PER_GROUP // 2)
        ys.append(jnp.where(lane_lo, prods[0], prods[1]) + y_offs[g][:, q * LANES:(q + 1) * LANES])
    y = jnp.concatenate(ys, axis=1)

    xs = xs16.astype(F32)
    y = (y + dexp_ref[...] * xs) * z_ref[...].astype(F32)
    for g in range(SSD_GROUPS):
        gs = slice(g * GROUP_W, (g + 1) * GROUP_W)
        out_ref[:, RET_WIDTH + g * GROUP_W:RET_WIDTH + (g + 1) * GROUP_W] = _rms(
            y[:, gs], ssdnw_ref[:, gs]).astype(BF16)

    tot = prefix[CHUNK - 1:CHUNK, :]
    w = jnp.exp2(tot - prefix) * dt
    wexp, cdec = _expand_rows(w, jnp.exp2(tot), expand_ref[...])
    xw = (xs * wexp).astype(BF16)
    ds = _state_increment(bm.astype(F32), xw)
    sf_state[...] = sf_state[...] * cdec + ds


def _mix_call(proj3, krt, xbc, dt3, rb, sb, consts, ret_dec_f, cast_weights):
    b, l, _ = proj3.shape
    nc = l // CHUNK

    def col(cb):
        return lambda bi, i: (bi, i, cb)

    def const(shape):
        return pl.BlockSpec(shape, lambda bi, i: (0,) * len(shape))

    in_specs = [
        pl.BlockSpec((None, CHUNK, RET_WIDTH), col(COL_Q)),
        pl.BlockSpec((None, CHUNK, RET_WIDTH), col(COL_K)),
        pl.BlockSpec((None, None, RET_HEADS, RET_HEAD_DIM, CHUNK), lambda bi, i: (bi, i, 0, 0, 0)),
        pl.BlockSpec((None, CHUNK, RET_WIDTH), col(COL_V)),
        pl.BlockSpec((None, CHUNK, RET_WIDTH), col(COL_G)),
        pl.BlockSpec((None, CHUNK, SSD_WIDTH), col(COL_Z)),
        pl.BlockSpec((None, CHUNK, SSD_CONV_DIM), col(0)),
        pl.BlockSpec((None, CHUNK, LANES), col(0)),
        pl.BlockSpec((None, None, RET_HEADS, RET_HEAD_DIM, RET_HEAD_DIM),
                     lambda bi, i: (bi, i, 0, 0, 0)),
        pl.BlockSpec((None, None, SSD_STATE, SSD_WIDTH), lambda bi, i: (bi, i, 0, 0)),
        const((RET_HEADS, CHUNK, CHUNK)), const((RET_HEADS, CHUNK, LANES)),
        const((RET_HEADS, CHUNK, LANES)), const((RET_HEADS, CHUNK, LANES)), const((1, RET_WIDTH)),
        const((1, LANES)), const((1, LANES)), const((LANES, SSD_WIDTH)),
        const((LANES, 2 * SSD_WIDTH)), const((1, SSD_WIDTH)), const((1, SSD_WIDTH)),
    ]
    cast_in, cast_out, cast_shapes, active, cast_srcs = _cast_slab_specs(cast_weights, b * nc, nc)
    outs = pl.pallas_call(
        functools.partial(_sweep_with_casts, functools.partial(_mix_kernel, ret_dec_f),
                          active, b * nc, len(in_specs), 1),
        grid=(b, nc),
        in_specs=in_specs + cast_in,
        out_specs=[pl.BlockSpec((None, CHUNK, D_MODEL), col(0))] + cast_out,
        out_shape=[jax.ShapeDtypeStruct((b, l, D_MODEL), BF16)] + cast_shapes,
        scratch_shapes=[
            pltpu.VMEM((RET_HEADS, RET_HEAD_DIM, RET_HEAD_DIM), F32),
            pltpu.VMEM((SSD_STATE, SSD_WIDTH), F32),
        ],
        compiler_params=_cparams(2),
        name="forward_sweep",
    )(proj3, proj3, krt, proj3, proj3, proj3, xbc, dt3, rb, sb,
      consts["ret_mask"], consts["ret_qf"], consts["ret_qb"], consts["ret_kf"],
      consts["ret_norm_w"], consts["dt_bias"], consts["a_row2"], consts["expand_f"],
      consts["expand_fb"], consts["d_exp"], consts["ssd_norm_w"], *cast_srcs)
    return outs[0], outs[1:]


def _outproj_kernel(mix_ref, w_ref, x_ref, nw_ref, h_ref, hn_ref):
    h = x_ref[...] + _dot(mix_ref[...], w_ref[...])
    h_ref[...] = h
    hn_ref[...] = _rms(h, nw_ref[...]).astype(BF16)


def _out_projection(mix2, w_out, x2, ffn_norm_w, tm=512):
    t = x2.shape[0]
    return pl.pallas_call(
        _outproj_kernel,
        grid=(t // tm,),
        in_specs=[
            pl.BlockSpec((tm, D_MODEL), lambda i: (i, 0)),
            pl.BlockSpec((D_MODEL, D_MODEL), lambda i: (0, 0)),
            pl.BlockSpec((tm, D_MODEL), lambda i: (i, 0)),
            pl.BlockSpec((1, D_MODEL), lambda i: (0, 0)),
        ],
        out_specs=[
            pl.BlockSpec((tm, D_MODEL), lambda i: (i, 0)),
            pl.BlockSpec((tm, D_MODEL), lambda i: (i, 0)),
        ],
        out_shape=[
            jax.ShapeDtypeStruct((t, D_MODEL), F32),
            jax.ShapeDtypeStruct((t, D_MODEL), BF16),
        ],
        compiler_params=_cparams(1),
        name="out_projection",
    )(mix2, w_out, x2, ffn_norm_w)


def _ffn_kernel(tiles_per_seq, hn_ref, hp_ref, hx_ref, wgu_ref, cw_ref, cb_ref, wd_ref,
                o_ref, hbuf, gu_s):
    i = pl.program_id(0)
    j = pl.program_id(1)
    tm = hn_ref.shape[0]
    tf = cw_ref.shape[1]

    def column_step(first):
        if first:
            pos_in_seq = i % tiles_per_seq
            zero = jnp.zeros((HALO, D_MODEL), BF16)
            hbuf[0:HALO, :] = jnp.where(pos_in_seq > 0, hp_ref[...], zero)
            hbuf[HALO:HALO + tm, :] = hn_ref[...]
            hbuf[HALO + tm:, :] = jnp.where(pos_in_seq < tiles_per_seq - 1, hx_ref[...], zero)
        gu_s[...] = _dot(hbuf[...], wgu_ref[...])
        pad = FFN_CONV // 2
        gate = cb_ref[...]
        for t in range(FFN_CONV):
            o = HALO + t - pad
            gate = gate + cw_ref[t:t + 1, :] * gu_s[o:o + tm, :tf]
        act = (_gelu_tanh(gate) * gu_s[HALO:HALO + tm, tf:]).astype(BF16)
        if first:
            o_ref[...] = _dot(act, wd_ref[...])
        else:
            o_ref[...] += _dot(act, wd_ref[...])

    pl.when(j == 0)(functools.partial(column_step, True))
    pl.when(j > 0)(functools.partial(column_step, False))


def _ffn_call(hn2, wgu, conv_w, conv_b, wd, seq_len, tm=1024):
    t = hn2.shape[0]
    tf = wgu.shape[2] // 2
    per16 = tm // HALO
    rows16 = t // HALO
    return pl.pallas_call(
        functools.partial(_ffn_kernel, seq_len // tm),
        grid=(t // tm, D_FF // tf),
        in_specs=[
            pl.BlockSpec((tm, D_MODEL), lambda i, j: (i, 0)),
            pl.BlockSpec((HALO, D_MODEL), lambda i, j: (jnp.maximum(i * per16 - 1, 0), 0)),
            pl.BlockSpec((HALO, D_MODEL), lambda i, j: (jnp.minimum((i + 1) * per16, rows16 - 1), 0)),
            pl.BlockSpec((None, D_MODEL, 2 * tf), lambda i, j: (j, 0, 0)),
            pl.BlockSpec((FFN_CONV, tf), lambda i, j: (0, j)),
            pl.BlockSpec((1, tf), lambda i, j: (0, j)),
            pl.BlockSpec((tf, D_MODEL), lambda i, j: (j, 0)),
        ],
        out_specs=pl.BlockSpec((tm, D_MODEL), lambda i, j: (i, 0)),
        out_shape=jax.ShapeDtypeStruct((t, D_MODEL), F32),
        scratch_shapes=[
            pltpu.VMEM((tm + 2 * HALO, D_MODEL), BF16),
            pltpu.VMEM((tm + 2 * HALO, 2 * tf), F32),
        ],
        compiler_params=_cparams(2),
        name="conv_glu_ffn",
    )(hn2, hn2, hn2, wgu, conv_w, conv_b, wd)


def _ple_kernel(apply_final, h_ref, d_ref, p_ref, nw_ref, wg_ref, bg_ref, wp_ref, fw_ref, o_ref):
    h = h_ref[...] + d_ref[...]
    hn = _rms(h, nw_ref[...]).astype(BF16)
    gate = jax.nn.sigmoid(_dot(hn, wg_ref[...]) + bg_ref[...])
    h = h + gate * _dot(p_ref[...].astype(BF16), wp_ref[...])
    if apply_final:
        h = _rms(h, fw_ref[...])
    o_ref[...] = h


def _ple_call(h2, delta2, p2, norm_w, wg, bg, wp, final_w, apply_final, tm=512):
    t = h2.shape[0]
    return pl.pallas_call(
        functools.partial(_ple_kernel, apply_final),
        grid=(t // tm,),
        in_specs=[
            pl.BlockSpec((tm, D_MODEL), lambda i: (i, 0)),
            pl.BlockSpec((tm, D_MODEL), lambda i: (i, 0)),
            pl.BlockSpec((tm, D_PLE), lambda i: (i, 0)),
            pl.BlockSpec((1, D_MODEL), lambda i: (0, 0)),
            pl.BlockSpec((D_MODEL, D_MODEL), lambda i: (0, 0)),
            pl.BlockSpec((1, D_MODEL), lambda i: (0, 0)),
            pl.BlockSpec((D_PLE, D_MODEL), lambda i: (0, 0)),
            pl.BlockSpec((1, D_MODEL), lambda i: (0, 0)),
        ],
        out_specs=pl.BlockSpec((tm, D_MODEL), lambda i: (i, 0)),
        out_shape=jax.ShapeDtypeStruct((t, D_MODEL), F32),
        compiler_params=_cparams(1),
        name="ple_gate",
    )(h2, delta2, p2, norm_w, wg, bg, wp, final_w)


def _retention_tables():
    hh = np.arange(RET_HEADS, dtype=np.float64)
    lf = np.log1p(-np.exp2(-5.0 - hh))
    lb = np.log1p(-np.exp2(-5.5 - hh))
    idx = np.arange(CHUNK, dtype=np.float64)
    dist = idx[:, None] - idx[None, :]
    mask = np.where(dist >= 0, np.exp(lf[:, None, None] * np.abs(dist)),
                    np.exp(lb[:, None, None] * np.abs(dist)))
    ones = np.ones((1, 1, LANES))
    qf = np.exp(lf[:, None] * (idx + 1.0)[None, :])[:, :, None] * ones
    qb = np.exp(lb[:, None] * (CHUNK - idx)[None, :])[:, :, None] * ones
    kf = np.exp(lf[:, None] * (CHUNK - 1.0 - idx)[None, :])[:, :, None] * ones
    kb = np.exp(lb[:, None] * idx[None, :])
    dec_f = tuple(float(v) for v in np.exp(lf * CHUNK))
    dec_b = tuple(float(v) for v in np.exp(lb * CHUNK))
    f = lambda a: jnp.asarray(a, F32)
    h = lambda a: jnp.asarray(a, BF16)
    return dict(ret_mask=f(mask), ret_qf=h(qf), ret_qb=h(qb), ret_kf=h(kf), ret_kb=f(kb)), dec_f, dec_b


def _expand_matrix(first_row):
    e = np.zeros((LANES, SSD_WIDTH), np.float32)
    for h in range(SSD_HEADS):
        e[first_row + h, h * SSD_HEAD_DIM:(h + 1) * SSD_HEAD_DIM] = 1.0
    return e


def _pad_lanes(v):
    return jnp.pad(v.reshape(1, -1), ((0, 0), (0, LANES - v.size)))


def kernel(x, p, positions, norm_mix_w, w_in, ret_norm_w, ssd_conv_w, ssd_conv_b, ssd_dt_bias,
           ssd_a_log, ssd_d, ssd_norm_w, w_out, norm_ffn_w, ffn_w_gate, ffn_w_up, ffn_conv_w,
           ffn_conv_b, ffn_w_down, ple_norm_w, ple_w_gate, ple_b_gate, ple_w_proj, final_norm_w):
    b, l, _ = x.shape
    depth = w_in.shape[0]
    t = b * l
    nc = l // CHUNK
    row = lambda v: v.reshape(1, -1).astype(F32)

    tables, dec_f, dec_b = _retention_tables()
    exp_f, exp_b = _expand_matrix(0), _expand_matrix(SSD_HEADS)
    rot = dict(
        expand_f=jnp.asarray(exp_f, BF16),
        expand_b=jnp.asarray(exp_b, BF16),
        expand_fb=jnp.asarray(np.concatenate([exp_f, exp_b], axis=1), BF16),
    )
    h = x.reshape(t, D_MODEL)
    for i in range(depth):
        consts = dict(tables)
        consts.update(rot)
        consts.update(
            conv_w_xs=ssd_conv_w[i][:, :SSD_WIDTH], conv_b_xs=row(ssd_conv_b[i][:SSD_WIDTH]),
            conv_w_bc=ssd_conv_w[i][:, SSD_WIDTH:], conv_b_bc=row(ssd_conv_b[i][SSD_WIDTH:]),
            dt_bias=_pad_lanes(ssd_dt_bias[i]),
            a_row2=_pad_lanes(-jnp.exp(ssd_a_log[i].astype(F32)) * LOG2E),
            ret_norm_w=row(ret_norm_w[i]),
            d_exp=row(jnp.repeat(ssd_d[i], SSD_HEAD_DIM)),
            ssd_norm_w=row(ssd_norm_w[i]),
        )
        w_all = w_in[i].astype(BF16)
        w_dt = jnp.pad(w_all[:, N_MAIN:], ((0, 0), (0, LANES - N_DT)))

        (proj, dt), (wgu16, wd16, wo16, wpg16) = _in_projection(
            h, row(norm_mix_w[i]), w_all, w_dt, positions,
            [(ffn_w_gate[i], ffn_w_up[i], FFN_TILE), ffn_w_down[i], w_out[i], ple_w_gate[i]])
        proj3 = proj.reshape(b, l, N_MAIN)
        dt3 = dt.reshape(b, l, LANES)
        (krt, xbc, rb, sb), _ = _prep_call(proj3, dt3, consts, dec_b, [])
        mix, _ = _mix_call(proj3, krt, xbc, dt3, rb, sb, consts, dec_f, [])
        h, hn = _out_projection(mix.reshape(t, D_MODEL), wo16, h, row(norm_ffn_w[i]))
        delta = _ffn_call(hn, wgu16, ffn_conv_w[i], row(ffn_conv_b[i]), wd16, l)
        h = _ple_call(h, delta, p[i].reshape(t, D_PLE), row(ple_norm_w[i]), wpg16,
                      row(ple_b_gate[i]), ple_w_proj[i].astype(BF16), row(final_norm_w),
                      apply_final=(i == depth - 1))
    return h.reshape(b, l, D_MODEL)
```

```python
import functools
import math

import numpy as np
import jax
import jax.numpy as jnp
from jax import lax
from jax.experimental import pallas as pl
from jax.experimental.pallas import tpu as pltpu

F32 = jnp.float32
BF16 = jnp.bfloat16

D_MODEL = 2048
EPS = 1e-6
D_PLE = 256
RET_WIDTH = D_MODEL // 2
RET_HEAD_DIM = 128
RET_HEADS = RET_WIDTH // RET_HEAD_DIM
ROPE_BASE = 10000.0
SSD_WIDTH = D_MODEL - RET_WIDTH
SSD_HEAD_DIM = 64
SSD_HEADS = SSD_WIDTH // SSD_HEAD_DIM
SSD_GROUPS = 2
SSD_HEADS_PER_GROUP = SSD_HEADS // SSD_GROUPS
SSD_STATE = 128
SSD_CONV = 5
SSD_BC = 2 * SSD_GROUPS * SSD_STATE
SSD_CONV_DIM = SSD_WIDTH + SSD_BC
D_FF = (11 * D_MODEL) // 4
FFN_CONV = 3
N_MAIN = 4 * RET_WIDTH + SSD_WIDTH + SSD_CONV_DIM
N_DT = 2 * SSD_HEADS

CHUNK = 128
LANES = 128
HALO = 16
GROUP_W = SSD_WIDTH // SSD_GROUPS
FFN_TILE = 512
MXU_WIDTH = 256
IN_COL_STEPS = 4
N_PAD = -(-(N_MAIN + LANES) // (IN_COL_STEPS * MXU_WIDTH)) * (IN_COL_STEPS * MXU_WIDTH)

COL_Q, COL_K, COL_V, COL_G, COL_Z, COL_XS = 0, 1, 2, 3, 4, 5
COL_BC = (5 * RET_WIDTH + SSD_WIDTH) // SSD_BC
OFF_K_END = 2 * RET_WIDTH

VMEM_LIMIT = 60 * 1024 * 1024
LOG2E = math.log2(math.e)


def _cparams(n_axes):
    return pltpu.CompilerParams(dimension_semantics=("arbitrary",) * n_axes,
                                vmem_limit_bytes=VMEM_LIMIT)


def _rms(xf, w_row):
    ms = jnp.mean(xf * xf, axis=-1, keepdims=True)
    return xf * lax.rsqrt(ms + EPS) * w_row


def _silu(x):
    return x * jax.nn.sigmoid(x)


def _softplus(x):
    return jnp.maximum(x, 0.0) + jnp.log1p(jnp.exp(-jnp.abs(x)))


def _gelu_tanh(x):
    c = math.sqrt(2.0 / math.pi)
    return 0.5 * x * (1.0 + jnp.tanh(c * (x + 0.044715 * (x * x * x))))


def _dot(a, b):
    return jnp.dot(a, b, preferred_element_type=F32)


def _dot_nt(a, b):
    return lax.dot_general(a, b, (((1,), (1,)), ((), ())), preferred_element_type=F32)


def _split3(a):
    hi = a.astype(BF16)
    r1 = a - hi.astype(F32)
    mid = r1.astype(BF16)
    lo = (r1 - mid.astype(F32)).astype(BF16)
    return hi, mid, lo


def _dot_exact_lhs(m01, parts):
    hi, mid, lo = parts
    return _dot(m01, hi) + _dot(m01, mid) + _dot(m01, lo)


def _col_bcast(row):
    return jnp.broadcast_to(row, (LANES, LANES)).T


def _tri_masks():
    r = lax.broadcasted_iota(jnp.int32, (CHUNK, CHUNK), 0)
    c = lax.broadcasted_iota(jnp.int32, (CHUNK, CHUNK), 1)
    lower = r >= c
    tri = jnp.where(lower, 1.0, 0.0).astype(BF16)
    tri_t = jnp.where(r <= c, 1.0, 0.0).astype(BF16)
    return lower, tri, tri_t


def _staggered_row_specs(tm, n_tiles, n_steps, n_split):
    assert n_split < n_steps and tm % n_split == 0
    tq = tm // n_split

    def spec(q):
        def index(i, j):
            nxt = jnp.minimum(i + (j >= n_steps - n_split + q).astype(jnp.int32), n_tiles - 1)
            return (nxt * n_split + q, 0)
        return pl.BlockSpec((tq, D_MODEL), index)

    return [spec(q) for q in range(n_split)]


def _inproj_kernel(n_split, n_col_steps, side_work, *refs):
    x_parts = refs[:n_split]
    (nw_ref, w_ref, pos_ref, freq_ref, phase_ref,
     proj_ref, dt_ref, hn_ref, cos_ref, sin_ref) = refs[n_split:]
    tq = x_parts[0].shape[0]
    half = RET_HEAD_DIM // 2

    @pl.when(pl.program_id(1) == 0)
    def _():
        for q, x_ref in enumerate(x_parts):
            hn_ref[q * tq:(q + 1) * tq, :] = _rms(x_ref[...], nw_ref[...]).astype(BF16)
        ang_t = freq_ref[...] * pos_ref[...].astype(F32) - phase_ref[...]
        cs = jnp.cos(ang_t).T
        sc = pltpu.roll(cs, half, axis=1)
        lane_lo = lax.broadcasted_iota(jnp.int32, cs.shape, 1) < half
        cos_ref[...] = jnp.where(lane_lo, cs, sc)
        sin_ref[...] = jnp.where(lane_lo, -sc, cs)

    tn = w_ref.shape[1]
    kinds = ((COL_G * RET_WIDTH, "plain"), (COL_XS * RET_WIDTH, "silu"), (n_col_steps * tn, "plain"))

    def finish(block, kind):
        if kind == "silu":
            return _silu(block)
        if kind in ("q", "k"):
            block = block * cos_ref[...] + pltpu.roll(block, half, axis=1) * sin_ref[...]
            return block * RET_HEAD_DIM ** -0.5 if kind == "k" else block
        return block

    for step in range(n_col_steps):
        segments, col = [], step * tn
        while col < (step + 1) * tn:
            if col < RET_WIDTH:
                kind, end = "q", col + RET_HEAD_DIM
            elif col < OFF_K_END:
                kind, end = "k", col + RET_HEAD_DIM
            else:
                end, kind = next((e, k) for e, k in kinds if col < e)
            end = min(end, (step + 1) * tn)
            segments.append((col - step * tn, end - step * tn, kind))
            col = end

        @pl.when(pl.program_id(1) == step)
        def _():
            r = _dot(hn_ref[...], w_ref[...])
            for lo, hi, kind in segments:
                proj_ref[:, lo:hi] = finish(r[:, lo:hi], kind).astype(BF16)
            dt_lo = N_MAIN - step * tn
            if 0 <= dt_lo < tn:
                dt_ref[...] = r[:, dt_lo:dt_lo + LANES]
            side_work()


def _in_projection(x2, norm_w, w_pad, positions, cast_weights, tm=1024, n_split=2):
    t = x2.shape[0]
    tn = N_PAD // IN_COL_STEPS
    n_rows, n_cols = t // tm, IN_COL_STEPS
    half = RET_HEAD_DIM // 2
    inv_freq = ROPE_BASE ** (-jnp.arange(half, dtype=F32) / half)
    freq = jnp.broadcast_to(jnp.concatenate([inv_freq, inv_freq])[:, None], (LANES, tm))
    phase = jnp.broadcast_to(
        jnp.concatenate([jnp.zeros((half,), F32), jnp.full((half,), math.pi / 2, F32)])[:, None],
        (LANES, tm))
    in_specs = _staggered_row_specs(tm, n_rows, n_cols, n_split) + [
        pl.BlockSpec((1, D_MODEL), lambda i, j: (0, 0)),
        pl.BlockSpec((D_MODEL, tn), lambda i, j: (0, j)),
        pl.BlockSpec((None, 1, tm), lambda i, j: (i, 0, 0)),
        pl.BlockSpec((LANES, tm), lambda i, j: (0, 0)),
        pl.BlockSpec((LANES, tm), lambda i, j: (0, 0)),
    ]
    out_specs = [
        pl.BlockSpec((tm, tn), lambda i, j: (i, j)),
        pl.BlockSpec((tm, LANES), lambda i, j: (i, 0)),
    ]
    out_shape = [
        jax.ShapeDtypeStruct((t, N_PAD), BF16),
        jax.ShapeDtypeStruct((t, LANES), F32),
    ]
    cast_in, cast_out, cast_shapes, jobs, cast_srcs = _cast_slab_specs(
        cast_weights, n_rows * n_cols, n_cols)
    outs = pl.pallas_call(
        functools.partial(_sweep_with_casts, functools.partial(_inproj_kernel, n_split, n_cols),
                          jobs, n_rows * n_cols, len(in_specs), len(out_specs)),
        grid=(n_rows, n_cols),
        in_specs=in_specs + cast_in,
        out_specs=out_specs + cast_out,
        out_shape=out_shape + cast_shapes,
        scratch_shapes=[pltpu.VMEM((tm, D_MODEL), BF16),
                        pltpu.VMEM((tm, LANES), F32), pltpu.VMEM((tm, LANES), F32)],
        compiler_params=_cparams(2),
        name="in_projection",
    )(*([x2] * n_split), norm_w, w_pad, positions.reshape(n_rows, 1, tm), freq, phase,
      *cast_srcs)
    return outs[:2], outs[2:]


def _conv5_silu(prev, cur, nxt, w_ref, b_ref, has_prev, has_next):
    zero = jnp.zeros_like(prev)
    depth = 2 * LANES
    ext = jnp.concatenate(
        [jnp.where(has_prev, prev, zero), cur, jnp.where(has_next, nxt, zero),
         jnp.zeros((depth - CHUNK - 2 * HALO, cur.shape[1]), cur.dtype)], axis=0)
    r = lax.broadcasted_iota(jnp.int32, (CHUNK, depth), 0)
    c = lax.broadcasted_iota(jnp.int32, (CHUNK, depth), 1)
    pad = SSD_CONV // 2
    acc = b_ref[...] + w_ref[pad:pad + 1, :] * cur.astype(F32)
    for j in range(SSD_CONV):
        if j != pad:
            shift = jnp.where(c == r + (HALO + j - pad), 1.0, 0.0).astype(BF16)
            acc = acc + w_ref[j:j + 1, :] * _dot(shift, ext)
    return _silu(acc)


def _expand_rows(w, tot_row, expand01):
    stacked = jnp.concatenate([w, jnp.broadcast_to(tot_row, (HALO, LANES))], axis=0)
    e = _dot(stacked.astype(BF16), expand01)
    return e[:CHUNK], e[CHUNK:CHUNK + 1]


def _state_increment(bm_f32, xw):
    parts = []
    for g in range(SSD_GROUPS):
        bm_t = bm_f32[:, g * SSD_STATE:(g + 1) * SSD_STATE].T.astype(BF16)
        parts.append(_dot(bm_t, xw[:, g * GROUP_W:(g + 1) * GROUP_W]))
    return jnp.concatenate(parts, axis=1)


def _cast_slab_specs(weights, n_steps, nc):
    in_specs, out_specs, out_shapes, jobs, flat = [], [], [], [], []
    for entry in weights:
        srcs = entry[:2] if isinstance(entry, tuple) else (entry,)
        tile = entry[2] if isinstance(entry, tuple) else 0
        n_rows, n_cols = srcs[0].shape
        rows = next(r for r in range(HALO, n_rows + 1, HALO)
                    if n_rows % r == 0 and n_rows // r <= n_steps)
        n_active = n_rows // rows

        def index(bi, i, _last=n_active - 1):
            return (jnp.minimum(bi * nc + i, _last), 0)

        for w in srcs:
            in_specs.append(pl.BlockSpec((rows, n_cols), index))
            flat.append(w)
        if tile:
            n_tiles = n_cols // tile
            out_specs.append(pl.BlockSpec((n_tiles, rows, 2 * tile),
                                          lambda bi, i, _index=index: (0,) + _index(bi, i)))
            out_shapes.append(jax.ShapeDtypeStruct((n_tiles, n_rows, 2 * tile), BF16))
        else:
            out_specs.append(pl.BlockSpec((rows, n_cols), index))
            out_shapes.append(jax.ShapeDtypeStruct((n_rows, n_cols), BF16))
        jobs.append((n_active, len(srcs), tile))
    return in_specs, out_specs, out_shapes, tuple(jobs), flat


def _sweep_with_casts(body, jobs, n_steps, n_in, n_out, *refs):
    n_src = sum(job[1] for job in jobs)
    ins, refs = refs[:n_in], refs[n_in:]
    srcs, refs = refs[:n_src], refs[n_src:]
    outs, refs = refs[:n_out], refs[n_out:]
    dsts, scratch = refs[:len(jobs)], refs[len(jobs):]
    step = pl.program_id(0) * pl.num_programs(1) + pl.program_id(1)

    def narrow(job_srcs, dst, tile):
        if tile:
            for t in range(dst.shape[0]):
                for k, src in enumerate(job_srcs):
                    dst[t, :, k * tile:(k + 1) * tile] = src[:, t * tile:(t + 1) * tile].astype(BF16)
        else:
            dst[...] = job_srcs[0][...].astype(BF16)

    every_step = []
    for (n_active, n_job_src, tile), dst in zip(jobs, dsts):
        job_srcs, srcs = srcs[:n_job_src], srcs[n_job_src:]
        if n_active == n_steps:
            every_step.append(functools.partial(narrow, job_srcs, dst, tile))
        else:
            pl.when(step < n_active)(functools.partial(narrow, job_srcs, dst, tile))

    def side_work():
        for f in every_step:
            f()

    body(side_work, *ins, *outs, *scratch)


def _prep_kernel(ret_dec_b, side_work,
                 k_ref, v_ref,
                 xs_ref, xsp_ref, xsn_ref, bc_ref, bcp_ref, bcn_ref, dt_ref,
                 cwx_ref, cbx_ref, cwb_ref, cbb_ref, dtbias_ref, arow2_ref,
                 kb_ref, expand_ref,
                 krt_ref, xbc_ref, rb_ref, sb_ref,
                 rb_state, sb_state):
    i = pl.program_id(1)
    nc = pl.num_programs(1)

    @pl.when(i == 0)
    def _():
        rb_state[...] = jnp.zeros_like(rb_state)
        sb_state[...] = jnp.zeros_like(sb_state)

    side_work()
    has_next = i > 0
    has_prev = i < nc - 1

    def retention_head(h):
        sl = slice(h * RET_HEAD_DIM, (h + 1) * RET_HEAD_DIM)
        kt = k_ref[:, sl].astype(F32).T
        krt_ref[h] = kt.astype(BF16)
        rb_ref[h] = rb_state[h].astype(BF16)
        rb_state[h] = rb_state[h] * ret_dec_b[h] + _dot((kt * kb_ref[h:h + 1, :]).astype(BF16),
                                                        v_ref[:, sl])

    slab_w = 2 * LANES
    n_x_slabs = SSD_WIDTH // slab_w

    def conv_slab(c):
        if c < n_x_slabs:
            prev, cur, nxt, w_ref, b_ref = xsp_ref, xs_ref, xsn_ref, cwx_ref, cbx_ref
        else:
            c -= n_x_slabs
            prev, cur, nxt, w_ref, b_ref = bcp_ref, bc_ref, bcn_ref, cwb_ref, cbb_ref
        cs = slice(c * slab_w, (c + 1) * slab_w)
        return _conv5_silu(prev[:, cs], cur[:, cs], nxt[:, cs], w_ref.at[:, cs], b_ref.at[:, cs],
                           has_prev, has_next)

    slabs = []
    n_slabs = SSD_CONV_DIM // slab_w
    assert n_slabs <= RET_HEADS
    for h in range(RET_HEADS):
        if h < n_slabs:
            slabs.append(conv_slab(h))
        retention_head(h)
    xs = jnp.concatenate(slabs[:n_x_slabs], axis=1)
    bc = jnp.concatenate(slabs[n_x_slabs:], axis=1)
    xbc_ref[:, :SSD_WIDTH] = xs.astype(BF16)
    xbc_ref[:, SSD_WIDTH:] = bc.astype(BF16)

    _, _, tri_t = _tri_masks()
    dt = _softplus(dt_ref[...] + dtbias_ref[...])
    rcs = _dot_exact_lhs(tri_t, _split3(dt * arow2_ref[...]))
    tot = rcs[0:1, :]
    w = jnp.exp2(tot - rcs) * dt
    wexp, cdec = _expand_rows(w, jnp.exp2(tot), expand_ref[...])
    xw = (xs * wexp).astype(BF16)
    ds = _state_increment(bc[:, :SSD_GROUPS * SSD_STATE], xw)
    sb_ref[...] = sb_state[...].astype(BF16)
    sb_state[...] = sb_state[...] * cdec + ds


def _prep_call(proj3, dt3, consts, ret_dec_b, cast_weights):
    b, l, _ = proj3.shape
    nc = l // CHUNK
    rows16 = l // HALO
    per16 = CHUNK // HALO

    def cix(i):
        return nc - 1 - i

    def col(cb):
        return lambda bi, i: (bi, cix(i), cb)

    def prev_halo(cb):
        return lambda bi, i: (bi, jnp.maximum(cix(i) * per16 - 1, 0), cb)

    def next_halo(cb):
        return lambda bi, i: (bi, jnp.minimum((cix(i) + 1) * per16, rows16 - 1), cb)

    def const(shape):
        return pl.BlockSpec(shape, lambda bi, i: (0,) * len(shape))

    in_specs = [
        pl.BlockSpec((None, CHUNK, RET_WIDTH), col(COL_K)),
        pl.BlockSpec((None, CHUNK, RET_WIDTH), col(COL_V)),
        pl.BlockSpec((None, CHUNK, SSD_WIDTH), col(COL_XS)),
        pl.BlockSpec((None, HALO, SSD_WIDTH), prev_halo(COL_XS)),
        pl.BlockSpec((None, HALO, SSD_WIDTH), next_halo(COL_XS)),
        pl.BlockSpec((None, CHUNK, SSD_BC), col(COL_BC)),
        pl.BlockSpec((None, HALO, SSD_BC), prev_halo(COL_BC)),
        pl.BlockSpec((None, HALO, SSD_BC), next_halo(COL_BC)),
        pl.BlockSpec((None, CHUNK, LANES), col(0)),
        const((SSD_CONV, SSD_WIDTH)), const((1, SSD_WIDTH)),
        const((SSD_CONV, SSD_BC)), const((1, SSD_BC)),
        const((1, LANES)), const((1, LANES)),
        const((RET_HEADS, LANES)),
        const((LANES, SSD_WIDTH)),
    ]
    out_specs = [
        pl.BlockSpec((None, None, RET_HEADS, RET_HEAD_DIM, CHUNK),
                     lambda bi, i: (bi, cix(i), 0, 0, 0)),
        pl.BlockSpec((None, CHUNK, SSD_CONV_DIM), col(0)),
        pl.BlockSpec((None, None, RET_HEADS, RET_HEAD_DIM, RET_HEAD_DIM),
                     lambda bi, i: (bi, cix(i), 0, 0, 0)),
        pl.BlockSpec((None, None, SSD_STATE, SSD_WIDTH), lambda bi, i: (bi, cix(i), 0, 0)),
    ]
    out_shape = [
        jax.ShapeDtypeStruct((b, nc, RET_HEADS, RET_HEAD_DIM, CHUNK), BF16),
        jax.ShapeDtypeStruct((b, l, SSD_CONV_DIM), BF16),
        jax.ShapeDtypeStruct((b, nc, RET_HEADS, RET_HEAD_DIM, RET_HEAD_DIM), BF16),
        jax.ShapeDtypeStruct((b, nc, SSD_STATE, SSD_WIDTH), BF16),
    ]
    scratch = [
        pltpu.VMEM((RET_HEADS, RET_HEAD_DIM, RET_HEAD_DIM), F32),
        pltpu.VMEM((SSD_STATE, SSD_WIDTH), F32),
    ]
    cast_in, cast_out, cast_shapes, active, cast_srcs = _cast_slab_specs(cast_weights, b * nc, nc)
    outs = pl.pallas_call(
        functools.partial(_sweep_with_casts, functools.partial(_prep_kernel, ret_dec_b),
                          active, b * nc, len(in_specs), len(out_specs)),
        grid=(b, nc),
        in_specs=in_specs + cast_in,
        out_specs=out_specs + cast_out,
        out_shape=out_shape + cast_shapes,
        scratch_shapes=scratch,
        compiler_params=_cparams(2),
        name="reverse_sweep",
    )(proj3, proj3, proj3, proj3, proj3, proj3, proj3, proj3, dt3,
      consts["conv_w_xs"], consts["conv_b_xs"], consts["conv_w_bc"], consts["conv_b_bc"],
      consts["dt_bias"], consts["a_row2"], consts["ret_kb"], consts["expand_b"], *cast_srcs)
    return outs[:len(out_specs)], outs[len(out_specs):]


def _mix_kernel(ret_dec_f, side_work,
                qr_ref, kr_ref, krt_ref, v_ref, g_ref, z_ref, xbc_ref, dt_ref, rb_ref, sb_ref,
                mask_ref, qf_ref, qb_ref, kf_ref, retnw_ref,
                dtbias_ref, arow2_ref, expand_ref, expand2_ref, dexp_ref, ssdnw_ref,
                out_ref,
                rf_state, sf_state):
    i = pl.program_id(1)

    @pl.when(i == 0)
    def _():
        rf_state[...] = jnp.zeros_like(rf_state)
        sf_state[...] = jnp.zeros_like(sf_state)

    side_work()
    lower, tri, tri_t = _tri_masks()
    dt = _softplus(dt_ref[...] + dtbias_ref[...])
    parts = _split3(dt * arow2_ref[...])
    lane = lax.broadcasted_iota(jnp.int32, (CHUNK, LANES), 1)
    prefix = _dot_exact_lhs(tri, parts)
    acs = jnp.where(lane < SSD_HEADS, prefix, _dot_exact_lhs(tri_t, parts))
    acs_t = acs.T
    src_t = (acs - jnp.log2(dt)).T
    edge = _dot(jnp.exp2(acs).astype(BF16), expand2_ref[...])
    xs16 = xbc_ref[:, :SSD_WIDTH]
    bm = xbc_ref[:, SSD_WIDTH:SSD_WIDTH + SSD_GROUPS * SSD_STATE]
    cm = xbc_ref[:, SSD_WIDTH + SSD_GROUPS * SSD_STATE:]
    lane_lo = lane < SSD_HEAD_DIM

    cbs, y_offs = [], []
    for g in range(SSD_GROUPS):
        gs = slice(g * SSD_STATE, (g + 1) * SSD_STATE)
        gw = slice(g * GROUP_W, (g + 1) * GROUP_W)
        gwb = slice(SSD_WIDTH + g * GROUP_W, SSD_WIDTH + (g + 1) * GROUP_W)
        cm_g = cm[:, gs]
        cbs.append(_dot_nt(cm_g, bm[:, gs]))
        y_offs.append(edge[:, gw] * _dot(cm_g, sf_state[:, gw].astype(BF16))
                      + edge[:, gwb] * _dot(cm_g, sb_ref[:, gw]))

    def decay_matrix(e):
        eb = SSD_HEADS + e
        dst = jnp.where(lower, _col_bcast(acs_t[e:e + 1, :]), _col_bcast(acs_t[eb:eb + 1, :]))
        src = jnp.where(lower, src_t[e:e + 1, :], src_t[eb:eb + 1, :])
        return (cbs[e // SSD_HEADS_PER_GROUP] * jnp.exp2(dst - src)).astype(BF16)

    ys = []
    for k in range(RET_HEADS):
        sl = slice(k * RET_HEAD_DIM, (k + 1) * RET_HEAD_DIM)
        ms = [decay_matrix(2 * k), decay_matrix(2 * k + 1)]
        qh = qr_ref[:, sl]
        kh = kr_ref[:, sl]
        vh = v_ref[:, sl]
        s = (_dot_nt(qh, kh) * mask_ref[k]).astype(BF16)
        lhs = jnp.concatenate([s, qh * qf_ref[k], qh * qb_ref[k]], axis=1)
        rhs = jnp.concatenate([vh, rf_state[k].astype(BF16), rb_ref[k]], axis=0)
        o = _dot(lhs, rhs)
        rf_state[k] = rf_state[k] * ret_dec_f[k] + _dot(krt_ref[k], vh * kf_ref[k])
        xs_pair = xs16[:, k * LANES:(k + 1) * LANES]
        prods = [_dot(m, xs_pair) for m in ms]
        o = _rms(o, retnw_ref[:, sl])
        out_ref[:, sl] = (g_ref[:, sl].astype(F32) * o).astype(BF16)
        g, q = divmod(k, SSD_HEADS_PER_GROUP // 2)
        ys.append(jnp.where(lane_lo, prods[0], prods[1]) + y_offs[g][:, q * LANES:(q + 1) * LANES])
    y = jnp.concatenate(ys, axis=1)

    xs = xs16.astype(F32)
    y = (y + dexp_ref[...] * xs) * z_ref[...].astype(F32)
    for g in range(SSD_GROUPS):
        gs = slice(g * GROUP_W, (g + 1) * GROUP_W)
        out_ref[:, RET_WIDTH + g * GROUP_W:RET_WIDTH + (g + 1) * GROUP_W] = _rms(
            y[:, gs], ssdnw_ref[:, gs]).astype(BF16)

    tot = prefix[CHUNK - 1:CHUNK, :]
    w = jnp.exp2(tot - prefix) * dt
    wexp, cdec = _expand_rows(w, jnp.exp2(tot), expand_ref[...])
    xw = (xs * wexp).astype(BF16)
    ds = _state_increment(bm.astype(F32), xw)
    sf_state[...] = sf_state[...] * cdec + ds


def _mix_call(proj3, krt, xbc, dt3, rb, sb, consts, ret_dec_f, cast_weights):
    b, l, _ = proj3.shape
    nc = l // CHUNK

    def col(cb):
        return lambda bi, i: (bi, i, cb)

    def const(shape):
        return pl.BlockSpec(shape, lambda bi, i: (0,) * len(shape))

    in_specs = [
        pl.BlockSpec((None, CHUNK, RET_WIDTH), col(COL_Q)),
        pl.BlockSpec((None, CHUNK, RET_WIDTH), col(COL_K)),
        pl.BlockSpec((None, None, RET_HEADS, RET_HEAD_DIM, CHUNK), lambda bi, i: (bi, i, 0, 0, 0)),
        pl.BlockSpec((None, CHUNK, RET_WIDTH), col(COL_V)),
        pl.BlockSpec((None, CHUNK, RET_WIDTH), col(COL_G)),
        pl.BlockSpec((None, CHUNK, SSD_WIDTH), col(COL_Z)),
        pl.BlockSpec((None, CHUNK, SSD_CONV_DIM), col(0)),
        pl.BlockSpec((None, CHUNK, LANES), col(0)),
        pl.BlockSpec((None, None, RET_HEADS, RET_HEAD_DIM, RET_HEAD_DIM),
                     lambda bi, i: (bi, i, 0, 0, 0)),
        pl.BlockSpec((None, None, SSD_STATE, SSD_WIDTH), lambda bi, i: (bi, i, 0, 0)),
        const((RET_HEADS, CHUNK, CHUNK)), const((RET_HEADS, CHUNK, LANES)),
        const((RET_HEADS, CHUNK, LANES)), const((RET_HEADS, CHUNK, LANES)), const((1, RET_WIDTH)),
        const((1, LANES)), const((1, LANES)), const((LANES, SSD_WIDTH)),
        const((LANES, 2 * SSD_WIDTH)), const((1, SSD_WIDTH)), const((1, SSD_WIDTH)),
    ]
    cast_in, cast_out, cast_shapes, active, cast_srcs = _cast_slab_specs(cast_weights, b * nc, nc)
    outs = pl.pallas_call(
        functools.partial(_sweep_with_casts, functools.partial(_mix_kernel, ret_dec_f),
                          active, b * nc, len(in_specs), 1),
        grid=(b, nc),
        in_specs=in_specs + cast_in,
        out_specs=[pl.BlockSpec((None, CHUNK, D_MODEL), col(0))] + cast_out,
        out_shape=[jax.ShapeDtypeStruct((b, l, D_MODEL), BF16)] + cast_shapes,
        scratch_shapes=[
            pltpu.VMEM((RET_HEADS, RET_HEAD_DIM, RET_HEAD_DIM), F32),
            pltpu.VMEM((SSD_STATE, SSD_WIDTH), F32),
        ],
        compiler_params=_cparams(2),
        name="forward_sweep",
    )(proj3, proj3, krt, proj3, proj3, proj3, xbc, dt3, rb, sb,
      consts["ret_mask"], consts["ret_qf"], consts["ret_qb"], consts["ret_kf"],
      consts["ret_norm_w"], consts["dt_bias"], consts["a_row2"], consts["expand_f"],
      consts["expand_fb"], consts["d_exp"], consts["ssd_norm_w"], *cast_srcs)
    return outs[0], outs[1:]


def _outproj_kernel(mix_ref, w_ref, x_ref, nw_ref, h_ref, hn_ref):
    h = x_ref[...] + _dot(mix_ref[...], w_ref[...])
    h_ref[...] = h
    hn_ref[...] = _rms(h, nw_ref[...]).astype(BF16)


def _out_projection(mix2, w_out, x2, ffn_norm_w, tm=512):
    t = x2.shape[0]
    return pl.pallas_call(
        _outproj_kernel,
        grid=(t // tm,),
        in_specs=[
            pl.BlockSpec((tm, D_MODEL), lambda i: (i, 0)),
            pl.BlockSpec((D_MODEL, D_MODEL), lambda i: (0, 0)),
            pl.BlockSpec((tm, D_MODEL), lambda i: (i, 0)),
            pl.BlockSpec((1, D_MODEL), lambda i: (0, 0)),
        ],
        out_specs=[
            pl.BlockSpec((tm, D_MODEL), lambda i: (i, 0)),
            pl.BlockSpec((tm, D_MODEL), lambda i: (i, 0)),
        ],
        out_shape=[
            jax.ShapeDtypeStruct((t, D_MODEL), F32),
            jax.ShapeDtypeStruct((t, D_MODEL), BF16),
        ],
        compiler_params=_cparams(1),
        name="out_projection",
    )(mix2, w_out, x2, ffn_norm_w)


def _ffn_kernel(tiles_per_seq, hn_ref, hp_ref, hx_ref, wgu_ref, cw_ref, cb_ref, wd_ref,
                o_ref, hbuf, gu_s):
    i = pl.program_id(0)
    j = pl.program_id(1)
    tm = hn_ref.shape[0]
    tf = cw_ref.shape[1]

    def column_step(first):
        if first:
            pos_in_seq = i % tiles_per_seq
            zero = jnp.zeros((HALO, D_MODEL), BF16)
            hbuf[0:HALO, :] = jnp.where(pos_in_seq > 0, hp_ref[...], zero)
            hbuf[HALO:HALO + tm, :] = hn_ref[...]
            hbuf[HALO + tm:, :] = jnp.where(pos_in_seq < tiles_per_seq - 1, hx_ref[...], zero)
        gu_s[...] = _dot(hbuf[...], wgu_ref[...])
        pad = FFN_CONV // 2
        gate = cb_ref[...]
        for t in range(FFN_CONV):
            o = HALO + t - pad
            gate = gate + cw_ref[t:t + 1, :] * gu_s[o:o + tm, :tf]
        act = (_gelu_tanh(gate) * gu_s[HALO:HALO + tm, tf:]).astype(BF16)
        if first:
            o_ref[...] = _dot(act, wd_ref[...])
        else:
            o_ref[...] += _dot(act, wd_ref[...])

    pl.when(j == 0)(functools.partial(column_step, True))
    pl.when(j > 0)(functools.partial(column_step, False))


def _ffn_call(hn2, wgu, conv_w, conv_b, wd, seq_len, tm=1024):
    t = hn2.shape[0]
    tf = wgu.shape[2] // 2
    per16 = tm // HALO
    rows16 = t // HALO
    return pl.pallas_call(
        functools.partial(_ffn_kernel, seq_len // tm),
        grid=(t // tm, D_FF // tf),
        in_specs=[
            pl.BlockSpec((tm, D_MODEL), lambda i, j: (i, 0)),
            pl.BlockSpec((HALO, D_MODEL), lambda i, j: (jnp.maximum(i * per16 - 1, 0), 0)),
            pl.BlockSpec((HALO, D_MODEL), lambda i, j: (jnp.minimum((i + 1) * per16, rows16 - 1), 0)),
            pl.BlockSpec((None, D_MODEL, 2 * tf), lambda i, j: (j, 0, 0)),
            pl.BlockSpec((FFN_CONV, tf), lambda i, j: (0, j)),
            pl.BlockSpec((1, tf), lambda i, j: (0, j)),
            pl.BlockSpec((tf, D_MODEL), lambda i, j: (j, 0)),
        ],
        out_specs=pl.BlockSpec((tm, D_MODEL), lambda i, j: (i, 0)),
        out_shape=jax.ShapeDtypeStruct((t, D_MODEL), F32),
        scratch_shapes=[
            pltpu.VMEM((tm + 2 * HALO, D_MODEL), BF16),
            pltpu.VMEM((tm + 2 * HALO, 2 * tf), F32),
        ],
        compiler_params=_cparams(2),
        name="conv_glu_ffn",
    )(hn2, hn2, hn2, wgu, conv_w, conv_b, wd)


def _ple_kernel(apply_final, h_ref, d_ref, p_ref, nw_ref, wg_ref, bg_ref, wp_ref, fw_ref, o_ref):
    h = h_ref[...] + d_ref[...]
    hn = _rms(h, nw_ref[...]).astype(BF16)
    gate = jax.nn.sigmoid(_dot(hn, wg_ref[...]) + bg_ref[...])
    h = h + gate * _dot(p_ref[...].astype(BF16), wp_ref[...])
    if apply_final:
        h = _rms(h, fw_ref[...])
    o_ref[...] = h


def _ple_call(h2, delta2, p2, norm_w, wg, bg, wp, final_w, apply_final, tm=512):
    t = h2.shape[0]
    return pl.pallas_call(
        functools.partial(_ple_kernel, apply_final),
        grid=(t // tm,),
        in_specs=[
            pl.BlockSpec((tm, D_MODEL), lambda i: (i, 0)),
            pl.BlockSpec((tm, D_MODEL), lambda i: (i, 0)),
            pl.BlockSpec((tm, D_PLE), lambda i: (i, 0)),
            pl.BlockSpec((1, D_MODEL), lambda i: (0, 0)),
            pl.BlockSpec((D_MODEL, D_MODEL), lambda i: (0, 0)),
            pl.BlockSpec((1, D_MODEL), lambda i: (0, 0)),
            pl.BlockSpec((D_PLE, D_MODEL), lambda i: (0, 0)),
            pl.BlockSpec((1, D_MODEL), lambda i: (0, 0)),
        ],
        out_specs=pl.BlockSpec((tm, D_MODEL), lambda i: (i, 0)),
        out_shape=jax.ShapeDtypeStruct((t, D_MODEL), F32),
        compiler_params=_cparams(1),
        name="ple_gate",
    )(h2, delta2, p2, norm_w, wg, bg, wp, final_w)


def _retention_tables():
    hh = np.arange(RET_HEADS, dtype=np.float64)
    lf = np.log1p(-np.exp2(-5.0 - hh))
    lb = np.log1p(-np.exp2(-5.5 - hh))
    idx = np.arange(CHUNK, dtype=np.float64)
    dist = idx[:, None] - idx[None, :]
    mask = np.where(dist >= 0, np.exp(lf[:, None, None] * np.abs(dist)),
                    np.exp(lb[:, None, None] * np.abs(dist)))
    ones = np.ones((1, 1, LANES))
    qf = np.exp(lf[:, None] * (idx + 1.0)[None, :])[:, :, None] * ones
    qb = np.exp(lb[:, None] * (CHUNK - idx)[None, :])[:, :, None] * ones
    kf = np.exp(lf[:, None] * (CHUNK - 1.0 - idx)[None, :])[:, :, None] * ones
    kb = np.exp(lb[:, None] * idx[None, :])
    dec_f = tuple(float(v) for v in np.exp(lf * CHUNK))
    dec_b = tuple(float(v) for v in np.exp(lb * CHUNK))
    f = lambda a: jnp.asarray(a, F32)
    h = lambda a: jnp.asarray(a, BF16)
    return dict(ret_mask=f(mask), ret_qf=h(qf), ret_qb=h(qb), ret_kf=h(kf), ret_kb=f(kb)), dec_f, dec_b


def _expand_matrix(first_row):
    e = np.zeros((LANES, SSD_WIDTH), np.float32)
    for h in range(SSD_HEADS):
        e[first_row + h, h * SSD_HEAD_DIM:(h + 1) * SSD_HEAD_DIM] = 1.0
    return e


def _pad_lanes(v):
    return jnp.pad(v.reshape(1, -1), ((0, 0), (0, LANES - v.size)))


def kernel(x, p, positions, norm_mix_w, w_in, ret_norm_w, ssd_conv_w, ssd_conv_b, ssd_dt_bias,
           ssd_a_log, ssd_d, ssd_norm_w, w_out, norm_ffn_w, ffn_w_gate, ffn_w_up, ffn_conv_w,
           ffn_conv_b, ffn_w_down, ple_norm_w, ple_w_gate, ple_b_gate, ple_w_proj, final_norm_w):
    b, l, _ = x.shape
    depth = w_in.shape[0]
    t = b * l
    nc = l // CHUNK
    row = lambda v: v.reshape(1, -1).astype(F32)

    tables, dec_f, dec_b = _retention_tables()
    exp_f, exp_b = _expand_matrix(0), _expand_matrix(SSD_HEADS)
    rot = dict(
        expand_f=jnp.asarray(exp_f, BF16),
        expand_b=jnp.asarray(exp_b, BF16),
        expand_fb=jnp.asarray(np.concatenate([exp_f, exp_b], axis=1), BF16),
    )
    h = x.reshape(t, D_MODEL)
    for i in range(depth):
        consts = dict(tables)
        consts.update(rot)
        consts.update(
            conv_w_xs=ssd_conv_w[i][:, :SSD_WIDTH], conv_b_xs=row(ssd_conv_b[i][:SSD_WIDTH]),
            conv_w_bc=ssd_conv_w[i][:, SSD_WIDTH:], conv_b_bc=row(ssd_conv_b[i][SSD_WIDTH:]),
            dt_bias=_pad_lanes(ssd_dt_bias[i]),
            a_row2=_pad_lanes(-jnp.exp(ssd_a_log[i].astype(F32)) * LOG2E),
            ret_norm_w=row(ret_norm_w[i]),
            d_exp=row(jnp.repeat(ssd_d[i], SSD_HEAD_DIM)),
            ssd_norm_w=row(ssd_norm_w[i]),
        )
        w_pad = jnp.pad(w_in[i].astype(BF16), ((0, 0), (0, N_PAD - w_in.shape[2])))

        (proj, dt), (wgu16, wd16, wo16, wpg16) = _in_projection(
            h, row(norm_mix_w[i]), w_pad, positions,
            [(ffn_w_gate[i], ffn_w_up[i], FFN_TILE), ffn_w_down[i], w_out[i], ple_w_gate[i]])
        proj3 = proj.reshape(b, l, N_PAD)
        dt3 = dt.reshape(b, l, LANES)
        (krt, xbc, rb, sb), _ = _prep_call(proj3, dt3, consts, dec_b, [])
        mix, _ = _mix_call(proj3, krt, xbc, dt3, rb, sb, consts, dec_f, [])
        h, hn = _out_projection(mix.reshape(t, D_MODEL), wo16, h, row(norm_ffn_w[i]))
        delta = _ffn_call(hn, wgu16, ffn_conv_w[i], row(ffn_conv_b[i]), wd16, l)
        h = _ple_call(h, delta, p[i].reshape(t, D_PLE), row(ple_norm_w[i]), wpg16,
                      row(ple_b_gate[i]), ple_w_proj[i].astype(BF16), row(final_norm_w),
                      apply_final=(i == depth - 1))
    return h.reshape(b, l, D_MODEL)
```

```python
import functools
import math

import numpy as np
import jax
import jax.numpy as jnp
from jax import lax
from jax.experimental import pallas as pl
from jax.experimental.pallas import tpu as pltpu

F32 = jnp.float32
BF16 = jnp.bfloat16

D_MODEL = 2048
EPS = 1e-6
D_PLE = 256
RET_WIDTH = D_MODEL // 2
RET_HEAD_DIM = 128
RET_HEADS = RET_WIDTH // RET_HEAD_DIM
ROPE_BASE = 10000.0
SSD_WIDTH = D_MODEL - RET_WIDTH
SSD_HEAD_DIM = 64
SSD_HEADS = SSD_WIDTH // SSD_HEAD_DIM
SSD_GROUPS = 2
SSD_HEADS_PER_GROUP = SSD_HEADS // SSD_GROUPS
SSD_STATE = 128
SSD_CONV = 5
SSD_BC = 2 * SSD_GROUPS * SSD_STATE
SSD_CONV_DIM = SSD_WIDTH + SSD_BC
D_FF = (11 * D_MODEL) // 4
FFN_CONV = 3
N_MAIN = 4 * RET_WIDTH + SSD_WIDTH + SSD_CONV_DIM
N_DT = 2 * SSD_HEADS

CHUNK = 128
LANES = 128
HALO = 16
GROUP_W = SSD_WIDTH // SSD_GROUPS
FFN_TILE = 512

COL_Q, COL_K, COL_V, COL_G, COL_Z, COL_XS = 0, 1, 2, 3, 4, 5
COL_BC = (5 * RET_WIDTH + SSD_WIDTH) // SSD_BC
OFF_K_END = 2 * RET_WIDTH

VMEM_LIMIT = 60 * 1024 * 1024
LOG2E = math.log2(math.e)


def _cparams(n_axes):
    return pltpu.CompilerParams(dimension_semantics=("arbitrary",) * n_axes,
                                vmem_limit_bytes=VMEM_LIMIT)


def _rms(xf, w_row):
    ms = jnp.mean(xf * xf, axis=-1, keepdims=True)
    return xf * lax.rsqrt(ms + EPS) * w_row


def _silu(x):
    return x * jax.nn.sigmoid(x)


def _softplus(x):
    return jnp.maximum(x, 0.0) + jnp.log1p(jnp.exp(-jnp.abs(x)))


def _gelu_tanh(x):
    c = math.sqrt(2.0 / math.pi)
    return 0.5 * x * (1.0 + jnp.tanh(c * (x + 0.044715 * (x * x * x))))


def _dot(a, b):
    return jnp.dot(a, b, preferred_element_type=F32)


def _dot_nt(a, b):
    return lax.dot_general(a, b, (((1,), (1,)), ((), ())), preferred_element_type=F32)


def _split3(a):
    hi = a.astype(BF16)
    r1 = a - hi.astype(F32)
    mid = r1.astype(BF16)
    lo = (r1 - mid.astype(F32)).astype(BF16)
    return hi, mid, lo


def _dot_exact_lhs(m01, parts):
    hi, mid, lo = parts
    return _dot(m01, hi) + _dot(m01, mid) + _dot(m01, lo)


def _col_bcast(row):
    return jnp.broadcast_to(row, (LANES, LANES)).T


def _tri_masks():
    r = lax.broadcasted_iota(jnp.int32, (CHUNK, CHUNK), 0)
    c = lax.broadcasted_iota(jnp.int32, (CHUNK, CHUNK), 1)
    lower = r >= c
    tri = jnp.where(lower, 1.0, 0.0).astype(BF16)
    tri_t = jnp.where(r <= c, 1.0, 0.0).astype(BF16)
    return lower, tri, tri_t


def _staggered_row_specs(tm, n_tiles, n_steps, n_split):
    assert n_split < n_steps and tm % n_split == 0
    tq = tm // n_split

    def spec(q):
        def index(i, j):
            nxt = jnp.minimum(i + (j >= n_steps - n_split + q).astype(jnp.int32), n_tiles - 1)
            return (nxt * n_split + q, 0)
        return pl.BlockSpec((tq, D_MODEL), index)

    return [spec(q) for q in range(n_split)]


def _inproj_kernel(n_split, n_col_steps, side_work, *refs):
    x_parts = refs[:n_split]
    (nw_ref, w_ref, wdt_ref, pos_ref, freq_ref, phase_ref,
     proj_ref, dt_ref, hn_ref, cos_ref, sin_ref) = refs[n_split:]
    tq = x_parts[0].shape[0]
    half = RET_HEAD_DIM // 2

    @pl.when(pl.program_id(1) == 0)
    def _():
        for q, x_ref in enumerate(x_parts):
            hn_ref[q * tq:(q + 1) * tq, :] = _rms(x_ref[...], nw_ref[...]).astype(BF16)
        dt_ref[...] = _dot(hn_ref[...], wdt_ref[...])
        ang_t = freq_ref[...] * pos_ref[...].astype(F32) - phase_ref[...]
        cs = jnp.cos(ang_t).T
        sc = pltpu.roll(cs, half, axis=1)
        lane_lo = lax.broadcasted_iota(jnp.int32, cs.shape, 1) < half
        cos_ref[...] = jnp.where(lane_lo, cs, sc)
        sin_ref[...] = jnp.where(lane_lo, -sc, cs)

    tn = w_ref.shape[1]
    kinds = ((COL_G * RET_WIDTH, "plain"), (COL_XS * RET_WIDTH, "silu"), (N_MAIN, "plain"))

    def finish(block, kind):
        if kind == "silu":
            return _silu(block)
        if kind in ("q", "k"):
            block = block * cos_ref[...] + pltpu.roll(block, half, axis=1) * sin_ref[...]
            return block * RET_HEAD_DIM ** -0.5 if kind == "k" else block
        return block

    for step in range(n_col_steps):
        segments, col = [], step * tn
        while col < (step + 1) * tn:
            if col < RET_WIDTH:
                kind, end = "q", col + RET_HEAD_DIM
            elif col < OFF_K_END:
                kind, end = "k", col + RET_HEAD_DIM
            else:
                end, kind = next((e, k) for e, k in kinds if col < e)
            end = min(end, (step + 1) * tn)
            segments.append((col - step * tn, end - step * tn, kind))
            col = end

        @pl.when(pl.program_id(1) == step)
        def _():
            r = _dot(hn_ref[...], w_ref[...])
            for lo, hi, kind in segments:
                proj_ref[:, lo:hi] = finish(r[:, lo:hi], kind).astype(BF16)
            side_work()


def _in_projection(x2, norm_w, w_all, w_dt, positions, cast_weights,
                   tm=1024, tn=N_MAIN // 4, n_split=2):
    t = x2.shape[0]
    n_rows, n_cols = t // tm, N_MAIN // tn
    half = RET_HEAD_DIM // 2
    inv_freq = ROPE_BASE ** (-jnp.arange(half, dtype=F32) / half)
    freq = jnp.broadcast_to(jnp.concatenate([inv_freq, inv_freq])[:, None], (LANES, tm))
    phase = jnp.broadcast_to(
        jnp.concatenate([jnp.zeros((half,), F32), jnp.full((half,), math.pi / 2, F32)])[:, None],
        (LANES, tm))
    in_specs = _staggered_row_specs(tm, n_rows, n_cols, n_split) + [
        pl.BlockSpec((1, D_MODEL), lambda i, j: (0, 0)),
        pl.BlockSpec((D_MODEL, tn), lambda i, j: (0, j)),
        pl.BlockSpec((D_MODEL, LANES), lambda i, j: (0, 0)),
        pl.BlockSpec((None, 1, tm), lambda i, j: (i, 0, 0)),
        pl.BlockSpec((LANES, tm), lambda i, j: (0, 0)),
        pl.BlockSpec((LANES, tm), lambda i, j: (0, 0)),
    ]
    out_specs = [
        pl.BlockSpec((tm, tn), lambda i, j: (i, j)),
        pl.BlockSpec((tm, LANES), lambda i, j: (i, 0)),
    ]
    out_shape = [
        jax.ShapeDtypeStruct((t, N_MAIN), BF16),
        jax.ShapeDtypeStruct((t, LANES), F32),
    ]
    cast_in, cast_out, cast_shapes, jobs, cast_srcs = _cast_slab_specs(
        cast_weights, n_rows * n_cols, n_cols)
    outs = pl.pallas_call(
        functools.partial(_sweep_with_casts, functools.partial(_inproj_kernel, n_split, n_cols),
                          jobs, n_rows * n_cols, len(in_specs), len(out_specs)),
        grid=(n_rows, n_cols),
        in_specs=in_specs + cast_in,
        out_specs=out_specs + cast_out,
        out_shape=out_shape + cast_shapes,
        scratch_shapes=[pltpu.VMEM((tm, D_MODEL), BF16),
                        pltpu.VMEM((tm, LANES), F32), pltpu.VMEM((tm, LANES), F32)],
        compiler_params=_cparams(2),
        name="in_projection",
    )(*([x2] * n_split), norm_w, w_all, w_dt, positions.reshape(n_rows, 1, tm), freq, phase,
      *cast_srcs)
    return outs[:2], outs[2:]


def _conv5_silu(prev, cur, nxt, w_ref, b_ref, has_prev, has_next):
    zero = jnp.zeros_like(prev)
    depth = 2 * LANES
    ext = jnp.concatenate(
        [jnp.where(has_prev, prev, zero), cur, jnp.where(has_next, nxt, zero),
         jnp.zeros((depth - CHUNK - 2 * HALO, cur.shape[1]), cur.dtype)], axis=0)
    r = lax.broadcasted_iota(jnp.int32, (CHUNK, depth), 0)
    c = lax.broadcasted_iota(jnp.int32, (CHUNK, depth), 1)
    pad = SSD_CONV // 2
    acc = b_ref[...] + w_ref[pad:pad + 1, :] * cur.astype(F32)
    for j in range(SSD_CONV):
        if j != pad:
            shift = jnp.where(c == r + (HALO + j - pad), 1.0, 0.0).astype(BF16)
            acc = acc + w_ref[j:j + 1, :] * _dot(shift, ext)
    return _silu(acc)


def _expand_rows(w, tot_row, expand01):
    stacked = jnp.concatenate([w, jnp.broadcast_to(tot_row, (HALO, LANES))], axis=0)
    e = _dot(stacked.astype(BF16), expand01)
    return e[:CHUNK], e[CHUNK:CHUNK + 1]


def _state_increment(bm_f32, xw):
    parts = []
    for g in range(SSD_GROUPS):
        bm_t = bm_f32[:, g * SSD_STATE:(g + 1) * SSD_STATE].T.astype(BF16)
        parts.append(_dot(bm_t, xw[:, g * GROUP_W:(g + 1) * GROUP_W]))
    return jnp.concatenate(parts, axis=1)


def _cast_slab_specs(weights, n_steps, nc):
    in_specs, out_specs, out_shapes, jobs, flat = [], [], [], [], []
    for entry in weights:
        srcs = entry[:2] if isinstance(entry, tuple) else (entry,)
        tile = entry[2] if isinstance(entry, tuple) else 0
        n_rows, n_cols = srcs[0].shape
        rows = next(r for r in range(HALO, n_rows + 1, HALO)
                    if n_rows % r == 0 and n_rows // r <= n_steps)
        n_active = n_rows // rows

        def index(bi, i, _last=n_active - 1):
            return (jnp.minimum(bi * nc + i, _last), 0)

        for w in srcs:
            in_specs.append(pl.BlockSpec((rows, n_cols), index))
            flat.append(w)
        if tile:
            n_tiles = n_cols // tile
            out_specs.append(pl.BlockSpec((n_tiles, rows, 2 * tile),
                                          lambda bi, i, _index=index: (0,) + _index(bi, i)))
            out_shapes.append(jax.ShapeDtypeStruct((n_tiles, n_rows, 2 * tile), BF16))
        else:
            out_specs.append(pl.BlockSpec((rows, n_cols), index))
            out_shapes.append(jax.ShapeDtypeStruct((n_rows, n_cols), BF16))
        jobs.append((n_active, len(srcs), tile))
    return in_specs, out_specs, out_shapes, tuple(jobs), flat


def _sweep_with_casts(body, jobs, n_steps, n_in, n_out, *refs):
    n_src = sum(job[1] for job in jobs)
    ins, refs = refs[:n_in], refs[n_in:]
    srcs, refs = refs[:n_src], refs[n_src:]
    outs, refs = refs[:n_out], refs[n_out:]
    dsts, scratch = refs[:len(jobs)], refs[len(jobs):]
    step = pl.program_id(0) * pl.num_programs(1) + pl.program_id(1)

    def narrow(job_srcs, dst, tile):
        if tile:
            for t in range(dst.shape[0]):
                for k, src in enumerate(job_srcs):
                    dst[t, :, k * tile:(k + 1) * tile] = src[:, t * tile:(t + 1) * tile].astype(BF16)
        else:
            dst[...] = job_srcs[0][...].astype(BF16)

    every_step = []
    for (n_active, n_job_src, tile), dst in zip(jobs, dsts):
        job_srcs, srcs = srcs[:n_job_src], srcs[n_job_src:]
        if n_active == n_steps:
            every_step.append(functools.partial(narrow, job_srcs, dst, tile))
        else:
            pl.when(step < n_active)(functools.partial(narrow, job_srcs, dst, tile))

    def side_work():
        for f in every_step:
            f()

    body(side_work, *ins, *outs, *scratch)


def _prep_kernel(ret_dec_b, side_work,
                 k_ref, v_ref,
                 xs_ref, xsp_ref, xsn_ref, bc_ref, bcp_ref, bcn_ref, dt_ref,
                 cwx_ref, cbx_ref, cwb_ref, cbb_ref, dtbias_ref, arow2_ref,
                 kb_ref, expand_ref,
                 krt_ref, xbc_ref, rb_ref, sb_ref,
                 rb_state, sb_state):
    i = pl.program_id(1)
    nc = pl.num_programs(1)

    @pl.when(i == 0)
    def _():
        rb_state[...] = jnp.zeros_like(rb_state)
        sb_state[...] = jnp.zeros_like(sb_state)

    side_work()
    has_next = i > 0
    has_prev = i < nc - 1

    def retention_head(h):
        sl = slice(h * RET_HEAD_DIM, (h + 1) * RET_HEAD_DIM)
        kt = k_ref[:, sl].astype(F32).T
        krt_ref[h] = kt.astype(BF16)
        rb_ref[h] = rb_state[h].astype(BF16)
        rb_state[h] = rb_state[h] * ret_dec_b[h] + _dot((kt * kb_ref[h:h + 1, :]).astype(BF16),
                                                        v_ref[:, sl])

    slab_w = 2 * LANES
    n_x_slabs = SSD_WIDTH // slab_w

    def conv_slab(c):
        if c < n_x_slabs:
            prev, cur, nxt, w_ref, b_ref = xsp_ref, xs_ref, xsn_ref, cwx_ref, cbx_ref
        else:
            c -= n_x_slabs
            prev, cur, nxt, w_ref, b_ref = bcp_ref, bc_ref, bcn_ref, cwb_ref, cbb_ref
        cs = slice(c * slab_w, (c + 1) * slab_w)
        return _conv5_silu(prev[:, cs], cur[:, cs], nxt[:, cs], w_ref.at[:, cs], b_ref.at[:, cs],
                           has_prev, has_next)

    slabs = []
    n_slabs = SSD_CONV_DIM // slab_w
    assert n_slabs <= RET_HEADS
    for h in range(RET_HEADS):
        if h < n_slabs:
            slabs.append(conv_slab(h))
        retention_head(h)
    xs = jnp.concatenate(slabs[:n_x_slabs], axis=1)
    bc = jnp.concatenate(slabs[n_x_slabs:], axis=1)
    xbc_ref[:, :SSD_WIDTH] = xs.astype(BF16)
    xbc_ref[:, SSD_WIDTH:] = bc.astype(BF16)

    _, _, tri_t = _tri_masks()
    dt = _softplus(dt_ref[...] + dtbias_ref[...])
    rcs = _dot_exact_lhs(tri_t, _split3(dt * arow2_ref[...]))
    tot = rcs[0:1, :]
    w = jnp.exp2(tot - rcs) * dt
    wexp, cdec = _expand_rows(w, jnp.exp2(tot), expand_ref[...])
    xw = (xs * wexp).astype(BF16)
    ds = _state_increment(bc[:, :SSD_GROUPS * SSD_STATE], xw)
    sb_ref[...] = sb_state[...].astype(BF16)
    sb_state[...] = sb_state[...] * cdec + ds


def _prep_call(proj3, dt3, consts, ret_dec_b, cast_weights):
    b, l, _ = proj3.shape
    nc = l // CHUNK
    rows16 = l // HALO
    per16 = CHUNK // HALO

    def cix(i):
        return nc - 1 - i

    def col(cb):
        return lambda bi, i: (bi, cix(i), cb)

    def prev_halo(cb):
        return lambda bi, i: (bi, jnp.maximum(cix(i) * per16 - 1, 0), cb)

    def next_halo(cb):
        return lambda bi, i: (bi, jnp.minimum((cix(i) + 1) * per16, rows16 - 1), cb)

    def const(shape):
        return pl.BlockSpec(shape, lambda bi, i: (0,) * len(shape))

    in_specs = [
        pl.BlockSpec((None, CHUNK, RET_WIDTH), col(COL_K)),
        pl.BlockSpec((None, CHUNK, RET_WIDTH), col(COL_V)),
        pl.BlockSpec((None, CHUNK, SSD_WIDTH), col(COL_XS)),
        pl.BlockSpec((None, HALO, SSD_WIDTH), prev_halo(COL_XS)),
        pl.BlockSpec((None, HALO, SSD_WIDTH), next_halo(COL_XS)),
        pl.BlockSpec((None, CHUNK, SSD_BC), col(COL_BC)),
        pl.BlockSpec((None, HALO, SSD_BC), prev_halo(COL_BC)),
        pl.BlockSpec((None, HALO, SSD_BC), next_halo(COL_BC)),
        pl.BlockSpec((None, CHUNK, LANES), col(0)),
        const((SSD_CONV, SSD_WIDTH)), const((1, SSD_WIDTH)),
        const((SSD_CONV, SSD_BC)), const((1, SSD_BC)),
        const((1, LANES)), const((1, LANES)),
        const((RET_HEADS, LANES)),
        const((LANES, SSD_WIDTH)),
    ]
    out_specs = [
        pl.BlockSpec((None, None, RET_HEADS, RET_HEAD_DIM, CHUNK),
                     lambda bi, i: (bi, cix(i), 0, 0, 0)),
        pl.BlockSpec((None, CHUNK, SSD_CONV_DIM), col(0)),
        pl.BlockSpec((None, None, RET_HEADS, RET_HEAD_DIM, RET_HEAD_DIM),
                     lambda bi, i: (bi, cix(i), 0, 0, 0)),
        pl.BlockSpec((None, None, SSD_STATE, SSD_WIDTH), lambda bi, i: (bi, cix(i), 0, 0)),
    ]
    out_shape = [
        jax.ShapeDtypeStruct((b, nc, RET_HEADS, RET_HEAD_DIM, CHUNK), BF16),
        jax.ShapeDtypeStruct((b, l, SSD_CONV_DIM), BF16),
        jax.ShapeDtypeStruct((b, nc, RET_HEADS, RET_HEAD_DIM, RET_HEAD_DIM), BF16),
        jax.ShapeDtypeStruct((b, nc, SSD_STATE, SSD_WIDTH), BF16),
    ]
    scratch = [
        pltpu.VMEM((RET_HEADS, RET_HEAD_DIM, RET_HEAD_DIM), F32),
        pltpu.VMEM((SSD_STATE, SSD_WIDTH), F32),
    ]
    cast_in, cast_out, cast_shapes, active, cast_srcs = _cast_slab_specs(cast_weights, b * nc, nc)
    outs = pl.pallas_call(
        functools.partial(_sweep_with_casts, functools.partial(_prep_kernel, ret_dec_b),
                          active, b * nc, len(in_specs), len(out_specs)),
        grid=(b, nc),
        in_specs=in_specs + cast_in,
        out_specs=out_specs + cast_out,
        out_shape=out_shape + cast_shapes,
        scratch_shapes=scratch,
        compiler_params=_cparams(2),
        name="reverse_sweep",
    )(proj3, proj3, proj3, proj3, proj3, proj3, proj3, proj3, dt3,
      consts["conv_w_xs"], consts["conv_b_xs"], consts["conv_w_bc"], consts["conv_b_bc"],
      consts["dt_bias"], consts["a_row2"], consts["ret_kb"], consts["expand_b"], *cast_srcs)
    return outs[:len(out_specs)], outs[len(out_specs):]


def _mix_kernel(ret_dec_f, side_work,
                qr_ref, kr_ref, krt_ref, v_ref, g_ref, z_ref, xbc_ref, dt_ref, rb_ref, sb_ref,
                mask_ref, qf_ref, qb_ref, kf_ref, retnw_ref,
                dtbias_ref, arow2_ref, expand_ref, expand2_ref, dexp_ref, ssdnw_ref,
                x_ref, wout_ref, ffnnw_ref,
                h_ref, hn_ref,
                rf_state, sf_state, out_ref):
    i = pl.program_id(1)

    @pl.when(i == 0)
    def _():
        rf_state[...] = jnp.zeros_like(rf_state)
        sf_state[...] = jnp.zeros_like(sf_state)

    side_work()
    lower, tri, tri_t = _tri_masks()
    dt = _softplus(dt_ref[...] + dtbias_ref[...])
    parts = _split3(dt * arow2_ref[...])
    lane = lax.broadcasted_iota(jnp.int32, (CHUNK, LANES), 1)
    prefix = _dot_exact_lhs(tri, parts)
    acs = jnp.where(lane < SSD_HEADS, prefix, _dot_exact_lhs(tri_t, parts))
    acs_t = acs.T
    src_t = (acs - jnp.log2(dt)).T
    edge = _dot(jnp.exp2(acs).astype(BF16), expand2_ref[...])
    xs16 = xbc_ref[:, :SSD_WIDTH]
    bm = xbc_ref[:, SSD_WIDTH:SSD_WIDTH + SSD_GROUPS * SSD_STATE]
    cm = xbc_ref[:, SSD_WIDTH + SSD_GROUPS * SSD_STATE:]
    lane_lo = lane < SSD_HEAD_DIM

    cbs, y_offs = [], []
    for g in range(SSD_GROUPS):
        gs = slice(g * SSD_STATE, (g + 1) * SSD_STATE)
        gw = slice(g * GROUP_W, (g + 1) * GROUP_W)
        gwb = slice(SSD_WIDTH + g * GROUP_W, SSD_WIDTH + (g + 1) * GROUP_W)
        cm_g = cm[:, gs]
        cbs.append(_dot_nt(cm_g, bm[:, gs]))
        y_offs.append(edge[:, gw] * _dot(cm_g, sf_state[:, gw].astype(BF16))
                      + edge[:, gwb] * _dot(cm_g, sb_ref[:, gw]))

    def decay_matrix(e):
        eb = SSD_HEADS + e
        dst = jnp.where(lower, _col_bcast(acs_t[e:e + 1, :]), _col_bcast(acs_t[eb:eb + 1, :]))
        src = jnp.where(lower, src_t[e:e + 1, :], src_t[eb:eb + 1, :])
        return (cbs[e // SSD_HEADS_PER_GROUP] * jnp.exp2(dst - src)).astype(BF16)

    ys = []
    for k in range(RET_HEADS):
        sl = slice(k * RET_HEAD_DIM, (k + 1) * RET_HEAD_DIM)
        ms = [decay_matrix(2 * k), decay_matrix(2 * k + 1)]
        qh = qr_ref[:, sl]
        kh = kr_ref[:, sl]
        vh = v_ref[:, sl]
        s = (_dot_nt(qh, kh) * mask_ref[k]).astype(BF16)
        lhs = jnp.concatenate([s, qh * qf_ref[k], qh * qb_ref[k]], axis=1)
        rhs = jnp.concatenate([vh, rf_state[k].astype(BF16), rb_ref[k]], axis=0)
        o = _dot(lhs, rhs)
        rf_state[k] = rf_state[k] * ret_dec_f[k] + _dot(krt_ref[k], vh * kf_ref[k])
        xs_pair = xs16[:, k * LANES:(k + 1) * LANES]
        prods = [_dot(m, xs_pair) for m in ms]
        o = _rms(o, retnw_ref[:, sl])
        out_ref[:, sl] = (g_ref[:, sl].astype(F32) * o).astype(BF16)
        g, q = divmod(k, SSD_HEADS_PER_GROUP // 2)
        ys.append(jnp.where(lane_lo, prods[0], prods[1]) + y_offs[g][:, q * LANES:(q + 1) * LANES])
    y = jnp.concatenate(ys, axis=1)

    h = x_ref[...] + _dot(out_ref[:, :RET_WIDTH], wout_ref[:RET_WIDTH, :])

    xs = xs16.astype(F32)
    y = (y + dexp_ref[...] * xs) * z_ref[...].astype(F32)
    for g in range(SSD_GROUPS):
        gs = slice(g * GROUP_W, (g + 1) * GROUP_W)
        out_ref[:, RET_WIDTH + g * GROUP_W:RET_WIDTH + (g + 1) * GROUP_W] = _rms(
            y[:, gs], ssdnw_ref[:, gs]).astype(BF16)

    h = h + _dot(out_ref[:, RET_WIDTH:], wout_ref[RET_WIDTH:, :])
    h_ref[...] = h
    hn_ref[...] = _rms(h, ffnnw_ref[...]).astype(BF16)

    tot = prefix[CHUNK - 1:CHUNK, :]
    w = jnp.exp2(tot - prefix) * dt
    wexp, cdec = _expand_rows(w, jnp.exp2(tot), expand_ref[...])
    xw = (xs * wexp).astype(BF16)
    ds = _state_increment(bm.astype(F32), xw)
    sf_state[...] = sf_state[...] * cdec + ds


def _mix_call(proj3, krt, xbc, dt3, rb, sb, consts, ret_dec_f, cast_weights, x3, w_out, ffn_norm_w):
    b, l, _ = proj3.shape
    nc = l // CHUNK

    def col(cb):
        return lambda bi, i: (bi, i, cb)

    def const(shape):
        return pl.BlockSpec(shape, lambda bi, i: (0,) * len(shape))

    in_specs = [
        pl.BlockSpec((None, CHUNK, RET_WIDTH), col(COL_Q)),
        pl.BlockSpec((None, CHUNK, RET_WIDTH), col(COL_K)),
        pl.BlockSpec((None, None, RET_HEADS, RET_HEAD_DIM, CHUNK), lambda bi, i: (bi, i, 0, 0, 0)),
        pl.BlockSpec((None, CHUNK, RET_WIDTH), col(COL_V)),
        pl.BlockSpec((None, CHUNK, RET_WIDTH), col(COL_G)),
        pl.BlockSpec((None, CHUNK, SSD_WIDTH), col(COL_Z)),
        pl.BlockSpec((None, CHUNK, SSD_CONV_DIM), col(0)),
        pl.BlockSpec((None, CHUNK, LANES), col(0)),
        pl.BlockSpec((None, None, RET_HEADS, RET_HEAD_DIM, RET_HEAD_DIM),
                     lambda bi, i: (bi, i, 0, 0, 0)),
        pl.BlockSpec((None, None, SSD_STATE, SSD_WIDTH), lambda bi, i: (bi, i, 0, 0)),
        const((RET_HEADS, CHUNK, CHUNK)), const((RET_HEADS, CHUNK, LANES)),
        const((RET_HEADS, CHUNK, LANES)), const((RET_HEADS, CHUNK, LANES)), const((1, RET_WIDTH)),
        const((1, LANES)), const((1, LANES)), const((LANES, SSD_WIDTH)),
        const((LANES, 2 * SSD_WIDTH)), const((1, SSD_WIDTH)), const((1, SSD_WIDTH)),
        pl.BlockSpec((None, CHUNK, D_MODEL), col(0)),
        const((D_MODEL, D_MODEL)), const((1, D_MODEL)),
    ]
    row_block = pl.BlockSpec((None, CHUNK, D_MODEL), col(0))
    cast_in, cast_out, cast_shapes, active, cast_srcs = _cast_slab_specs(cast_weights, b * nc, nc)
    outs = pl.pallas_call(
        functools.partial(_sweep_with_casts, functools.partial(_mix_kernel, ret_dec_f),
                          active, b * nc, len(in_specs), 2),
        grid=(b, nc),
        in_specs=in_specs + cast_in,
        out_specs=[row_block, row_block] + cast_out,
        out_shape=[jax.ShapeDtypeStruct((b, l, D_MODEL), F32),
                   jax.ShapeDtypeStruct((b, l, D_MODEL), BF16)] + cast_shapes,
        scratch_shapes=[
            pltpu.VMEM((RET_HEADS, RET_HEAD_DIM, RET_HEAD_DIM), F32),
            pltpu.VMEM((SSD_STATE, SSD_WIDTH), F32),
            pltpu.VMEM((CHUNK, D_MODEL), BF16),
        ],
        compiler_params=_cparams(2),
        name="forward_sweep",
    )(proj3, proj3, krt, proj3, proj3, proj3, xbc, dt3, rb, sb,
      consts["ret_mask"], consts["ret_qf"], consts["ret_qb"], consts["ret_kf"],
      consts["ret_norm_w"], consts["dt_bias"], consts["a_row2"], consts["expand_f"],
      consts["expand_fb"], consts["d_exp"], consts["ssd_norm_w"], x3, w_out, ffn_norm_w,
      *cast_srcs)
    return outs[:2], outs[2:]


def _outproj_kernel(mix_ref, w_ref, x_ref, nw_ref, h_ref, hn_ref):
    h = x_ref[...] + _dot(mix_ref[...], w_ref[...])
    h_ref[...] = h
    hn_ref[...] = _rms(h, nw_ref[...]).astype(BF16)


def _out_projection(mix2, w_out, x2, ffn_norm_w, tm=512):
    t = x2.shape[0]
    return pl.pallas_call(
        _outproj_kernel,
        grid=(t // tm,),
        in_specs=[
            pl.BlockSpec((tm, D_MODEL), lambda i: (i, 0)),
            pl.BlockSpec((D_MODEL, D_MODEL), lambda i: (0, 0)),
            pl.BlockSpec((tm, D_MODEL), lambda i: (i, 0)),
            pl.BlockSpec((1, D_MODEL), lambda i: (0, 0)),
        ],
        out_specs=[
            pl.BlockSpec((tm, D_MODEL), lambda i: (i, 0)),
            pl.BlockSpec((tm, D_MODEL), lambda i: (i, 0)),
        ],
        out_shape=[
            jax.ShapeDtypeStruct((t, D_MODEL), F32),
            jax.ShapeDtypeStruct((t, D_MODEL), BF16),
        ],
        compiler_params=_cparams(1),
        name="out_projection",
    )(mix2, w_out, x2, ffn_norm_w)


def _ffn_kernel(tiles_per_seq, hn_ref, hp_ref, hx_ref, wgu_ref, cw_ref, cb_ref, wd_ref,
                o_ref, hbuf, gu_s):
    i = pl.program_id(0)
    j = pl.program_id(1)
    tm = hn_ref.shape[0]
    tf = cw_ref.shape[1]

    def column_step(first):
        if first:
            pos_in_seq = i % tiles_per_seq
            zero = jnp.zeros((HALO, D_MODEL), BF16)
            hbuf[0:HALO, :] = jnp.where(pos_in_seq > 0, hp_ref[...], zero)
            hbuf[HALO:HALO + tm, :] = hn_ref[...]
            hbuf[HALO + tm:, :] = jnp.where(pos_in_seq < tiles_per_seq - 1, hx_ref[...], zero)
        gu_s[...] = _dot(hbuf[...], wgu_ref[...])
        pad = FFN_CONV // 2
        gate = cb_ref[...]
        for t in range(FFN_CONV):
            o = HALO + t - pad
            gate = gate + cw_ref[t:t + 1, :] * gu_s[o:o + tm, :tf]
        act = (_gelu_tanh(gate) * gu_s[HALO:HALO + tm, tf:]).astype(BF16)
        if first:
            o_ref[...] = _dot(act, wd_ref[...])
        else:
            o_ref[...] += _dot(act, wd_ref[...])

    pl.when(j == 0)(functools.partial(column_step, True))
    pl.when(j > 0)(functools.partial(column_step, False))


def _ffn_call(hn2, wgu, conv_w, conv_b, wd, seq_len, tm=1024):
    t = hn2.shape[0]
    tf = wgu.shape[2] // 2
    per16 = tm // HALO
    rows16 = t // HALO
    return pl.pallas_call(
        functools.partial(_ffn_kernel, seq_len // tm),
        grid=(t // tm, D_FF // tf),
        in_specs=[
            pl.BlockSpec((tm, D_MODEL), lambda i, j: (i, 0)),
            pl.BlockSpec((HALO, D_MODEL), lambda i, j: (jnp.maximum(i * per16 - 1, 0), 0)),
            pl.BlockSpec((HALO, D_MODEL), lambda i, j: (jnp.minimum((i + 1) * per16, rows16 - 1), 0)),
            pl.BlockSpec((None, D_MODEL, 2 * tf), lambda i, j: (j, 0, 0)),
            pl.BlockSpec((FFN_CONV, tf), lambda i, j: (0, j)),
            pl.BlockSpec((1, tf), lambda i, j: (0, j)),
            pl.BlockSpec((tf, D_MODEL), lambda i, j: (j, 0)),
        ],
        out_specs=pl.BlockSpec((tm, D_MODEL), lambda i, j: (i, 0)),
        out_shape=jax.ShapeDtypeStruct((t, D_MODEL), F32),
        scratch_shapes=[
            pltpu.VMEM((tm + 2 * HALO, D_MODEL), BF16),
            pltpu.VMEM((tm + 2 * HALO, 2 * tf), F32),
        ],
        compiler_params=_cparams(2),
        name="conv_glu_ffn",
    )(hn2, hn2, hn2, wgu, conv_w, conv_b, wd)


def _ple_kernel(apply_final, h_ref, d_ref, p_ref, nw_ref, wg_ref, bg_ref, wp_ref, fw_ref, o_ref):
    h = h_ref[...] + d_ref[...]
    hn = _rms(h, nw_ref[...]).astype(BF16)
    gate = jax.nn.sigmoid(_dot(hn, wg_ref[...]) + bg_ref[...])
    h = h + gate * _dot(p_ref[...].astype(BF16), wp_ref[...])
    if apply_final:
        h = _rms(h, fw_ref[...])
    o_ref[...] = h


def _ple_call(h2, delta2, p2, norm_w, wg, bg, wp, final_w, apply_final, tm=512):
    t = h2.shape[0]
    return pl.pallas_call(
        functools.partial(_ple_kernel, apply_final),
        grid=(t // tm,),
        in_specs=[
            pl.BlockSpec((tm, D_MODEL), lambda i: (i, 0)),
            pl.BlockSpec((tm, D_MODEL), lambda i: (i, 0)),
            pl.BlockSpec((tm, D_PLE), lambda i: (i, 0)),
            pl.BlockSpec((1, D_MODEL), lambda i: (0, 0)),
            pl.BlockSpec((D_MODEL, D_MODEL), lambda i: (0, 0)),
            pl.BlockSpec((1, D_MODEL), lambda i: (0, 0)),
            pl.BlockSpec((D_PLE, D_MODEL), lambda i: (0, 0)),
            pl.BlockSpec((1, D_MODEL), lambda i: (0, 0)),
        ],
        out_specs=pl.BlockSpec((tm, D_MODEL), lambda i: (i, 0)),
        out_shape=jax.ShapeDtypeStruct((t, D_MODEL), F32),
        compiler_params=_cparams(1),
        name="ple_gate",
    )(h2, delta2, p2, norm_w, wg, bg, wp, final_w)


def _retention_tables():
    hh = np.arange(RET_HEADS, dtype=np.float64)
    lf = np.log1p(-np.exp2(-5.0 - hh))
    lb = np.log1p(-np.exp2(-5.5 - hh))
    idx = np.arange(CHUNK, dtype=np.float64)
    dist = idx[:, None] - idx[None, :]
    mask = np.where(dist >= 0, np.exp(lf[:, None, None] * np.abs(dist)),
                    np.exp(lb[:, None, None] * np.abs(dist)))
    ones = np.ones((1, 1, LANES))
    qf = np.exp(lf[:, None] * (idx + 1.0)[None, :])[:, :, None] * ones
    qb = np.exp(lb[:, None] * (CHUNK - idx)[None, :])[:, :, None] * ones
    kf = np.exp(lf[:, None] * (CHUNK - 1.0 - idx)[None, :])[:, :, None] * ones
    kb = np.exp(lb[:, None] * idx[None, :])
    dec_f = tuple(float(v) for v in np.exp(lf * CHUNK))
    dec_b = tuple(float(v) for v in np.exp(lb * CHUNK))
    f = lambda a: jnp.asarray(a, F32)
    h = lambda a: jnp.asarray(a, BF16)
    return dict(ret_mask=f(mask), ret_qf=h(qf), ret_qb=h(qb), ret_kf=h(kf), ret_kb=f(kb)), dec_f, dec_b


def _expand_matrix(first_row):
    e = np.zeros((LANES, SSD_WIDTH), np.float32)
    for h in range(SSD_HEADS):
        e[first_row + h, h * SSD_HEAD_DIM:(h + 1) * SSD_HEAD_DIM] = 1.0
    return e


def _pad_lanes(v):
    return jnp.pad(v.reshape(1, -1), ((0, 0), (0, LANES - v.size)))


def kernel(x, p, positions, norm_mix_w, w_in, ret_norm_w, ssd_conv_w, ssd_conv_b, ssd_dt_bias,
           ssd_a_log, ssd_d, ssd_norm_w, w_out, norm_ffn_w, ffn_w_gate, ffn_w_up, ffn_conv_w,
           ffn_conv_b, ffn_w_down, ple_norm_w, ple_w_gate, ple_b_gate, ple_w_proj, final_norm_w):
    b, l, _ = x.shape
    depth = w_in.shape[0]
    t = b * l
    nc = l // CHUNK
    row = lambda v: v.reshape(1, -1).astype(F32)

    tables, dec_f, dec_b = _retention_tables()
    exp_f, exp_b = _expand_matrix(0), _expand_matrix(SSD_HEADS)
    rot = dict(
        expand_f=jnp.asarray(exp_f, BF16),
        expand_b=jnp.asarray(exp_b, BF16),
        expand_fb=jnp.asarray(np.concatenate([exp_f, exp_b], axis=1), BF16),
    )
    h = x.reshape(t, D_MODEL)
    for i in range(depth):
        consts = dict(tables)
        consts.update(rot)
        consts.update(
            conv_w_xs=ssd_conv_w[i][:, :SSD_WIDTH], conv_b_xs=row(ssd_conv_b[i][:SSD_WIDTH]),
            conv_w_bc=ssd_conv_w[i][:, SSD_WIDTH:], conv_b_bc=row(ssd_conv_b[i][SSD_WIDTH:]),
            dt_bias=_pad_lanes(ssd_dt_bias[i]),
            a_row2=_pad_lanes(-jnp.exp(ssd_a_log[i].astype(F32)) * LOG2E),
            ret_norm_w=row(ret_norm_w[i]),
            d_exp=row(jnp.repeat(ssd_d[i], SSD_HEAD_DIM)),
            ssd_norm_w=row(ssd_norm_w[i]),
        )
        w_all = w_in[i].astype(BF16)
        w_dt = jnp.pad(w_all[:, N_MAIN:], ((0, 0), (0, LANES - N_DT)))

        (proj, dt), (wgu16, wd16, wo16, wpg16) = _in_projection(
            h, row(norm_mix_w[i]), w_all, w_dt, positions,
            [(ffn_w_gate[i], ffn_w_up[i], FFN_TILE), ffn_w_down[i], w_out[i], ple_w_gate[i]])
        proj3 = proj.reshape(b, l, N_MAIN)
        dt3 = dt.reshape(b, l, LANES)
        (krt, xbc, rb, sb), _ = _prep_call(proj3, dt3, consts, dec_b, [])
        (h3, hn3), _ = _mix_call(proj3, krt, xbc, dt3, rb, sb, consts, dec_f, [],
                                 h.reshape(b, l, D_MODEL), wo16, row(norm_ffn_w[i]))
        h, hn = h3.reshape(t, D_MODEL), hn3.reshape(t, D_MODEL)
        delta = _ffn_call(hn, wgu16, ffn_conv_w[i], row(ffn_conv_b[i]), wd16, l)
        h = _ple_call(h, delta, p[i].reshape(t, D_PLE), row(ple_norm_w[i]), wpg16,
                      row(ple_b_gate[i]), ple_w_proj[i].astype(BF16), row(final_norm_w),
                      apply_final=(i == depth - 1))
    return h.reshape(b, l, D_MODEL)
```

```python
import functools
import math

import numpy as np
import jax
import jax.numpy as jnp
from jax import lax
from jax.experimental import pallas as pl
from jax.experimental.pallas import tpu as pltpu

F32 = jnp.float32
BF16 = jnp.bfloat16

D_MODEL = 2048
EPS = 1e-6
D_PLE = 256
RET_WIDTH = D_MODEL // 2
RET_HEAD_DIM = 128
RET_HEADS = RET_WIDTH // RET_HEAD_DIM
ROPE_BASE = 10000.0
SSD_WIDTH = D_MODEL - RET_WIDTH
SSD_HEAD_DIM = 64
SSD_HEADS = SSD_WIDTH // SSD_HEAD_DIM
SSD_GROUPS = 2
SSD_HEADS_PER_GROUP = SSD_HEADS // SSD_GROUPS
SSD_STATE = 128
SSD_CONV = 5
SSD_BC = 2 * SSD_GROUPS * SSD_STATE
SSD_CONV_DIM = SSD_WIDTH + SSD_BC
D_FF = (11 * D_MODEL) // 4
FFN_CONV = 3
N_MAIN = 4 * RET_WIDTH + SSD_WIDTH + SSD_CONV_DIM
N_DT = 2 * SSD_HEADS

CHUNK = 128
LANES = 128
HALO = 16
GROUP_W = SSD_WIDTH // SSD_GROUPS
FFN_TILE = 512

COL_Q, COL_K, COL_V, COL_G, COL_Z, COL_XS = 0, 1, 2, 3, 4, 5
COL_BC = (5 * RET_WIDTH + SSD_WIDTH) // SSD_BC
OFF_K_END = 2 * RET_WIDTH

VMEM_LIMIT = 60 * 1024 * 1024
LOG2E = math.log2(math.e)


def _cparams(n_axes):
    return pltpu.CompilerParams(dimension_semantics=("arbitrary",) * n_axes,
                                vmem_limit_bytes=VMEM_LIMIT)


def _rms(xf, w_row):
    ms = jnp.mean(xf * xf, axis=-1, keepdims=True)
    return xf * lax.rsqrt(ms + EPS) * w_row


def _silu(x):
    return x * jax.nn.sigmoid(x)


def _softplus(x):
    return jnp.maximum(x, 0.0) + jnp.log1p(jnp.exp(-jnp.abs(x)))


def _gelu_tanh(x):
    c = math.sqrt(2.0 / math.pi)
    return 0.5 * x * (1.0 + jnp.tanh(c * (x + 0.044715 * (x * x * x))))


def _dot(a, b):
    return jnp.dot(a, b, preferred_element_type=F32)


def _dot_nt(a, b):
    return lax.dot_general(a, b, (((1,), (1,)), ((), ())), preferred_element_type=F32)


def _split3(a):
    hi = a.astype(BF16)
    r1 = a - hi.astype(F32)
    mid = r1.astype(BF16)
    lo = (r1 - mid.astype(F32)).astype(BF16)
    return hi, mid, lo


def _dot_exact_lhs(m01, parts):
    hi, mid, lo = parts
    return _dot(m01, hi) + _dot(m01, mid) + _dot(m01, lo)


def _col_bcast(row):
    return jnp.broadcast_to(row, (LANES, LANES)).T


def _tri_masks():
    r = lax.broadcasted_iota(jnp.int32, (CHUNK, CHUNK), 0)
    c = lax.broadcasted_iota(jnp.int32, (CHUNK, CHUNK), 1)
    lower = r >= c
    tri = jnp.where(lower, 1.0, 0.0).astype(BF16)
    tri_t = jnp.where(r <= c, 1.0, 0.0).astype(BF16)
    return lower, tri, tri_t


def _staggered_row_specs(tm, n_tiles, n_steps, n_split):
    assert n_split < n_steps and tm % n_split == 0
    tq = tm // n_split

    def spec(q):
        def index(i, j):
            nxt = jnp.minimum(i + (j >= n_steps - n_split + q).astype(jnp.int32), n_tiles - 1)
            return (nxt * n_split + q, 0)
        return pl.BlockSpec((tq, D_MODEL), index)

    return [spec(q) for q in range(n_split)]


def _inproj_kernel(n_split, n_col_steps, side_work, *refs):
    x_parts = refs[:n_split]
    (nw_ref, w_ref, wdt_ref, pos_ref, freq_ref, phase_ref,
     proj_ref, dt_ref, hn_ref, cos_ref, sin_ref) = refs[n_split:]
    tq = x_parts[0].shape[0]
    half = RET_HEAD_DIM // 2

    @pl.when(pl.program_id(1) == 0)
    def _():
        for q, x_ref in enumerate(x_parts):
            hn_ref[q * tq:(q + 1) * tq, :] = _rms(x_ref[...], nw_ref[...]).astype(BF16)
        dt_ref[...] = _dot(hn_ref[...], wdt_ref[...])
        ang_t = freq_ref[...] * pos_ref[...].astype(F32) - phase_ref[...]
        cs = jnp.cos(ang_t).T
        sc = pltpu.roll(cs, half, axis=1)
        lane_lo = lax.broadcasted_iota(jnp.int32, cs.shape, 1) < half
        cos_ref[...] = jnp.where(lane_lo, cs, sc)
        sin_ref[...] = jnp.where(lane_lo, -sc, cs)

    tn = w_ref.shape[1]
    kinds = ((COL_G * RET_WIDTH, "plain"), (COL_XS * RET_WIDTH, "silu"), (N_MAIN, "plain"))

    def finish(block, kind):
        if kind == "silu":
            return _silu(block)
        if kind in ("q", "k"):
            block = block * cos_ref[...] + pltpu.roll(block, half, axis=1) * sin_ref[...]
            return block * RET_HEAD_DIM ** -0.5 if kind == "k" else block
        return block

    for step in range(n_col_steps):
        segments, col = [], step * tn
        while col < (step + 1) * tn:
            if col < RET_WIDTH:
                kind, end = "q", col + RET_HEAD_DIM
            elif col < OFF_K_END:
                kind, end = "k", col + RET_HEAD_DIM
            else:
                end, kind = next((e, k) for e, k in kinds if col < e)
            end = min(end, (step + 1) * tn)
            segments.append((col - step * tn, end - step * tn, kind))
            col = end

        @pl.when(pl.program_id(1) == step)
        def _():
            r = _dot(hn_ref[...], w_ref[...])
            for lo, hi, kind in segments:
                proj_ref[:, lo:hi] = finish(r[:, lo:hi], kind).astype(BF16)
            side_work()


def _in_projection(x2, norm_w, w_all, w_dt, positions, cast_weights,
                   tm=1024, tn=N_MAIN // 4, n_split=2):
    t = x2.shape[0]
    n_rows, n_cols = t // tm, N_MAIN // tn
    half = RET_HEAD_DIM // 2
    inv_freq = ROPE_BASE ** (-jnp.arange(half, dtype=F32) / half)
    freq = jnp.broadcast_to(jnp.concatenate([inv_freq, inv_freq])[:, None], (LANES, tm))
    phase = jnp.broadcast_to(
        jnp.concatenate([jnp.zeros((half,), F32), jnp.full((half,), math.pi / 2, F32)])[:, None],
        (LANES, tm))
    in_specs = _staggered_row_specs(tm, n_rows, n_cols, n_split) + [
        pl.BlockSpec((1, D_MODEL), lambda i, j: (0, 0)),
        pl.BlockSpec((D_MODEL, tn), lambda i, j: (0, j)),
        pl.BlockSpec((D_MODEL, LANES), lambda i, j: (0, 0)),
        pl.BlockSpec((None, 1, tm), lambda i, j: (i, 0, 0)),
        pl.BlockSpec((LANES, tm), lambda i, j: (0, 0)),
        pl.BlockSpec((LANES, tm), lambda i, j: (0, 0)),
    ]
    out_specs = [
        pl.BlockSpec((tm, tn), lambda i, j: (i, j)),
        pl.BlockSpec((tm, LANES), lambda i, j: (i, 0)),
    ]
    out_shape = [
        jax.ShapeDtypeStruct((t, N_MAIN), BF16),
        jax.ShapeDtypeStruct((t, LANES), F32),
    ]
    cast_in, cast_out, cast_shapes, jobs, cast_srcs = _cast_slab_specs(
        cast_weights, n_rows * n_cols, n_cols)
    outs = pl.pallas_call(
        functools.partial(_sweep_with_casts, functools.partial(_inproj_kernel, n_split, n_cols),
                          jobs, n_rows * n_cols, len(in_specs), len(out_specs)),
        grid=(n_rows, n_cols),
        in_specs=in_specs + cast_in,
        out_specs=out_specs + cast_out,
        out_shape=out_shape + cast_shapes,
        scratch_shapes=[pltpu.VMEM((tm, D_MODEL), BF16),
                        pltpu.VMEM((tm, LANES), F32), pltpu.VMEM((tm, LANES), F32)],
        compiler_params=_cparams(2),
        name="in_projection",
    )(*([x2] * n_split), norm_w, w_all, w_dt, positions.reshape(n_rows, 1, tm), freq, phase,
      *cast_srcs)
    return outs[:2], outs[2:]


def _conv5_silu(prev, cur, nxt, w_ref, b_ref, has_prev, has_next):
    zero = jnp.zeros_like(prev)
    depth = 2 * LANES
    ext = jnp.concatenate(
        [jnp.where(has_prev, prev, zero), cur, jnp.where(has_next, nxt, zero),
         jnp.zeros((depth - CHUNK - 2 * HALO, cur.shape[1]), cur.dtype)], axis=0)
    r = lax.broadcasted_iota(jnp.int32, (CHUNK, depth), 0)
    c = lax.broadcasted_iota(jnp.int32, (CHUNK, depth), 1)
    pad = SSD_CONV // 2
    acc = b_ref[...] + w_ref[pad:pad + 1, :] * cur.astype(F32)
    for j in range(SSD_CONV):
        if j != pad:
            shift = jnp.where(c == r + (HALO + j - pad), 1.0, 0.0).astype(BF16)
            acc = acc + w_ref[j:j + 1, :] * _dot(shift, ext)
    return _silu(acc)


def _expand_rows(w, tot_row, expand01):
    stacked = jnp.concatenate([w, jnp.broadcast_to(tot_row, (HALO, LANES))], axis=0)
    e = _dot(stacked.astype(BF16), expand01)
    return e[:CHUNK], e[CHUNK:CHUNK + 1]


def _state_increment(bm_f32, xw):
    parts = []
    for g in range(SSD_GROUPS):
        bm_t = bm_f32[:, g * SSD_STATE:(g + 1) * SSD_STATE].T.astype(BF16)
        parts.append(_dot(bm_t, xw[:, g * GROUP_W:(g + 1) * GROUP_W]))
    return jnp.concatenate(parts, axis=1)


def _cast_slab_specs(weights, n_steps, nc):
    in_specs, out_specs, out_shapes, jobs, flat = [], [], [], [], []
    for entry in weights:
        srcs = entry[:2] if isinstance(entry, tuple) else (entry,)
        tile = entry[2] if isinstance(entry, tuple) else 0
        n_rows, n_cols = srcs[0].shape
        rows = next(r for r in range(HALO, n_rows + 1, HALO)
                    if n_rows % r == 0 and n_rows // r <= n_steps)
        n_active = n_rows // rows

        def index(bi, i, _last=n_active - 1):
            return (jnp.minimum(bi * nc + i, _last), 0)

        for w in srcs:
            in_specs.append(pl.BlockSpec((rows, n_cols), index))
            flat.append(w)
        if tile:
            n_tiles = n_cols // tile
            out_specs.append(pl.BlockSpec((n_tiles, rows, 2 * tile),
                                          lambda bi, i, _index=index: (0,) + _index(bi, i)))
            out_shapes.append(jax.ShapeDtypeStruct((n_tiles, n_rows, 2 * tile), BF16))
        else:
            out_specs.append(pl.BlockSpec((rows, n_cols), index))
            out_shapes.append(jax.ShapeDtypeStruct((n_rows, n_cols), BF16))
        jobs.append((n_active, len(srcs), tile))
    return in_specs, out_specs, out_shapes, tuple(jobs), flat


def _sweep_with_casts(body, jobs, n_steps, n_in, n_out, *refs):
    n_src = sum(job[1] for job in jobs)
    ins, refs = refs[:n_in], refs[n_in:]
    srcs, refs = refs[:n_src], refs[n_src:]
    outs, refs = refs[:n_out], refs[n_out:]
    dsts, scratch = refs[:len(jobs)], refs[len(jobs):]
    step = pl.program_id(0) * pl.num_programs(1) + pl.program_id(1)

    def narrow(job_srcs, dst, tile):
        if tile:
            for t in range(dst.shape[0]):
                for k, src in enumerate(job_srcs):
                    dst[t, :, k * tile:(k + 1) * tile] = src[:, t * tile:(t + 1) * tile].astype(BF16)
        else:
            dst[...] = job_srcs[0][...].astype(BF16)

    every_step = []
    for (n_active, n_job_src, tile), dst in zip(jobs, dsts):
        job_srcs, srcs = srcs[:n_job_src], srcs[n_job_src:]
        if n_active == n_steps:
            every_step.append(functools.partial(narrow, job_srcs, dst, tile))
        else:
            pl.when(step < n_active)(functools.partial(narrow, job_srcs, dst, tile))

    def side_work():
        for f in every_step:
            f()

    body(side_work, *ins, *outs, *scratch)


def _per_batch(body, n_batched_in, n_const, side_work, *refs):
    batched_in = refs[:n_batched_in]
    const = refs[n_batched_in:n_batched_in + n_const]
    rest = refs[n_batched_in + n_const:]
    for b in range(batched_in[0].shape[0]):
        body(side_work, *[r.at[b] for r in batched_in], *const, *[r.at[b] for r in rest])


def _all_batches(spec, nb):
    shape = tuple(spec.block_shape)
    if shape[0] is None:
        shape = (nb,) + shape[1:]
    return pl.BlockSpec(shape, lambda bi, i, _index=spec.index_map: _index(0, i))


def _prep_kernel(ret_dec_b, side_work,
                 k_ref, v_ref,
                 xs_ref, xsp_ref, xsn_ref, bc_ref, bcp_ref, bcn_ref, dt_ref,
                 cwx_ref, cbx_ref, cwb_ref, cbb_ref, dtbias_ref, arow2_ref,
                 kb_ref, expand_ref,
                 krt_ref, xbc_ref, rb_ref, sb_ref,
                 rb_state, sb_state):
    i = pl.program_id(1)
    nc = pl.num_programs(1)

    @pl.when(i == 0)
    def _():
        rb_state[...] = jnp.zeros_like(rb_state)
        sb_state[...] = jnp.zeros_like(sb_state)

    side_work()
    has_next = i > 0
    has_prev = i < nc - 1

    def retention_head(h):
        sl = slice(h * RET_HEAD_DIM, (h + 1) * RET_HEAD_DIM)
        kt = k_ref[:, sl].astype(F32).T
        krt_ref[h] = kt.astype(BF16)
        rb_ref[h] = rb_state[h].astype(BF16)
        rb_state[h] = rb_state[h] * ret_dec_b[h] + _dot((kt * kb_ref[h:h + 1, :]).astype(BF16),
                                                        v_ref[:, sl])

    slab_w = 2 * LANES
    n_x_slabs = SSD_WIDTH // slab_w

    def conv_slab(c):
        if c < n_x_slabs:
            prev, cur, nxt, w_ref, b_ref = xsp_ref, xs_ref, xsn_ref, cwx_ref, cbx_ref
        else:
            c -= n_x_slabs
            prev, cur, nxt, w_ref, b_ref = bcp_ref, bc_ref, bcn_ref, cwb_ref, cbb_ref
        cs = slice(c * slab_w, (c + 1) * slab_w)
        return _conv5_silu(prev[:, cs], cur[:, cs], nxt[:, cs], w_ref.at[:, cs], b_ref.at[:, cs],
                           has_prev, has_next)

    slabs = []
    n_slabs = SSD_CONV_DIM // slab_w
    assert n_slabs <= RET_HEADS
    for h in range(RET_HEADS):
        if h < n_slabs:
            slabs.append(conv_slab(h))
        retention_head(h)
    xs = jnp.concatenate(slabs[:n_x_slabs], axis=1)
    bc = jnp.concatenate(slabs[n_x_slabs:], axis=1)
    xbc_ref[:, :SSD_WIDTH] = xs.astype(BF16)
    xbc_ref[:, SSD_WIDTH:] = bc.astype(BF16)

    _, _, tri_t = _tri_masks()
    dt = _softplus(dt_ref[...] + dtbias_ref[...])
    rcs = _dot_exact_lhs(tri_t, _split3(dt * arow2_ref[...]))
    tot = rcs[0:1, :]
    w = jnp.exp2(tot - rcs) * dt
    wexp, cdec = _expand_rows(w, jnp.exp2(tot), expand_ref[...])
    xw = (xs * wexp).astype(BF16)
    ds = _state_increment(bc[:, :SSD_GROUPS * SSD_STATE], xw)
    sb_ref[...] = sb_state[...].astype(BF16)
    sb_state[...] = sb_state[...] * cdec + ds


def _prep_call(proj3, dt3, consts, ret_dec_b, cast_weights):
    b, l, _ = proj3.shape
    nc = l // CHUNK
    rows16 = l // HALO
    per16 = CHUNK // HALO

    def cix(i):
        return nc - 1 - i

    def col(cb):
        return lambda bi, i: (bi, cix(i), cb)

    def prev_halo(cb):
        return lambda bi, i: (bi, jnp.maximum(cix(i) * per16 - 1, 0), cb)

    def next_halo(cb):
        return lambda bi, i: (bi, jnp.minimum((cix(i) + 1) * per16, rows16 - 1), cb)

    def const(shape):
        return pl.BlockSpec(shape, lambda bi, i: (0,) * len(shape))

    in_specs = [
        pl.BlockSpec((None, CHUNK, RET_WIDTH), col(COL_K)),
        pl.BlockSpec((None, CHUNK, RET_WIDTH), col(COL_V)),
        pl.BlockSpec((None, CHUNK, SSD_WIDTH), col(COL_XS)),
        pl.BlockSpec((None, HALO, SSD_WIDTH), prev_halo(COL_XS)),
        pl.BlockSpec((None, HALO, SSD_WIDTH), next_halo(COL_XS)),
        pl.BlockSpec((None, CHUNK, SSD_BC), col(COL_BC)),
        pl.BlockSpec((None, HALO, SSD_BC), prev_halo(COL_BC)),
        pl.BlockSpec((None, HALO, SSD_BC), next_halo(COL_BC)),
        pl.BlockSpec((None, CHUNK, LANES), col(0)),
        const((SSD_CONV, SSD_WIDTH)), const((1, SSD_WIDTH)),
        const((SSD_CONV, SSD_BC)), const((1, SSD_BC)),
        const((1, LANES)), const((1, LANES)),
        const((RET_HEADS, LANES)),
        const((LANES, SSD_WIDTH)),
    ]
    out_specs = [
        pl.BlockSpec((None, None, RET_HEADS, RET_HEAD_DIM, CHUNK),
                     lambda bi, i: (bi, cix(i), 0, 0, 0)),
        pl.BlockSpec((None, CHUNK, SSD_CONV_DIM), col(0)),
        pl.BlockSpec((None, None, RET_HEADS, RET_HEAD_DIM, RET_HEAD_DIM),
                     lambda bi, i: (bi, cix(i), 0, 0, 0)),
        pl.BlockSpec((None, None, SSD_STATE, SSD_WIDTH), lambda bi, i: (bi, cix(i), 0, 0)),
    ]
    out_shape = [
        jax.ShapeDtypeStruct((b, nc, RET_HEADS, RET_HEAD_DIM, CHUNK), BF16),
        jax.ShapeDtypeStruct((b, l, SSD_CONV_DIM), BF16),
        jax.ShapeDtypeStruct((b, nc, RET_HEADS, RET_HEAD_DIM, RET_HEAD_DIM), BF16),
        jax.ShapeDtypeStruct((b, nc, SSD_STATE, SSD_WIDTH), BF16),
    ]
    scratch = [
        pltpu.VMEM((b, RET_HEADS, RET_HEAD_DIM, RET_HEAD_DIM), F32),
        pltpu.VMEM((b, SSD_STATE, SSD_WIDTH), F32),
    ]
    n_batched_in = 9
    n_const = len(in_specs) - n_batched_in
    in_specs = [_all_batches(s, b) for s in in_specs]
    out_specs = [_all_batches(s, b) for s in out_specs]
    cast_in, cast_out, cast_shapes, active, cast_srcs = _cast_slab_specs(cast_weights, nc, nc)
    outs = pl.pallas_call(
        functools.partial(
            _sweep_with_casts,
            functools.partial(_per_batch, functools.partial(_prep_kernel, ret_dec_b),
                              n_batched_in, n_const),
            active, nc, len(in_specs), len(out_specs)),
        grid=(1, nc),
        in_specs=in_specs + cast_in,
        out_specs=out_specs + cast_out,
        out_shape=out_shape + cast_shapes,
        scratch_shapes=scratch,
        compiler_params=_cparams(2),
        name="reverse_sweep",
    )(proj3, proj3, proj3, proj3, proj3, proj3, proj3, proj3, dt3,
      consts["conv_w_xs"], consts["conv_b_xs"], consts["conv_w_bc"], consts["conv_b_bc"],
      consts["dt_bias"], consts["a_row2"], consts["ret_kb"], consts["expand_b"], *cast_srcs)
    return outs[:len(out_specs)], outs[len(out_specs):]


def _mix_kernel(ret_dec_f, side_work,
                qr_ref, kr_ref, krt_ref, v_ref, g_ref, z_ref, xbc_ref, dt_ref, rb_ref, sb_ref,
                mask_ref, qf_ref, qb_ref, kf_ref, retnw_ref,
                dtbias_ref, arow2_ref, expand_ref, expand2_ref, dexp_ref, ssdnw_ref,
                out_ref,
                rf_state, sf_state):
    i = pl.program_id(1)

    @pl.when(i == 0)
    def _():
        rf_state[...] = jnp.zeros_like(rf_state)
        sf_state[...] = jnp.zeros_like(sf_state)

    side_work()
    lower, tri, tri_t = _tri_masks()
    dt = _softplus(dt_ref[...] + dtbias_ref[...])
    parts = _split3(dt * arow2_ref[...])
    lane = lax.broadcasted_iota(jnp.int32, (CHUNK, LANES), 1)
    prefix = _dot_exact_lhs(tri, parts)
    acs = jnp.where(lane < SSD_HEADS, prefix, _dot_exact_lhs(tri_t, parts))
    acs_t = acs.T
    src_t = (acs - jnp.log2(dt)).T
    edge = _dot(jnp.exp2(acs).astype(BF16), expand2_ref[...])
    xs16 = xbc_ref[:, :SSD_WIDTH]
    bm = xbc_ref[:, SSD_WIDTH:SSD_WIDTH + SSD_GROUPS * SSD_STATE]
    cm = xbc_ref[:, SSD_WIDTH + SSD_GROUPS * SSD_STATE:]
    lane_lo = lane < SSD_HEAD_DIM

    cbs, y_offs = [], []
    for g in range(SSD_GROUPS):
        gs = slice(g * SSD_STATE, (g + 1) * SSD_STATE)
        gw = slice(g * GROUP_W, (g + 1) * GROUP_W)
        gwb = slice(SSD_WIDTH + g * GROUP_W, SSD_WIDTH + (g + 1) * GROUP_W)
        cm_g = cm[:, gs]
        cbs.append(_dot_nt(cm_g, bm[:, gs]))
        y_offs.append(edge[:, gw] * _dot(cm_g, sf_state[:, gw].astype(BF16))
                      + edge[:, gwb] * _dot(cm_g, sb_ref[:, gw]))

    def decay_matrix(e):
        eb = SSD_HEADS + e
        dst = jnp.where(lower, _col_bcast(acs_t[e:e + 1, :]), _col_bcast(acs_t[eb:eb + 1, :]))
        src = jnp.where(lower, src_t[e:e + 1, :], src_t[eb:eb + 1, :])
        return (cbs[e // SSD_HEADS_PER_GROUP] * jnp.exp2(dst - src)).astype(BF16)

    ys = []
    for k in range(RET_HEADS):
        sl = slice(k * RET_HEAD_DIM, (k + 1) * RET_HEAD_DIM)
        ms = [decay_matrix(2 * k), decay_matrix(2 * k + 1)]
        qh = qr_ref[:, sl]
        kh = kr_ref[:, sl]
        vh = v_ref[:, sl]
        s = (_dot_nt(qh, kh) * mask_ref[k]).astype(BF16)
        lhs = jnp.concatenate([s, qh * qf_ref[k], qh * qb_ref[k]], axis=1)
        rhs = jnp.concatenate([vh, rf_state[k].astype(BF16), rb_ref[k]], axis=0)
        o = _dot(lhs, rhs)
        rf_state[k] = rf_state[k] * ret_dec_f[k] + _dot(krt_ref[k], vh * kf_ref[k])
        xs_pair = xs16[:, k * LANES:(k + 1) * LANES]
        prods = [_dot(m, xs_pair) for m in ms]
        o = _rms(o, retnw_ref[:, sl])
        out_ref[:, sl] = (g_ref[:, sl].astype(F32) * o).astype(BF16)
        g, q = divmod(k, SSD_HEADS_PER_GROUP // 2)
        ys.append(jnp.where(lane_lo, prods[0], prods[1]) + y_offs[g][:, q * LANES:(q + 1) * LANES])
    y = jnp.concatenate(ys, axis=1)

    xs = xs16.astype(F32)
    y = (y + dexp_ref[...] * xs) * z_ref[...].astype(F32)
    for g in range(SSD_GROUPS):
        gs = slice(g * GROUP_W, (g + 1) * GROUP_W)
        out_ref[:, RET_WIDTH + g * GROUP_W:RET_WIDTH + (g + 1) * GROUP_W] = _rms(
            y[:, gs], ssdnw_ref[:, gs]).astype(BF16)

    tot = prefix[CHUNK - 1:CHUNK, :]
    w = jnp.exp2(tot - prefix) * dt
    wexp, cdec = _expand_rows(w, jnp.exp2(tot), expand_ref[...])
    xw = (xs * wexp).astype(BF16)
    ds = _state_increment(bm.astype(F32), xw)
    sf_state[...] = sf_state[...] * cdec + ds


def _mix_call(proj3, krt, xbc, dt3, rb, sb, consts, ret_dec_f, cast_weights):
    b, l, _ = proj3.shape
    nc = l // CHUNK

    def col(cb):
        return lambda bi, i: (bi, i, cb)

    def const(shape):
        return pl.BlockSpec(shape, lambda bi, i: (0,) * len(shape))

    in_specs = [
        pl.BlockSpec((None, CHUNK, RET_WIDTH), col(COL_Q)),
        pl.BlockSpec((None, CHUNK, RET_WIDTH), col(COL_K)),
        pl.BlockSpec((None, None, RET_HEADS, RET_HEAD_DIM, CHUNK), lambda bi, i: (bi, i, 0, 0, 0)),
        pl.BlockSpec((None, CHUNK, RET_WIDTH), col(COL_V)),
        pl.BlockSpec((None, CHUNK, RET_WIDTH), col(COL_G)),
        pl.BlockSpec((None, CHUNK, SSD_WIDTH), col(COL_Z)),
        pl.BlockSpec((None, CHUNK, SSD_CONV_DIM), col(0)),
        pl.BlockSpec((None, CHUNK, LANES), col(0)),
        pl.BlockSpec((None, None, RET_HEADS, RET_HEAD_DIM, RET_HEAD_DIM),
                     lambda bi, i: (bi, i, 0, 0, 0)),
        pl.BlockSpec((None, None, SSD_STATE, SSD_WIDTH), lambda bi, i: (bi, i, 0, 0)),
        const((RET_HEADS, CHUNK, CHUNK)), const((RET_HEADS, CHUNK, LANES)),
        const((RET_HEADS, CHUNK, LANES)), const((RET_HEADS, CHUNK, LANES)), const((1, RET_WIDTH)),
        const((1, LANES)), const((1, LANES)), const((LANES, SSD_WIDTH)),
        const((LANES, 2 * SSD_WIDTH)), const((1, SSD_WIDTH)), const((1, SSD_WIDTH)),
    ]
    cast_in, cast_out, cast_shapes, active, cast_srcs = _cast_slab_specs(cast_weights, b * nc, nc)
    outs = pl.pallas_call(
        functools.partial(_sweep_with_casts, functools.partial(_mix_kernel, ret_dec_f),
                          active, b * nc, len(in_specs), 1),
        grid=(b, nc),
        in_specs=in_specs + cast_in,
        out_specs=[pl.BlockSpec((None, CHUNK, D_MODEL), col(0))] + cast_out,
        out_shape=[jax.ShapeDtypeStruct((b, l, D_MODEL), BF16)] + cast_shapes,
        scratch_shapes=[
            pltpu.VMEM((RET_HEADS, RET_HEAD_DIM, RET_HEAD_DIM), F32),
            pltpu.VMEM((SSD_STATE, SSD_WIDTH), F32),
        ],
        compiler_params=_cparams(2),
        name="forward_sweep",
    )(proj3, proj3, krt, proj3, proj3, proj3, xbc, dt3, rb, sb,
      consts["ret_mask"], consts["ret_qf"], consts["ret_qb"], consts["ret_kf"],
      consts["ret_norm_w"], consts["dt_bias"], consts["a_row2"], consts["expand_f"],
      consts["expand_fb"], consts["d_exp"], consts["ssd_norm_w"], *cast_srcs)
    return outs[0], outs[1:]


def _outproj_kernel(mix_ref, w_ref, x_ref, nw_ref, h_ref, hn_ref):
    h = x_ref[...] + _dot(mix_ref[...], w_ref[...])
    h_ref[...] = h
    hn_ref[...] = _rms(h, nw_ref[...]).astype(BF16)


def _out_projection(mix2, w_out, x2, ffn_norm_w, tm=512):
    t = x2.shape[0]
    return pl.pallas_call(
        _outproj_kernel,
        grid=(t // tm,),
        in_specs=[
            pl.BlockSpec((tm, D_MODEL), lambda i: (i, 0)),
            pl.BlockSpec((D_MODEL, D_MODEL), lambda i: (0, 0)),
            pl.BlockSpec((tm, D_MODEL), lambda i: (i, 0)),
            pl.BlockSpec((1, D_MODEL), lambda i: (0, 0)),
        ],
        out_specs=[
            pl.BlockSpec((tm, D_MODEL), lambda i: (i, 0)),
            pl.BlockSpec((tm, D_MODEL), lambda i: (i, 0)),
        ],
        out_shape=[
            jax.ShapeDtypeStruct((t, D_MODEL), F32),
            jax.ShapeDtypeStruct((t, D_MODEL), BF16),
        ],
        compiler_params=_cparams(1),
        name="out_projection",
    )(mix2, w_out, x2, ffn_norm_w)


def _ffn_kernel(tiles_per_seq, hn_ref, hp_ref, hx_ref, wgu_ref, cw_ref, cb_ref, wd_ref,
                o_ref, hbuf, gu_s):
    i = pl.program_id(0)
    j = pl.program_id(1)
    tm = hn_ref.shape[0]
    tf = cw_ref.shape[1]

    def column_step(first):
        if first:
            pos_in_seq = i % tiles_per_seq
            zero = jnp.zeros((HALO, D_MODEL), BF16)
            hbuf[0:HALO, :] = jnp.where(pos_in_seq > 0, hp_ref[...], zero)
            hbuf[HALO:HALO + tm, :] = hn_ref[...]
            hbuf[HALO + tm:, :] = jnp.where(pos_in_seq < tiles_per_seq - 1, hx_ref[...], zero)
        gu_s[...] = _dot(hbuf[...], wgu_ref[...])
        pad = FFN_CONV // 2
        gate = cb_ref[...]
        for t in range(FFN_CONV):
            o = HALO + t - pad
            gate = gate + cw_ref[t:t + 1, :] * gu_s[o:o + tm, :tf]
        act = (_gelu_tanh(gate) * gu_s[HALO:HALO + tm, tf:]).astype(BF16)
        if first:
            o_ref[...] = _dot(act, wd_ref[...])
        else:
            o_ref[...] += _dot(act, wd_ref[...])

    pl.when(j == 0)(functools.partial(column_step, True))
    pl.when(j > 0)(functools.partial(column_step, False))


def _ffn_call(hn2, wgu, conv_w, conv_b, wd, seq_len, tm=1024):
    t = hn2.shape[0]
    tf = wgu.shape[2] // 2
    per16 = tm // HALO
    rows16 = t // HALO
    return pl.pallas_call(
        functools.partial(_ffn_kernel, seq_len // tm),
        grid=(t // tm, D_FF // tf),
        in_specs=[
            pl.BlockSpec((tm, D_MODEL), lambda i, j: (i, 0)),
            pl.BlockSpec((HALO, D_MODEL), lambda i, j: (jnp.maximum(i * per16 - 1, 0), 0)),
            pl.BlockSpec((HALO, D_MODEL), lambda i, j: (jnp.minimum((i + 1) * per16, rows16 - 1), 0)),
            pl.BlockSpec((None, D_MODEL, 2 * tf), lambda i, j: (j, 0, 0)),
            pl.BlockSpec((FFN_CONV, tf), lambda i, j: (0, j)),
            pl.BlockSpec((1, tf), lambda i, j: (0, j)),
            pl.BlockSpec((tf, D_MODEL), lambda i, j: (j, 0)),
        ],
        out_specs=pl.BlockSpec((tm, D_MODEL), lambda i, j: (i, 0)),
        out_shape=jax.ShapeDtypeStruct((t, D_MODEL), F32),
        scratch_shapes=[
            pltpu.VMEM((tm + 2 * HALO, D_MODEL), BF16),
            pltpu.VMEM((tm + 2 * HALO, 2 * tf), F32),
        ],
        compiler_params=_cparams(2),
        name="conv_glu_ffn",
    )(hn2, hn2, hn2, wgu, conv_w, conv_b, wd)


def _ple_kernel(apply_final, h_ref, d_ref, p_ref, nw_ref, wg_ref, bg_ref, wp_ref, fw_ref, o_ref):
    h = h_ref[...] + d_ref[...]
    hn = _rms(h, nw_ref[...]).astype(BF16)
    gate = jax.nn.sigmoid(_dot(hn, wg_ref[...]) + bg_ref[...])
    h = h + gate * _dot(p_ref[...].astype(BF16), wp_ref[...])
    if apply_final:
        h = _rms(h, fw_ref[...])
    o_ref[...] = h


def _ple_call(h2, delta2, p2, norm_w, wg, bg, wp, final_w, apply_final, tm=512):
    t = h2.shape[0]
    return pl.pallas_call(
        functools.partial(_ple_kernel, apply_final),
        grid=(t // tm,),
        in_specs=[
            pl.BlockSpec((tm, D_MODEL), lambda i: (i, 0)),
            pl.BlockSpec((tm, D_MODEL), lambda i: (i, 0)),
            pl.BlockSpec((tm, D_PLE), lambda i: (i, 0)),
            pl.BlockSpec((1, D_MODEL), lambda i: (0, 0)),
            pl.BlockSpec((D_MODEL, D_MODEL), lambda i: (0, 0)),
            pl.BlockSpec((1, D_MODEL), lambda i: (0, 0)),
            pl.BlockSpec((D_PLE, D_MODEL), lambda i: (0, 0)),
            pl.BlockSpec((1, D_MODEL), lambda i: (0, 0)),
        ],
        out_specs=pl.BlockSpec((tm, D_MODEL), lambda i: (i, 0)),
        out_shape=jax.ShapeDtypeStruct((t, D_MODEL), F32),
        compiler_params=_cparams(1),
        name="ple_gate",
    )(h2, delta2, p2, norm_w, wg, bg, wp, final_w)


def _retention_tables():
    hh = np.arange(RET_HEADS, dtype=np.float64)
    lf = np.log1p(-np.exp2(-5.0 - hh))
    lb = np.log1p(-np.exp2(-5.5 - hh))
    idx = np.arange(CHUNK, dtype=np.float64)
    dist = idx[:, None] - idx[None, :]
    mask = np.where(dist >= 0, np.exp(lf[:, None, None] * np.abs(dist)),
                    np.exp(lb[:, None, None] * np.abs(dist)))
    ones = np.ones((1, 1, LANES))
    qf = np.exp(lf[:, None] * (idx + 1.0)[None, :])[:, :, None] * ones
    qb = np.exp(lb[:, None] * (CHUNK - idx)[None, :])[:, :, None] * ones
    kf = np.exp(lf[:, None] * (CHUNK - 1.0 - idx)[None, :])[:, :, None] * ones
    kb = np.exp(lb[:, None] * idx[None, :])
    dec_f = tuple(float(v) for v in np.exp(lf * CHUNK))
    dec_b = tuple(float(v) for v in np.exp(lb * CHUNK))
    f = lambda a: jnp.asarray(a, F32)
    h = lambda a: jnp.asarray(a, BF16)
    return dict(ret_mask=f(mask), ret_qf=h(qf), ret_qb=h(qb), ret_kf=h(kf), ret_kb=f(kb)), dec_f, dec_b


def _expand_matrix(first_row):
    e = np.zeros((LANES, SSD_WIDTH), np.float32)
    for h in range(SSD_HEADS):
        e[first_row + h, h * SSD_HEAD_DIM:(h + 1) * SSD_HEAD_DIM] = 1.0
    return e


def _pad_lanes(v):
    return jnp.pad(v.reshape(1, -1), ((0, 0), (0, LANES - v.size)))


def kernel(x, p, positions, norm_mix_w, w_in, ret_norm_w, ssd_conv_w, ssd_conv_b, ssd_dt_bias,
           ssd_a_log, ssd_d, ssd_norm_w, w_out, norm_ffn_w, ffn_w_gate, ffn_w_up, ffn_conv_w,
           ffn_conv_b, ffn_w_down, ple_norm_w, ple_w_gate, ple_b_gate, ple_w_proj, final_norm_w):
    b, l, _ = x.shape
    depth = w_in.shape[0]
    t = b * l
    nc = l // CHUNK
    row = lambda v: v.reshape(1, -1).astype(F32)

    tables, dec_f, dec_b = _retention_tables()
    exp_f, exp_b = _expand_matrix(0), _expand_matrix(SSD_HEADS)
    rot = dict(
        expand_f=jnp.asarray(exp_f, BF16),
        expand_b=jnp.asarray(exp_b, BF16),
        expand_fb=jnp.asarray(np.concatenate([exp_f, exp_b], axis=1), BF16),
    )
    h = x.reshape(t, D_MODEL)
    for i in range(depth):
        consts = dict(tables)
        consts.update(rot)
        consts.update(
            conv_w_xs=ssd_conv_w[i][:, :SSD_WIDTH], conv_b_xs=row(ssd_conv_b[i][:SSD_WIDTH]),
            conv_w_bc=ssd_conv_w[i][:, SSD_WIDTH:], conv_b_bc=row(ssd_conv_b[i][SSD_WIDTH:]),
            dt_bias=_pad_lanes(ssd_dt_bias[i]),
            a_row2=_pad_lanes(-jnp.exp(ssd_a_log[i].astype(F32)) * LOG2E),
            ret_norm_w=row(ret_norm_w[i]),
            d_exp=row(jnp.repeat(ssd_d[i], SSD_HEAD_DIM)),
            ssd_norm_w=row(ssd_norm_w[i]),
        )
        w_all = w_in[i].astype(BF16)
        w_dt = jnp.pad(w_all[:, N_MAIN:], ((0, 0), (0, LANES - N_DT)))

        (proj, dt), (wgu16, wd16, wo16, wpg16) = _in_projection(
            h, row(norm_mix_w[i]), w_all, w_dt, positions,
            [(ffn_w_gate[i], ffn_w_up[i], FFN_TILE), ffn_w_down[i], w_out[i], ple_w_gate[i]])
        proj3 = proj.reshape(b, l, N_MAIN)
        dt3 = dt.reshape(b, l, LANES)
        (krt, xbc, rb, sb), _ = _prep_call(proj3, dt3, consts, dec_b, [])
        mix, _ = _mix_call(proj3, krt, xbc, dt3, rb, sb, consts, dec_f, [])
        h, hn = _out_projection(mix.reshape(t, D_MODEL), wo16, h, row(norm_ffn_w[i]))
        delta = _ffn_call(hn, wgu16, ffn_conv_w[i], row(ffn_conv_b[i]), wd16, l)
        h = _ple_call(h, delta, p[i].reshape(t, D_PLE), row(ple_norm_w[i]), wpg16,
                      row(ple_b_gate[i]), ple_w_proj[i].astype(BF16), row(final_norm_w),
                      apply_final=(i == depth - 1))
    return h.reshape(b, l, D_MODEL)
```
